```python
import math
import jax, jax.numpy as jnp
from jax import lax
import numpy as np

D_MODEL = 1024
BATCH = 4
SEQ = 4096
DEPTH = 4
DEC_BATCH = 2
DEC_SEQ = 16384
PAST_LEN = 128

GRID_W = 64
N_MIXERS = 3
D_FF = 4 * D_MODEL
EPS = 1e-6
ROPE_THETA = 10000.0
Q_BLOCK = 128
A_HEAD_DIM = 64
A_HEADS = D_MODEL // A_HEAD_DIM
A_WIN_ROWS = 8
A_WIN_COLS = 16
B_HEAD_DIM = 64
B_HEADS = D_MODEL // B_HEAD_DIM
B_PAIRS = ((128, 1), (512, 4), (2048, 16))
B_GROUPS = len(B_PAIRS)
C_HEAD_DIM = 128
C_Q_HEADS = D_MODEL // C_HEAD_DIM
C_KV_HEADS = 2
C_GROUP = C_Q_HEADS // C_KV_HEADS
C_QKV_WIDTH = (C_Q_HEADS + 2 * C_KV_HEADS) * C_HEAD_DIM
N_A = (DEPTH + 2) // 3
N_B = (DEPTH + 1) // 3
N_C = DEPTH // 3

kernel_name = "hybrid_bidir_encoder_natten_dilated_axial_gqa"


def rms_norm(x, g):
    x32 = x.astype(jnp.float32)
    y = x32 * lax.rsqrt(jnp.mean(x32 * x32, axis=-1, keepdims=True) + EPS)
    return (y * g.astype(jnp.float32)).astype(x.dtype)


def rope(x, pos):
    dim = x.shape[-1]
    half = dim // 2
    inv = ROPE_THETA ** (-jnp.arange(half, dtype=jnp.float32) / half)
    ang = pos.astype(jnp.float32)[:, None] * inv[None, :]
    cos = jnp.cos(ang).astype(x.dtype)
    sin = jnp.sin(ang).astype(x.dtype)
    x1, x2 = x[..., :half], x[..., half:]
    return jnp.concatenate([x1 * cos - x2 * sin, x1 * sin + x2 * cos], axis=-1)


def neighbourhood_attention(h, w_qkv, rpb, w_o):
    bn, seq_len, _ = h.shape
    rows = seq_len // GRID_W
    kr = min(A_WIN_ROWS, rows)
    kc = A_WIN_COLS
    qkv = (h @ w_qkv).reshape(bn, seq_len, 3, A_HEADS, A_HEAD_DIM)
    q = jnp.moveaxis(qkv[:, :, 0], 1, 2)
    k = jnp.moveaxis(qkv[:, :, 1], 1, 2)
    v = jnp.moveaxis(qkv[:, :, 2], 1, 2)
    cols = np.arange(GRID_W)
    col_start = np.clip(cols - kc // 2, 0, GRID_W - kc)
    col_idx = col_start[:, None] + np.arange(kc)[None, :]
    dc_idx = col_idx - cols[:, None] + (A_WIN_COLS - 1)
    scale = A_HEAD_DIM ** -0.5

    def row_fn(r):
        rs = jnp.clip(r - kr // 2, 0, rows - kr)
        key_rows = rs + jnp.arange(kr)
        kidx = (key_rows[None, :, None] * GRID_W + col_idx[:, None, :]).reshape(GRID_W, kr * kc)
        qr = lax.dynamic_slice_in_dim(q, r * GRID_W, GRID_W, axis=2)
        kg = k[:, :, kidx]
        vg = v[:, :, kidx]
        dr_idx = key_rows - r + (A_WIN_ROWS - 1)
        bias = rpb[:, dr_idx[None, :, None], dc_idx[:, None, :]].reshape(A_HEADS, GRID_W, kr * kc)
        s = jnp.einsum("bhqd,bhqkd->bhqk", qr, kg).astype(jnp.float32) * scale + bias.astype(jnp.float32)
        p = jax.nn.softmax(s, axis=-1).astype(v.dtype)
        return jnp.einsum("bhqk,bhqkd->bhqd", p, vg)

    o = lax.map(row_fn, jnp.arange(rows))
    o = jnp.transpose(o, (1, 0, 3, 2, 4)).reshape(bn, seq_len, A_HEADS * A_HEAD_DIM)
    return o @ w_o


def dilated_attention(h, w_qkv, w_o, pos):
    bn, seq_len, _ = h.shape
    nb = seq_len // Q_BLOCK
    qkv = (h @ w_qkv).reshape(bn, seq_len, B_GROUPS, 3, B_HEADS, B_HEAD_DIM)
    scale = B_HEAD_DIM ** -0.5
    outs, lses = [], []
    for g, (win, dil) in enumerate(B_PAIRS):
        q = rope(jnp.moveaxis(qkv[:, :, g, 0], 1, 2), pos)
        k = rope(jnp.moveaxis(qkv[:, :, g, 1], 1, 2), pos)
        v = jnp.moveaxis(qkv[:, :, g, 2], 1, 2)
        n_side = win // (2 * dil)
        offs = dil * np.arange(-n_side, n_side + 1)

        def blk(i, q=q, k=k, v=v, offs=offs):
            t0 = i * Q_BLOCK
            idx = t0 + jnp.arange(Q_BLOCK)[:, None] + offs[None, :]
            valid = (idx >= 0) & (idx < seq_len)
            idxc = jnp.clip(idx, 0, seq_len - 1)
            qb = lax.dynamic_slice_in_dim(q, t0, Q_BLOCK, axis=2)
            kb = k[:, :, idxc]
            vb = v[:, :, idxc]
            s = jnp.einsum("bhqd,bhqkd->bhqk", qb, kb).astype(jnp.float32) * scale
            s = jnp.where(valid[None, None], s, -jnp.inf)
            lse = jax.nn.logsumexp(s, axis=-1)
            p = jnp.exp(s - lse[..., None]).astype(v.dtype)
            return jnp.einsum("bhqk,bhqkd->bhqd", p, vb), lse

        o, lse = lax.map(blk, jnp.arange(nb))
        outs.append(jnp.transpose(o, (1, 2, 0, 3, 4)).reshape(bn, B_HEADS, seq_len, B_HEAD_DIM))
        lses.append(jnp.transpose(lse, (1, 2, 0, 3)).reshape(bn, B_HEADS, seq_len))
    wgt = jax.nn.softmax(jnp.stack(lses), axis=0).astype(h.dtype)
    o = jnp.einsum("gbhl,gbhld->bhld", wgt, jnp.stack(outs))
    o = jnp.moveaxis(o, 1, 2).reshape(bn, seq_len, B_HEADS * B_HEAD_DIM)
    return o @ w_o


def axial_rope(x, row, col):
    half = x.shape[-1] // 2
    return jnp.concatenate([rope(x[..., :half], row), rope(x[..., half:], col)], axis=-1)


def axial_gqa_attention(h, w_qkv, q_g, k_g, w_o, row, col):
    bn, seq_len, _ = h.shape
    nb = seq_len // Q_BLOCK
    qw = C_Q_HEADS * C_HEAD_DIM
    kw = C_KV_HEADS * C_HEAD_DIM
    proj = h @ w_qkv
    q = proj[..., :qw].reshape(bn, seq_len, C_Q_HEADS, C_HEAD_DIM)
    k = proj[..., qw:qw + kw].reshape(bn, seq_len, C_KV_HEADS, C_HEAD_DIM)
    v = proj[..., qw + kw:].reshape(bn, seq_len, C_KV_HEADS, C_HEAD_DIM)
    q = axial_rope(jnp.moveaxis(rms_norm(q, q_g), 1, 2), row, col)
    k = axial_rope(jnp.moveaxis(rms_norm(k, k_g), 1, 2), row, col)
    v = jnp.moveaxis(v, 1, 2)
    q = q.reshape(bn, C_KV_HEADS, C_GROUP, seq_len, C_HEAD_DIM)
    scale = C_HEAD_DIM ** -0.5

    def blk(i):
        qb = lax.dynamic_slice_in_dim(q, i * Q_BLOCK, Q_BLOCK, axis=3)
        s = jnp.einsum("bkgqd,bksd->bkgqs", qb, k).astype(jnp.float32) * scale
        p = jax.nn.softmax(s, axis=-1).astype(v.dtype)
        return jnp.einsum("bkgqs,bksd->bkgqd", p, v)

    o = lax.map(blk, jnp.arange(nb))
    o = jnp.transpose(o, (1, 0, 4, 2, 3, 5)).reshape(bn, seq_len, qw)
    return o @ w_o


def sq_relu_mlp(h, w1, w2):
    return jnp.square(jax.nn.relu(h @ w1)) @ w2


def trunk(x, c, w_mod, b_mod, norm_g, final_g, a_w_qkv, a_rpb, a_w_o, b_w_qkv, b_w_o,
          c_w_qkv, c_q_g, c_k_g, c_w_o, mlp_w1, mlp_w2):
    seq_len = x.shape[1]
    t = jnp.arange(seq_len)
    row = t // GRID_W
    col = t % GRID_W
    c_act = jax.nn.silu(c)
    for i in range(DEPTH):
        mod = (c_act @ w_mod[i] + b_mod[i])[:, None, :]
        sh1, sc1, g1, sh2, sc2, g2 = jnp.split(mod, 6, axis=-1)
        h = rms_norm(x, norm_g[i, 0]) * (1 + sc1) + sh1
        kind, j = i % N_MIXERS, i // N_MIXERS
        if kind == 0:
            m = neighbourhood_attention(h, a_w_qkv[j], a_rpb[j], a_w_o[j])
        elif kind == 1:
            m = dilated_attention(h, b_w_qkv[j], b_w_o[j], t)
        else:
            m = axial_gqa_attention(h, c_w_qkv[j], c_q_g[j], c_k_g[j], c_w_o[j], row, col)
        x = x + g1 * m
        h = rms_norm(x, norm_g[i, 1]) * (1 + sc2) + sh2
        x = x + g2 * sq_relu_mlp(h, mlp_w1[i], mlp_w2[i])
    return rms_norm(x, final_g)


def setup_inputs(seed: int = 0) -> dict:
    key = jax.random.key(seed)
    ks = jax.random.split(key, 20)
    D = D_MODEL

    def nrm(k, shape, scale):
        return jax.random.normal(k, shape, jnp.float32) * scale

    return {
        "x_prompt": nrm(ks[0], (BATCH, SEQ, D), 1.0),
        "x_sample": nrm(ks[1], (DEC_BATCH, DEC_SEQ, D), 1.0),
        "c_prompt": nrm(ks[2], (BATCH, D), 1.0),
        "c_sample": nrm(ks[3], (DEC_BATCH, D), 1.0),
        "w_mod": nrm(ks[4], (DEPTH, D, 6 * D), 0.5 * D ** -0.5),
        "b_mod": nrm(ks[5], (DEPTH, 6 * D), 0.02),
        "norm_g": 1.0 + nrm(ks[6], (DEPTH, 2, D), 0.02),
        "final_g": 1.0 + nrm(ks[7], (D,), 0.02),
        "a_w_qkv": nrm(ks[8], (N_A, D, 3 * A_HEADS * A_HEAD_DIM), D ** -0.5),
        "a_rpb": nrm(ks[9], (N_A, A_HEADS, 2 * A_WIN_ROWS - 1, 2 * A_WIN_COLS - 1), 0.1),
        "a_w_o": nrm(ks[10], (N_A, A_HEADS * A_HEAD_DIM, D), (A_HEADS * A_HEAD_DIM) ** -0.5),
        "b_w_qkv": nrm(ks[11], (N_B, D, B_GROUPS * 3 * B_HEADS * B_HEAD_DIM), D ** -0.5),
        "b_w_o": nrm(ks[12], (N_B, B_HEADS * B_HEAD_DIM, D), (B_HEADS * B_HEAD_DIM) ** -0.5),
        "c_w_qkv": nrm(ks[13], (N_C, D, C_QKV_WIDTH), D ** -0.5),
        "c_q_g": 1.0 + nrm(ks[14], (N_C, C_HEAD_DIM), 0.02),
        "c_k_g": 1.0 + nrm(ks[15], (N_C, C_HEAD_DIM), 0.02),
        "c_w_o": nrm(ks[16], (N_C, C_Q_HEADS * C_HEAD_DIM, D), (C_Q_HEADS * C_HEAD_DIM) ** -0.5),
        "mlp_w1": nrm(ks[17], (DEPTH, D, D_FF), D ** -0.5),
        "mlp_w2": nrm(ks[18], (DEPTH, D_FF, D), D_FF ** -0.5),
    }


def reference(x_prompt, x_sample, c_prompt, c_sample, w_mod, b_mod, norm_g, final_g,
              a_w_qkv, a_rpb, a_w_o, b_w_qkv, b_w_o, c_w_qkv, c_q_g, c_k_g, c_w_o,
              mlp_w1, mlp_w2):
    y_prompt = trunk(x_prompt, c_prompt, w_mod, b_mod, norm_g, final_g, a_w_qkv, a_rpb, a_w_o,
                     b_w_qkv, b_w_o, c_w_qkv, c_q_g, c_k_g, c_w_o, mlp_w1, mlp_w2)
    y_sample = trunk(x_sample, c_sample, w_mod, b_mod, norm_g, final_g, a_w_qkv, a_rpb, a_w_o,
                     b_w_qkv, b_w_o, c_w_qkv, c_q_g, c_k_g, c_w_o, mlp_w1, mlp_w2)
    return (y_prompt, y_sample)
```

```python
import functools

import numpy as np
import jax
import jax.numpy as jnp
from jax import lax
from jax.experimental import pallas as pl
from jax.experimental.pallas import tpu as pltpu

F32 = jnp.float32
BF16 = jnp.bfloat16

D_MODEL = 1024
DEPTH = 4
GRID_W = 64
D_FF = 4 * D_MODEL
EPS = 1e-6
ROPE_THETA = 10000.0
N_MIXERS = 3
HEAD_DIM_AB = 64
HEADS_AB = D_MODEL // HEAD_DIM_AB
A_WIN_ROWS = 8
A_WIN_COLS = 16
B_PAIRS = ((128, 1), (512, 4), (2048, 16))
B_SIDE = 64
C_HEAD_DIM = 128
C_Q_HEADS = 8
C_KV_HEADS = 2
C_GROUP = C_Q_HEADS // C_KV_HEADS
C_QKV_WIDTH = (C_Q_HEADS + 2 * C_KV_HEADS) * C_HEAD_DIM
ROPE_HALF = 32
LANES = 128
MOD_ROWS = 8
MASK_NEG = -1e30
VMEM_LIMIT = 56 * 1024 * 1024

TM_QKV = 1024
TN_QKV = 1024
TM_POST = 512
TF_POST = 1024
A_TQ = 4 * GRID_W
B_TQ = 256
C_TQ = 512
C_TK = 1024


def _cparams(sem):
    return pltpu.CompilerParams(dimension_semantics=sem, vmem_limit_bytes=VMEM_LIMIT)


def _mod_kernel(c_ref, w_ref, b_ref, o_ref):
    c = c_ref[...]
    act = (c / (1.0 + jnp.exp(-c))).astype(BF16)
    o_ref[0] = jnp.dot(act, w_ref[0].astype(BF16), preferred_element_type=F32) + b_ref[0]


def _modulation(c_all, w_mod, b_mod):
    tn = 1536
    n = 6 * D_MODEL
    return pl.pallas_call(
        _mod_kernel,
        grid=(DEPTH, n // tn),
        in_specs=[
            pl.BlockSpec((MOD_ROWS, D_MODEL), lambda l, j: (0, 0)),
            pl.BlockSpec((1, D_MODEL, tn), lambda l, j: (l, 0, j)),
            pl.BlockSpec((1, 1, tn), lambda l, j: (l, 0, j)),
        ],
        out_specs=pl.BlockSpec((1, MOD_ROWS, tn), lambda l, j: (l, 0, j)),
        out_shape=jax.ShapeDtypeStruct((DEPTH, MOD_ROWS, n), F32),
        compiler_params=_cparams(("parallel", "parallel")),
        name="adaln_modulation",
    )(c_all, w_mod, b_mod.reshape(DEPTH, 1, n))


def _norm_mod(x, gain, scale, shift):
    ms = jnp.mean(x * x, axis=-1, keepdims=True)
    return (x * lax.rsqrt(ms + EPS) * gain) * (1.0 + scale) + shift


def _rope128(x, cos, sin_signed, low_half):
    up = pltpu.roll(x, LANES - ROPE_HALF, 1)
    down = pltpu.roll(x, ROPE_HALF, 1)
    return x * cos + jnp.where(low_half, up, down) * sin_signed


def _low_half_mask():
    lane = lax.broadcasted_iota(jnp.int32, (1, LANES), 1)
    return (lane % (2 * ROPE_HALF)) < ROPE_HALF


def _rope_tables(pos_a, pos_b):
    inv = ROPE_THETA ** (-jnp.arange(ROPE_HALF, dtype=F32) / ROPE_HALF)
    def one(pos):
        ang = pos.astype(F32)[:, None] * inv[None, :]
        c, s = jnp.cos(ang), jnp.sin(ang)
        return jnp.concatenate([c, c], axis=1), jnp.concatenate([-s, s], axis=1)
    ca, sa = one(pos_a)
    cb, sb = one(pos_b)
    return jnp.concatenate([ca, cb], axis=1), jnp.concatenate([sa, sb], axis=1)


def _qkv_kernel(*refs, mode):
    if mode == "a":
        x_ref, mod_ref, g_ref, w_ref, o_ref, h_ref = refs
    elif mode == "b":
        x_ref, mod_ref, g_ref, w_ref, cos_ref, sin_ref, o_ref, h_ref = refs
    else:
        x_ref, mod_ref, g_ref, w_ref, cos_ref, sin_ref, qg_ref, kg_ref, o_ref, h_ref = refs
    j = pl.program_id(1)

    @pl.when(j == 0)
    def _():
        mod = mod_ref[0]
        h = _norm_mod(x_ref[...], g_ref[0], mod[:, D_MODEL:2 * D_MODEL], mod[:, 0:D_MODEL])
        h_ref[...] = h.astype(BF16)

    acc = jnp.dot(h_ref[...], w_ref[...], preferred_element_type=F32)
    if mode == "a":
        o_ref[...] = acc.astype(BF16)
    elif mode == "b":
        @pl.when(j % 3 < 2)
        def _():
            low = _low_half_mask()
            cos, sin = cos_ref[...], sin_ref[...]
            for c in range(acc.shape[1] // LANES):
                sl = slice(c * LANES, (c + 1) * LANES)
                o_ref[:, sl] = _rope128(acc[:, sl], cos, sin, low).astype(BF16)

        @pl.when(j % 3 == 2)
        def _():
            o_ref[...] = acc.astype(BF16)
    else:
        low = _low_half_mask()
        cos, sin = cos_ref[...], sin_ref[...]
        n_qk = C_Q_HEADS + C_KV_HEADS
        for hd in range(n_qk):
            sl = slice(hd * LANES, (hd + 1) * LANES)
            xh = acc[:, sl]
            gain = qg_ref[...] if hd < C_Q_HEADS else kg_ref[...]
            ms = jnp.mean(xh * xh, axis=-1, keepdims=True)
            xh = xh * lax.rsqrt(ms + EPS) * gain
            o_ref[:, sl] = _rope128(xh, cos, sin, low).astype(BF16)
        o_ref[:, n_qk * LANES:] = acc[:, n_qk * LANES:].astype(BF16)


def _qkv_proj(x2d, mods3, norm_g3, w, *, layer, mod_row0, seq, mode, cos=None, sin=None, qg=None, kg=None):
    t = x2d.shape[0]
    n = w.shape[1]
    tm = TM_QKV
    tn = n if mode == "c" else TN_QKV
    per_seq = seq // tm
    in_specs = [
        pl.BlockSpec((tm, D_MODEL), lambda i, j: (i, 0)),
        pl.BlockSpec((1, 1, 6 * D_MODEL), lambda i, j: (layer * MOD_ROWS + mod_row0 + i // per_seq, 0, 0)),
        pl.BlockSpec((1, 1, D_MODEL), lambda i, j: (2 * layer, 0, 0)),
        pl.BlockSpec((D_MODEL, tn), lambda i, j: (0, j)),
    ]
    args = [x2d, mods3, norm_g3, w]
    if mode in ("b", "c"):
        in_specs += [pl.BlockSpec((tm, LANES), lambda i, j: (i % per_seq, 0))] * 2
        args += [cos, sin]
    if mode == "c":
        in_specs += [pl.BlockSpec((1, C_HEAD_DIM), lambda i, j: (0, 0))] * 2
        args += [qg, kg]
    return pl.pallas_call(
        functools.partial(_qkv_kernel, mode=mode),
        grid=(t // tm, n // tn),
        in_specs=in_specs,
        out_specs=pl.BlockSpec((tm, tn), lambda i, j: (i, j)),
        out_shape=jax.ShapeDtypeStruct((t, n), BF16),
        scratch_shapes=[pltpu.VMEM((tm, D_MODEL), BF16)],
        compiler_params=_cparams(("parallel", "arbitrary")),
        name="norm_qkv_" + mode,
    )(*args)


def _pair_masks():
    lane = lax.broadcasted_iota(jnp.int32, (1, LANES), 1)
    first = lane < HEAD_DIM_AB
    return first, jnp.logical_not(first)


def _dot_nt(a, b):
    return lax.dot_general(a, b, (((1,), (1,)), ((), ())), preferred_element_type=F32)


def _attn_a_kernel(q_ref, kp_ref, kc_ref, kn_ref, vp_ref, vc_ref, vn_ref, bias_ref, o_ref):
    tq = q_ref.shape[0]
    masks = _pair_masks()
    for p in range(HEADS_AB // 2):
        sl = slice(p * LANES, (p + 1) * LANES)
        qp = q_ref[:, sl]
        ks = (kp_ref[:, sl], kc_ref[:, sl], kn_ref[:, sl])
        vs = (vp_ref[:, sl], vc_ref[:, sl], vn_ref[:, sl])
        outs = []
        for e in range(2):
            head = 2 * p + e
            qm = jnp.where(masks[e], qp, jnp.zeros_like(qp))
            s = [_dot_nt(qm, ks[n]) + bias_ref[0, head, :, n * tq:(n + 1) * tq] for n in range(3)]
            m = jnp.maximum(jnp.maximum(s[0], s[1]), s[2]).max(axis=-1, keepdims=True)
            pr = [jnp.exp(sn - m) for sn in s]
            l = (pr[0] + pr[1] + pr[2]).sum(axis=-1, keepdims=True)
            o = sum(jnp.dot(pr[n].astype(BF16), vs[n], preferred_element_type=F32) for n in range(3))
            outs.append(o / l)
        o_ref[:, sl] = jnp.where(masks[0], outs[0], outs[1]).astype(BF16)


def _attn_a_index_tables():
    rows_per = A_TQ // GRID_W
    rows = 8 * rows_per
    nb = rows // rows_per
    rl = np.arange(A_TQ) // GRID_W
    c = np.arange(A_TQ) % GRID_W
    krl = np.arange(3 * A_TQ) // GRID_W
    kc = np.arange(3 * A_TQ) % GRID_W
    valid, flat = [], []
    for i in (0, 1, nb - 1):
        r = rows_per * i + rl
        rs = np.clip(r - A_WIN_ROWS // 2, 0, rows - A_WIN_ROWS)
        cs = np.clip(c - A_WIN_COLS // 2, 0, GRID_W - A_WIN_COLS)
        kr = rows_per * (i - 1) + krl
        ok = ((kr[None, :] >= rs[:, None]) & (kr[None, :] < rs[:, None] + A_WIN_ROWS)
              & (kc[None, :] >= cs[:, None]) & (kc[None, :] < cs[:, None] + A_WIN_COLS))
        dr = np.clip(kr[None, :] - r[:, None] + A_WIN_ROWS - 1, 0, 2 * A_WIN_ROWS - 2)
        dc = np.clip(kc[None, :] - c[:, None] + A_WIN_COLS - 1, 0, 2 * A_WIN_COLS - 2)
        valid.append(ok)
        flat.append(dr * (2 * A_WIN_COLS - 1) + dc)
    return np.stack(valid), np.stack(flat).astype(np.int32)


def _attn_a_bias(rpb):
    valid, flat = _attn_a_index_tables()
    table = jnp.take(rpb.reshape(HEADS_AB, -1), jnp.asarray(flat), axis=1)
    table = jnp.where(jnp.asarray(valid)[None], table, MASK_NEG)
    return jnp.transpose(table, (1, 0, 2, 3))


def _attn_a(qkv, bias, *, batch, seq):
    tq = A_TQ
    nb = seq // tq
    assert nb >= 3
    q_spec = pl.BlockSpec((tq, D_MODEL), lambda b, i: (b * nb + i, 0))
    def kv_spec(col, off):
        return pl.BlockSpec((tq, D_MODEL), lambda b, i: (b * nb + jnp.clip(i + off, 0, nb - 1), col))
    bias_spec = pl.BlockSpec((1, HEADS_AB, tq, 3 * tq),
                             lambda b, i: (jnp.where(i == 0, 0, jnp.where(i == nb - 1, 2, 1)), 0, 0, 0))
    return pl.pallas_call(
        _attn_a_kernel,
        grid=(batch, nb),
        in_specs=[q_spec, kv_spec(1, -1), kv_spec(1, 0), kv_spec(1, 1),
                  kv_spec(2, -1), kv_spec(2, 0), kv_spec(2, 1), bias_spec],
        out_specs=pl.BlockSpec((tq, D_MODEL), lambda b, i: (b * nb + i, 0)),
        out_shape=jax.ShapeDtypeStruct((batch * seq, D_MODEL), BF16),
        compiler_params=_cparams(("parallel", "arbitrary")),
        name="attn_neighbourhood",
    )(qkv, qkv, qkv, qkv, qkv, qkv, qkv, bias)


def _attn_b_kernel(q_ref, kp_ref, kc_ref, kn_ref, vp_ref, vc_ref, vn_ref, mask_ref, o_ref, lse_ref):
    tq = q_ref.shape[0]
    masks = _pair_masks()
    mask = mask_ref[0]
    lane = lax.broadcasted_iota(jnp.int32, (1, LANES), 1)
    lse_tile = jnp.zeros((tq, LANES), F32)
    for p in range(HEADS_AB // 2):
        sl = slice(p * LANES, (p + 1) * LANES)
        qp = q_ref[:, sl]
        kcat = jnp.concatenate([kp_ref[tq - B_SIDE:, sl], kc_ref[:, sl], kn_ref[:B_SIDE, sl]], axis=0)
        vcat = jnp.concatenate([vp_ref[tq - B_SIDE:, sl], vc_ref[:, sl], vn_ref[:B_SIDE, sl]], axis=0)
        outs = []
        for e in range(2):
            qm = jnp.where(masks[e], qp, jnp.zeros_like(qp))
            s = _dot_nt(qm, kcat) + mask
            m = s.max(axis=-1, keepdims=True)
            pr = jnp.exp(s - m)
            l = pr.sum(axis=-1, keepdims=True)
            o = jnp.dot(pr.astype(BF16), vcat, preferred_element_type=F32)
            outs.append(o / l)
            lse_tile = jnp.where(lane == 2 * p + e, m + jnp.log(l), lse_tile)
        o_ref[:, sl] = jnp.where(masks[0], outs[0], outs[1]).astype(BF16)
    lse_ref[...] = lse_tile


def _attn_b_masks():
    qq = np.arange(B_TQ)[:, None]
    jj = np.arange(B_TQ + 2 * B_SIDE)[None, :]
    band = (jj - qq >= 0) & (jj - qq <= 2 * B_SIDE)
    out = []
    for ty in range(4):
        ok = band
        if ty & 1:
            ok = ok & (jj >= B_SIDE)
        if ty & 2:
            ok = ok & (jj < B_SIDE + B_TQ)
        out.append(np.where(ok, 0.0, MASK_NEG))
    return np.stack(out).astype(np.float32)


def _attn_b_group(qkv, masks, *, batch, seq, group, dil):
    tq = B_TQ
    sub = seq // dil
    nb = sub // tq
    assert sub % tq == 0
    width = qkv.shape[1]
    col_blocks = width // D_MODEL
    view = qkv.reshape(batch * sub, dil * width)
    def col(res, which):
        return res * col_blocks + 3 * group + which
    q_spec = pl.BlockSpec((tq, D_MODEL), lambda b, r, i: (b * nb + i, col(r, 0)))
    def kv_spec(which, off):
        return pl.BlockSpec((tq, D_MODEL),
                            lambda b, r, i: (b * nb + jnp.clip(i + off, 0, nb - 1), col(r, which)))
    mask_spec = pl.BlockSpec((1, tq, tq + 2 * B_SIDE),
                             lambda b, r, i: ((i == 0).astype(jnp.int32) + 2 * (i == nb - 1).astype(jnp.int32), 0, 0))
    o, lse = pl.pallas_call(
        _attn_b_kernel,
        grid=(batch, dil, nb),
        in_specs=[q_spec, kv_spec(1, -1), kv_spec(1, 0), kv_spec(1, 1),
                  kv_spec(2, -1), kv_spec(2, 0), kv_spec(2, 1), mask_spec],
        out_specs=[pl.BlockSpec((tq, D_MODEL), lambda b, r, i: (b * nb + i, r)),
                   pl.BlockSpec((tq, LANES), lambda b, r, i: (b * nb + i, r))],
        out_shape=[jax.ShapeDtypeStruct((batch * sub, dil * D_MODEL), BF16),
                   jax.ShapeDtypeStruct((batch * sub, dil * LANES), F32)],
        compiler_params=_cparams(("parallel", "parallel", "arbitrary")),
        name="attn_dilated_g%d" % group,
    )(view, view, view, view, view, view, view, masks)
    return o.reshape(batch * seq, D_MODEL), lse.reshape(batch * seq, LANES)


def _attn_c_kernel(q_ref, k_ref, v_ref, o_ref, m_ref, l_ref, acc_ref):
    kj = pl.program_id(3)

    @pl.when(kj == 0)
    def _():
        m_ref[...] = jnp.full(m_ref.shape, MASK_NEG, F32)
        l_ref[...] = jnp.zeros(l_ref.shape, F32)
        acc_ref[...] = jnp.zeros(acc_ref.shape, F32)

    k = k_ref[...]
    v = v_ref[...]
    for h in range(C_GROUP):
        q = q_ref[:, h * C_HEAD_DIM:(h + 1) * C_HEAD_DIM]
        s = _dot_nt(q, k)
        m_prev = m_ref[h]
        m_new = jnp.maximum(m_prev, s.max(axis=-1, keepdims=True))
        alpha = jnp.exp(m_prev - m_new)
        pr = jnp.exp(s - m_new)
        l_ref[h] = alpha * l_ref[h] + pr.sum(axis=-1, keepdims=True)
        acc_ref[h] = alpha * acc_ref[h] + jnp.dot(pr.astype(BF16), v, preferred_element_type=F32)
        m_ref[h] = m_new

    @pl.when(kj == pl.num_programs(3) - 1)
    def _():
        for h in range(C_GROUP):
            o_ref[:, h * C_HEAD_DIM:(h + 1) * C_HEAD_DIM] = (acc_ref[h] / l_ref[h]).astype(BF16)


def _attn_c(qkv, *, batch, seq):
    tq, tk = C_TQ, C_TK
    nq, nk = seq // tq, seq // tk
    qw = C_GROUP * C_HEAD_DIM
    k_col0 = C_Q_HEADS
    v_col0 = C_Q_HEADS + C_KV_HEADS
    return pl.pallas_call(
        _attn_c_kernel,
        grid=(batch, C_KV_HEADS, nq, nk),
        in_specs=[
            pl.BlockSpec((tq, qw), lambda b, g, i, j: (b * nq + i, g)),
            pl.BlockSpec((tk, C_HEAD_DIM), lambda b, g, i, j: (b * nk + j, k_col0 + g)),
            pl.BlockSpec((tk, C_HEAD_DIM), lambda b, g, i, j: (b * nk + j, v_col0 + g)),
        ],
        out_specs=pl.BlockSpec((tq, qw), lambda b, g, i, j: (b * nq + i, g)),
        out_shape=jax.ShapeDtypeStruct((batch * seq, C_Q_HEADS * C_HEAD_DIM), BF16),
        scratch_shapes=[pltpu.VMEM((C_GROUP, tq, 1), F32), pltpu.VMEM((C_GROUP, tq, 1), F32),
                        pltpu.VMEM((C_GROUP, tq, C_HEAD_DIM), F32)],
        compiler_params=_cparams(("parallel", "parallel", "parallel", "arbitrary")),
        name="attn_gqa_flash",
    )(qkv, qkv, qkv)


def _post_kernel(*refs, merge, final):
    refs = list(refs)
    x_ref = refs.pop(0)
    if merge:
        o_refs = [refs.pop(0) for _ in range(3)]
        lse_refs = [refs.pop(0) for _ in range(3)]
        expand_ref = refs.pop(0)
    else:
        o_ref_in = refs.pop(0)
    mod_ref, g_ref, wo_ref, w1_ref, w2_ref = [refs.pop(0) for _ in range(5)]
    fg_ref = refs.pop(0) if final else None
    out_ref, x1_ref, h2_ref, acc_ref = refs
    kf = pl.program_id(1)

    @pl.when(kf == 0)
    def _():
        mod = mod_ref[0]
        if merge:
            lse = [r[...] for r in lse_refs]
            top = jnp.maximum(jnp.maximum(lse[0], lse[1]), lse[2])
            ex = [jnp.exp(v - top) for v in lse]
            den = ex[0] + ex[1] + ex[2]
            mixed = jnp.zeros(x1_ref.shape, F32)
            for g in range(3):
                wgt = ex[g] / den
                hi = wgt.astype(BF16)
                lo = (wgt - hi.astype(F32)).astype(BF16)
                wide = (jnp.dot(hi, expand_ref[...], preferred_element_type=F32)
                        + jnp.dot(lo, expand_ref[...], preferred_element_type=F32))
                mixed = mixed + wide * o_refs[g][...].astype(F32)
            o = mixed.astype(BF16)
        else:
            o = o_ref_in[...]
        mix = jnp.dot(o, wo_ref[...], preferred_element_type=F32)
        x1 = x_ref[...] + mod[:, 2 * D_MODEL:3 * D_MODEL] * mix
        x1_ref[...] = x1
        h2 = _norm_mod(x1, g_ref[0], mod[:, 4 * D_MODEL:5 * D_MODEL], mod[:, 3 * D_MODEL:4 * D_MODEL])
        h2_ref[...] = h2.astype(BF16)
        acc_ref[...] = jnp.zeros(acc_ref.shape, F32)

    a = jnp.dot(h2_ref[...], w1_ref[...], preferred_element_type=F32)
    a = jnp.square(jnp.maximum(a, 0.0)).astype(BF16)
    acc_ref[...] += jnp.dot(a, w2_ref[...], preferred_element_type=F32)

    @pl.when(kf == pl.num_programs(1) - 1)
    def _():
        mod = mod_ref[0]
        x2 = x1_ref[...] + mod[:, 5 * D_MODEL:6 * D_MODEL] * acc_ref[...]
        if final:
            ms = jnp.mean(x2 * x2, axis=-1, keepdims=True)
            x2 = x2 * lax.rsqrt(ms + EPS) * fg_ref[...]
        out_ref[...] = x2


def _head_expand_matrix():
    e = np.zeros((LANES, D_MODEL), np.float32)
    for h in range(HEADS_AB):
        e[h, h * HEAD_DIM_AB:(h + 1) * HEAD_DIM_AB] = 1.0
    return e


def _post(x2d, attn, mods3, norm_g3, wo, w1, w2, final_g, *, layer, mod_row0, seq, merge, final):
    t = x2d.shape[0]
    tm, tf = TM_POST, TF_POST
    per_seq = seq // tm
    row = lambda i, k: (i, 0)
    in_specs = [pl.BlockSpec((tm, D_MODEL), row)]
    args = [x2d]
    if merge:
        outs, lses = attn
        in_specs += [pl.BlockSpec((tm, D_MODEL), row)] * 3 + [pl.BlockSpec((tm, LANES), row)] * 3
        in_specs += [pl.BlockSpec((LANES, D_MODEL), lambda i, k: (0, 0))]
        args += list(outs) + list(lses) + [jnp.asarray(_head_expand_matrix(), BF16)]
    else:
        in_specs += [pl.BlockSpec((tm, D_MODEL), row)]
        args += [attn]
    in_specs += [
        pl.BlockSpec((1, 1, 6 * D_MODEL), lambda i, k: (layer * MOD_ROWS + mod_row0 + i // per_seq, 0, 0)),
        pl.BlockSpec((1, 1, D_MODEL), lambda i, k: (2 * layer + 1, 0, 0)),
        pl.BlockSpec((D_MODEL, D_MODEL), lambda i, k: (0, 0)),
        pl.BlockSpec((D_MODEL, tf), lambda i, k: (0, k)),
        pl.BlockSpec((tf, D_MODEL), lambda i, k: (k, 0)),
    ]
    args += [mods3, norm_g3, wo, w1, w2]
    if final:
        in_specs += [pl.BlockSpec((1, D_MODEL), lambda i, k: (0, 0))]
        args += [final_g.reshape(1, D_MODEL)]
    return pl.pallas_call(
        functools.partial(_post_kernel, merge=merge, final=final),
        grid=(t // tm, D_FF // tf),
        in_specs=in_specs,
        out_specs=pl.BlockSpec((tm, D_MODEL), row),
        out_shape=jax.ShapeDtypeStruct((t, D_MODEL), F32),
        scratch_shapes=[pltpu.VMEM((tm, D_MODEL), F32), pltpu.VMEM((tm, D_MODEL), BF16),
                        pltpu.VMEM((tm, D_MODEL), F32)],
        compiler_params=_cparams(("parallel", "arbitrary")),
        name="wo_mlp_merge" if merge else "wo_mlp",
    )(*args)


def _trunk(x, mods3, mod_row0, norm_g3, final_g, wts, a_bias, b_masks):
    batch, seq, _ = x.shape
    x2d = x.reshape(batch * seq, D_MODEL)
    t = jnp.arange(seq)
    tabs_b = _rope_tables(t, t)
    tabs_c = _rope_tables(t // GRID_W, t % GRID_W)
    for layer in range(DEPTH):
        kind, j = layer % N_MIXERS, layer // N_MIXERS
        common = dict(layer=layer, mod_row0=mod_row0, seq=seq)
        if kind == 0:
            qkv = _qkv_proj(x2d, mods3, norm_g3, wts["a_qkv"][j], mode="a", **common)
            attn = _attn_a(qkv, a_bias[j], batch=batch, seq=seq)
            wo = wts["a_o"][j]
        elif kind == 1:
            qkv = _qkv_proj(x2d, mods3, norm_g3, wts["b_qkv"][j], mode="b", cos=tabs_b[0], sin=tabs_b[1], **common)
            groups = [_attn_b_group(qkv, b_masks, batch=batch, seq=seq, group=g, dil=dil)
                      for g, (_, dil) in enumerate(B_PAIRS)]
            attn = ([o for o, _ in groups], [l for _, l in groups])
            wo = wts["b_o"][j]
        else:
            qkv = _qkv_proj(x2d, mods3, norm_g3, wts["c_qkv"][j], mode="c", cos=tabs_c[0], sin=tabs_c[1],
                            qg=wts["c_qg"][j], kg=wts["c_kg"][j], **common)
            attn = _attn_c(qkv, batch=batch, seq=seq)
            wo = wts["c_o"][j]
        x2d = _post(x2d, attn, mods3, norm_g3, wo, wts["w1"][layer], wts["w2"][layer], final_g,
                    merge=(kind == 1), final=(layer == DEPTH - 1), **common)
    return x2d.reshape(batch, seq, D_MODEL)


def kernel(x_prompt, x_sample, c_prompt, c_sample, w_mod, b_mod, norm_g, final_g, a_w_qkv, a_rpb, a_w_o,
           b_w_qkv, b_w_o, c_w_qkv, c_q_g, c_k_g, c_w_o, mlp_w1, mlp_w2):
    nb_p, nb_s = c_prompt.shape[0], c_sample.shape[0]
    assert nb_p + nb_s <= MOD_ROWS
    c_all = jnp.concatenate([c_prompt, c_sample, jnp.zeros((MOD_ROWS - nb_p - nb_s, D_MODEL), F32)], axis=0)
    mods3 = _modulation(c_all, w_mod, b_mod).reshape(DEPTH * MOD_ROWS, 1, 6 * D_MODEL)
    norm_g3 = norm_g.reshape(DEPTH * 2, 1, D_MODEL)

    scale_ab = HEAD_DIM_AB ** -0.5
    a_qkv = a_w_qkv.at[:, :, :D_MODEL].multiply(scale_ab).astype(BF16)
    b_qkv = b_w_qkv.reshape(-1, D_MODEL, len(B_PAIRS), 3, D_MODEL).at[:, :, :, 0].multiply(scale_ab)
    b_qkv = b_qkv.reshape(b_w_qkv.shape).astype(BF16)
    wts = {
        "a_qkv": a_qkv, "a_o": a_w_o.astype(BF16),
        "b_qkv": b_qkv, "b_o": b_w_o.astype(BF16),
        "c_qkv": c_w_qkv.astype(BF16), "c_o": c_w_o.astype(BF16),
        "c_qg": (c_q_g * (C_HEAD_DIM ** -0.5)).reshape(-1, 1, C_HEAD_DIM),
        "c_kg": c_k_g.reshape(-1, 1, C_HEAD_DIM),
        "w1": mlp_w1.astype(BF16), "w2": mlp_w2.astype(BF16),
    }
    a_bias = [_attn_a_bias(a_rpb[j]) for j in range(a_rpb.shape[0])]
    b_masks = jnp.asarray(_attn_b_masks())

    y_prompt = _trunk(x_prompt, mods3, 0, norm_g3, final_g, wts, a_bias, b_masks)
    y_sample = _trunk(x_sample, mods3, nb_p, norm_g3, final_g, wts, a_bias, b_masks)
    return (y_prompt, y_sample)
```

```python
import functools

import numpy as np
import jax
import jax.numpy as jnp
from jax import lax
from jax.experimental import pallas as pl
from jax.experimental.pallas import tpu as pltpu

F32 = jnp.float32
BF16 = jnp.bfloat16

D_MODEL = 1024
DEPTH = 4
GRID_W = 64
D_FF = 4 * D_MODEL
EPS = 1e-6
ROPE_THETA = 10000.0
N_MIXERS = 3
HEAD_DIM_AB = 64
HEADS_AB = D_MODEL // HEAD_DIM_AB
A_WIN_ROWS = 8
A_WIN_COLS = 16
B_PAIRS = ((128, 1), (512, 4), (2048, 16))
B_SIDE = 64
C_HEAD_DIM = 128
C_Q_HEADS = 8
C_KV_HEADS = 2
C_GROUP = C_Q_HEADS // C_KV_HEADS
C_QKV_WIDTH = (C_Q_HEADS + 2 * C_KV_HEADS) * C_HEAD_DIM
ROPE_HALF = 32
LANES = 128
MOD_ROWS = 8
MASK_NEG = -1e30
VMEM_LIMIT = 56 * 1024 * 1024

TM_QKV = 1024
TN_QKV = 1024
TM_POST = 512
TF_POST = 1024
A_TQ = 4 * GRID_W
B_TQ = 256
C_TQ = 512
C_TK = 1024


def _cparams(sem):
    return pltpu.CompilerParams(dimension_semantics=sem, vmem_limit_bytes=VMEM_LIMIT)


def _mod_kernel(c_ref, w_ref, b_ref, o_ref):
    c = c_ref[...]
    act = (c / (1.0 + jnp.exp(-c))).astype(BF16)
    o_ref[0] = jnp.dot(act, w_ref[0].astype(BF16), preferred_element_type=F32) + b_ref[0]


def _modulation(c_all, w_mod, b_mod):
    tn = 1536
    n = 6 * D_MODEL
    return pl.pallas_call(
        _mod_kernel,
        grid=(DEPTH, n // tn),
        in_specs=[
            pl.BlockSpec((MOD_ROWS, D_MODEL), lambda l, j: (0, 0)),
            pl.BlockSpec((1, D_MODEL, tn), lambda l, j: (l, 0, j)),
            pl.BlockSpec((1, 1, tn), lambda l, j: (l, 0, j)),
        ],
        out_specs=pl.BlockSpec((1, MOD_ROWS, tn), lambda l, j: (l, 0, j)),
        out_shape=jax.ShapeDtypeStruct((DEPTH, MOD_ROWS, n), F32),
        compiler_params=_cparams(("parallel", "parallel")),
        name="adaln_modulation",
    )(c_all, w_mod, b_mod.reshape(DEPTH, 1, n))


def _norm_mod(x, gain, scale, shift):
    ms = jnp.mean(x * x, axis=-1, keepdims=True)
    return (x * lax.rsqrt(ms + EPS) * gain) * (1.0 + scale) + shift


def _rope128(x, cos, sin_signed, low_half):
    up = pltpu.roll(x, LANES - ROPE_HALF, 1)
    down = pltpu.roll(x, ROPE_HALF, 1)
    return x * cos + jnp.where(low_half, up, down) * sin_signed


def _low_half_mask():
    lane = lax.broadcasted_iota(jnp.int32, (1, LANES), 1)
    return (lane % (2 * ROPE_HALF)) < ROPE_HALF


def _rope_tables(pos_a, pos_b):
    inv = ROPE_THETA ** (-jnp.arange(ROPE_HALF, dtype=F32) / ROPE_HALF)
    def one(pos):
        ang = pos.astype(F32)[:, None] * inv[None, :]
        c, s = jnp.cos(ang), jnp.sin(ang)
        return jnp.concatenate([c, c], axis=1), jnp.concatenate([-s, s], axis=1)
    ca, sa = one(pos_a)
    cb, sb = one(pos_b)
    return jnp.concatenate([ca, cb], axis=1), jnp.concatenate([sa, sb], axis=1)


def _qkv_kernel(*refs, mode, dil=1):
    if mode == "a":
        x_ref, mod_ref, g_ref, w_ref, o_ref, h_ref = refs
    elif mode == "b":
        x_ref, mod_ref, g_ref, w_ref, cos_ref, sin_ref, o_ref, h_ref, stage_ref = refs
    else:
        x_ref, mod_ref, g_ref, w_ref, cos_ref, sin_ref, qg_ref, kg_ref, o_ref, h_ref = refs
    j = pl.program_id(1)

    @pl.when(j == 0)
    def _():
        mod = mod_ref[0]
        h = _norm_mod(x_ref[...], g_ref[0], mod[:, D_MODEL:2 * D_MODEL], mod[:, 0:D_MODEL])
        h_ref[...] = h.astype(BF16)

    acc = jnp.dot(h_ref[...], w_ref[...], preferred_element_type=F32)
    if mode == "a":
        o_ref[...] = acc.astype(BF16)
    elif mode == "b":
        rows = acc.shape[0] // dil
        chunks = acc.shape[1] // LANES

        def emit(c, val):
            sl = slice(c * LANES, (c + 1) * LANES)
            if dil == 1:
                o_ref[0, 0, :, sl] = val.astype(BF16)
                return
            stage_ref[c] = val
            for r in range(dil):
                o_ref[0, r, :, sl] = stage_ref[c, pl.ds(r, rows, stride=dil), :].astype(BF16)

        @pl.when(j < 2)
        def _():
            low = _low_half_mask()
            cos, sin = cos_ref[...], sin_ref[...]
            for c in range(chunks):
                emit(c, _rope128(acc[:, c * LANES:(c + 1) * LANES], cos, sin, low))

        @pl.when(j == 2)
        def _():
            for c in range(chunks):
                emit(c, acc[:, c * LANES:(c + 1) * LANES])
    else:
        low = _low_half_mask()
        cos, sin = cos_ref[...], sin_ref[...]
        n_qk = C_Q_HEADS + C_KV_HEADS
        for hd in range(n_qk):
            sl = slice(hd * LANES, (hd + 1) * LANES)
            xh = acc[:, sl]
            gain = qg_ref[...] if hd < C_Q_HEADS else kg_ref[...]
            ms = jnp.mean(xh * xh, axis=-1, keepdims=True)
            xh = xh * lax.rsqrt(ms + EPS) * gain
            o_ref[:, sl] = _rope128(xh, cos, sin, low).astype(BF16)
        o_ref[:, n_qk * LANES:] = acc[:, n_qk * LANES:].astype(BF16)


def _qkv_proj(x2d, mods3, norm_g3, w, *, layer, mod_row0, seq, mode, cos=None, sin=None, qg=None, kg=None,
              group=0, dil=1):
    t = x2d.shape[0]
    tm = TM_QKV
    if mode == "b":
        n, col0 = 3 * D_MODEL, 3 * group
    else:
        n, col0 = w.shape[1], 0
    tn = n if mode == "c" else TN_QKV
    per_seq = seq // tm
    in_specs = [
        pl.BlockSpec((tm, D_MODEL), lambda i, j: (i, 0)),
        pl.BlockSpec((1, 1, 6 * D_MODEL), lambda i, j: (layer * MOD_ROWS + mod_row0 + i // per_seq, 0, 0)),
        pl.BlockSpec((1, 1, D_MODEL), lambda i, j: (2 * layer, 0, 0)),
        pl.BlockSpec((D_MODEL, tn), lambda i, j: (0, col0 + j)),
    ]
    args = [x2d, mods3, norm_g3, w]
    if mode in ("b", "c"):
        in_specs += [pl.BlockSpec((tm, LANES), lambda i, j: (i % per_seq, 0))] * 2
        args += [cos, sin]
    if mode == "c":
        in_specs += [pl.BlockSpec((1, C_HEAD_DIM), lambda i, j: (0, 0))] * 2
        args += [qg, kg]
    scratch = [pltpu.VMEM((tm, D_MODEL), BF16)]
    if mode == "b":
        assert tm % dil == 0 and tn == D_MODEL
        batch = t // seq
        out_spec = pl.BlockSpec((1, dil, tm // dil, tn), lambda i, j: (i // per_seq, 0, i % per_seq, j))
        out_shape = jax.ShapeDtypeStruct((batch, dil, seq // dil, n), BF16)
        scratch += [pltpu.VMEM((tn // LANES, tm, LANES), F32)]
    else:
        out_spec = pl.BlockSpec((tm, tn), lambda i, j: (i, j))
        out_shape = jax.ShapeDtypeStruct((t, n), BF16)
    return pl.pallas_call(
        functools.partial(_qkv_kernel, mode=mode, dil=dil),
        grid=(t // tm, n // tn),
        in_specs=in_specs,
        out_specs=out_spec,
        out_shape=out_shape,
        scratch_shapes=scratch,
        compiler_params=_cparams(("parallel", "arbitrary")),
        name="norm_qkv_" + mode,
    )(*args)


def _pair_masks():
    lane = lax.broadcasted_iota(jnp.int32, (1, LANES), 1)
    first = lane < HEAD_DIM_AB
    return first, jnp.logical_not(first)


def _dot_nt(a, b):
    return lax.dot_general(a, b, (((1,), (1,)), ((), ())), preferred_element_type=F32)


def _pipelined_heads(scores, finish):
    nxt = scores(0)
    for head in range(HEADS_AB):
        cur = nxt
        if head + 1 < HEADS_AB:
            nxt = scores(head + 1)
        finish(head, cur)


def _attn_a_kernel(q_ref, kp_ref, kc_ref, kn_ref, vp_ref, vc_ref, vn_ref, bias_ref, o_ref):
    masks = _pair_masks()
    k_refs = (kp_ref, kc_ref, kn_ref)
    v_refs = (vp_ref, vc_ref, vn_ref)
    outs = {}

    def scores(head):
        sl = slice((head // 2) * LANES, (head // 2 + 1) * LANES)
        qp = q_ref[:, sl]
        qm = jnp.where(masks[head % 2], qp, jnp.zeros_like(qp))
        return _dot_nt(qm, jnp.concatenate([r[:, sl] for r in k_refs], axis=0)) + bias_ref[0, head]

    def finish(head, s):
        sl = slice((head // 2) * LANES, (head // 2 + 1) * LANES)
        m = s.max(axis=-1, keepdims=True)
        pr = jnp.exp(s - m)
        l = pr.sum(axis=-1, keepdims=True)
        vcat = jnp.concatenate([r[:, sl] for r in v_refs], axis=0)
        outs[head] = jnp.dot(pr.astype(BF16), vcat, preferred_element_type=F32) / l
        if head % 2 == 1:
            o_ref[:, sl] = jnp.where(masks[0], outs.pop(head - 1), outs.pop(head)).astype(BF16)

    _pipelined_heads(scores, finish)


def _attn_a_bias(rpb):
    rows_per = A_TQ // GRID_W
    n_dr, n_dc = 2 * A_WIN_ROWS - 1, 2 * A_WIN_COLS - 1
    ql, kl = np.arange(rows_per)[:, None], np.arange(3 * rows_per)[None, :]
    c, kc = np.arange(GRID_W)[:, None], np.arange(GRID_W)[None, :]
    dr = kl - rows_per - ql + A_WIN_ROWS - 1
    dc = kc - c + A_WIN_COLS - 1
    onehot_r = (dr[..., None] == np.arange(n_dr)).astype(np.float32)
    onehot_c = (dc[..., None] == np.arange(n_dc)).astype(np.float32)
    table = jnp.einsum("hde,qkd,cje->hqckj", rpb, onehot_r, onehot_c, precision=lax.Precision.HIGHEST)
    cs = np.clip(c - A_WIN_COLS // 2, 0, GRID_W - A_WIN_COLS)
    col_ok = (kc >= cs) & (kc < cs + A_WIN_COLS)
    rows = 8 * rows_per
    out = []
    for i in (0, 1, rows // rows_per - 1):
        r = rows_per * i + ql
        rs = np.clip(r - A_WIN_ROWS // 2, 0, rows - A_WIN_ROWS)
        kr = rows_per * (i - 1) + kl
        row_ok = (kr >= rs) & (kr < rs + A_WIN_ROWS)
        ok = row_ok[:, None, :, None] & col_ok[None, :, None, :]
        out.append(jnp.where(jnp.asarray(ok)[None], table, MASK_NEG).reshape(HEADS_AB, A_TQ, 3 * A_TQ))
    return jnp.stack(out)


def _attn_a(qkv, bias, *, batch, seq):
    tq = A_TQ
    nb = seq // tq
    assert nb >= 3
    q_spec = pl.BlockSpec((tq, D_MODEL), lambda b, i: (b * nb + i, 0))
    def kv_spec(col, off):
        return pl.BlockSpec((tq, D_MODEL), lambda b, i: (b * nb + jnp.clip(i + off, 0, nb - 1), col))
    bias_spec = pl.BlockSpec((1, HEADS_AB, tq, 3 * tq),
                             lambda b, i: (jnp.where(i == 0, 0, jnp.where(i == nb - 1, 2, 1)), 0, 0, 0))
    return pl.pallas_call(
        _attn_a_kernel,
        grid=(batch, nb),
        in_specs=[q_spec, kv_spec(1, -1), kv_spec(1, 0), kv_spec(1, 1),
                  kv_spec(2, -1), kv_spec(2, 0), kv_spec(2, 1), bias_spec],
        out_specs=pl.BlockSpec((tq, D_MODEL), lambda b, i: (b * nb + i, 0)),
        out_shape=jax.ShapeDtypeStruct((batch * seq, D_MODEL), BF16),
        compiler_params=_cparams(("parallel", "arbitrary")),
        name="attn_neighbourhood",
    )(qkv, qkv, qkv, qkv, qkv, qkv, qkv, bias)


def _attn_b_kernel(q_ref, kp_ref, kc_ref, kn_ref, vp_ref, vc_ref, vn_ref, mask_ref, o_ref, lse_ref):
    tq = q_ref.shape[0]
    masks = _pair_masks()
    lane = lax.broadcasted_iota(jnp.int32, (1, LANES), 1)
    outs = {}
    lse_tile = [jnp.zeros((tq, LANES), F32)]

    def window(prev_ref, cur_ref, next_ref, sl):
        return jnp.concatenate([prev_ref[tq - B_SIDE:, sl], cur_ref[:, sl], next_ref[:B_SIDE, sl]], axis=0)

    def scores(head):
        sl = slice((head // 2) * LANES, (head // 2 + 1) * LANES)
        qp = q_ref[:, sl]
        qm = jnp.where(masks[head % 2], qp, jnp.zeros_like(qp))
        return _dot_nt(qm, window(kp_ref, kc_ref, kn_ref, sl)) + mask_ref[0]

    def finish(head, s):
        sl = slice((head // 2) * LANES, (head // 2 + 1) * LANES)
        m = s.max(axis=-1, keepdims=True)
        pr = jnp.exp(s - m)
        l = pr.sum(axis=-1, keepdims=True)
        outs[head] = jnp.dot(pr.astype(BF16), window(vp_ref, vc_ref, vn_ref, sl), preferred_element_type=F32) / l
        lse_tile[0] = jnp.where(lane == head, m + jnp.log(l), lse_tile[0])
        if head % 2 == 1:
            o_ref[:, sl] = jnp.where(masks[0], outs.pop(head - 1), outs.pop(head)).astype(BF16)

    _pipelined_heads(scores, finish)
    lse_ref[...] = lse_tile[0]


def _attn_b_masks():
    qq = np.arange(B_TQ)[:, None]
    jj = np.arange(B_TQ + 2 * B_SIDE)[None, :]
    band = (jj - qq >= 0) & (jj - qq <= 2 * B_SIDE)
    out = []
    for ty in range(4):
        ok = band
        if ty & 1:
            ok = ok & (jj >= B_SIDE)
        if ty & 2:
            ok = ok & (jj < B_SIDE + B_TQ)
        out.append(np.where(ok, 0.0, MASK_NEG))
    return np.stack(out).astype(np.float32)


def _attn_b_group(qkv, masks, *, group):
    batch, dil, sub, _ = qkv.shape
    tq = B_TQ
    nb = sub // tq
    assert sub % tq == 0
    q_spec = pl.BlockSpec((None, None, tq, D_MODEL), lambda b, r, i: (b, r, i, 0))
    def kv_spec(which, off):
        return pl.BlockSpec((None, None, tq, D_MODEL),
                            lambda b, r, i: (b, r, jnp.clip(i + off, 0, nb - 1), which))
    mask_spec = pl.BlockSpec((1, tq, tq + 2 * B_SIDE),
                             lambda b, r, i: ((i == 0).astype(jnp.int32) + 2 * (i == nb - 1).astype(jnp.int32), 0, 0))
    return pl.pallas_call(
        _attn_b_kernel,
        grid=(batch, dil, nb),
        in_specs=[q_spec, kv_spec(1, -1), kv_spec(1, 0), kv_spec(1, 1),
                  kv_spec(2, -1), kv_spec(2, 0), kv_spec(2, 1), mask_spec],
        out_specs=[pl.BlockSpec((None, None, tq, D_MODEL), lambda b, r, i: (b, r, i, 0)),
                   pl.BlockSpec((None, None, tq, LANES), lambda b, r, i: (b, r, i, 0))],
        out_shape=[jax.ShapeDtypeStruct((batch, dil, sub, D_MODEL), BF16),
                   jax.ShapeDtypeStruct((batch, dil, sub, LANES), F32)],
        compiler_params=_cparams(("parallel", "parallel", "arbitrary")),
        name="attn_dilated_g%d" % group,
    )(qkv, qkv, qkv, qkv, qkv, qkv, qkv, masks)


def _attn_c_kernel(q_ref, k_ref, v_ref, o_ref, s_ref, m_ref, l_ref, acc_ref, *, tk):
    nk = k_ref.shape[0] // tk
    m_ref[...] = jnp.full(m_ref.shape, MASK_NEG, F32)
    l_ref[...] = jnp.zeros(l_ref.shape, F32)
    acc_ref[...] = jnp.zeros(acc_ref.shape, F32)

    def scores(h, c):
        start = pl.multiple_of(c * tk, tk)
        return _dot_nt(q_ref[:, h * C_HEAD_DIM:(h + 1) * C_HEAD_DIM], k_ref[pl.ds(start, tk), :])

    s_ref[0] = scores(0, 0)

    def chunk(c, carry):
        v = v_ref[pl.ds(pl.multiple_of(c * tk, tk), tk), :]
        for h in range(C_GROUP):
            if h + 1 < C_GROUP:
                s_ref[(h + 1) % 2] = scores(h + 1, c)
            else:
                s_ref[(h + 1) % 2] = scores(0, jnp.minimum(c + 1, nk - 1))
            s = s_ref[h % 2]
            m_prev = m_ref[h]
            m_new = jnp.maximum(m_prev, s.max(axis=-1, keepdims=True))
            alpha = jnp.exp2(m_prev - m_new)
            pr = jnp.exp2(s - m_new)
            l_ref[h] = alpha * l_ref[h] + pr.sum(axis=-1, keepdims=True)
            acc_ref[h] = alpha * acc_ref[h] + jnp.dot(pr.astype(BF16), v, preferred_element_type=F32)
            m_ref[h] = m_new
        return carry

    lax.fori_loop(0, nk, chunk, 0)
    for h in range(C_GROUP):
        o_ref[:, h * C_HEAD_DIM:(h + 1) * C_HEAD_DIM] = (acc_ref[h] / l_ref[h]).astype(BF16)


def _attn_c(qkv, *, batch, seq):
    tq, tk = C_TQ, C_TK
    nq = seq // tq
    assert seq % tk == 0 and C_GROUP % 2 == 0
    qw = C_GROUP * C_HEAD_DIM
    k_col0 = C_Q_HEADS
    v_col0 = C_Q_HEADS + C_KV_HEADS
    return pl.pallas_call(
        functools.partial(_attn_c_kernel, tk=tk),
        grid=(batch, C_KV_HEADS, nq),
        in_specs=[
            pl.BlockSpec((tq, qw), lambda b, g, i: (b * nq + i, g)),
            pl.BlockSpec((seq, C_HEAD_DIM), lambda b, g, i: (b, k_col0 + g)),
            pl.BlockSpec((seq, C_HEAD_DIM), lambda b, g, i: (b, v_col0 + g)),
        ],
        out_specs=pl.BlockSpec((tq, qw), lambda b, g, i: (b * nq + i, g)),
        out_shape=jax.ShapeDtypeStruct((batch * seq, C_Q_HEADS * C_HEAD_DIM), BF16),
        scratch_shapes=[pltpu.VMEM((2, tq, tk), F32),
                        pltpu.VMEM((C_GROUP, tq, 1), F32), pltpu.VMEM((C_GROUP, tq, 1), F32),
                        pltpu.VMEM((C_GROUP, tq, C_HEAD_DIM), F32)],
        compiler_params=_cparams(("parallel", "parallel", "arbitrary")),
        name="attn_gqa_flash",
    )(qkv, qkv, qkv)


def _post_kernel(*refs, merge, final):
    refs = list(refs)
    x_ref = refs.pop(0)
    if merge:
        o_refs = [refs.pop(0) for _ in range(3)]
        lse_refs = [refs.pop(0) for _ in range(3)]
        expand_ref = refs.pop(0)
    else:
        o_ref_in = refs.pop(0)
    mod_ref, g_ref, wo_ref, w1_ref, w2_ref = [refs.pop(0) for _ in range(5)]
    fg_ref = refs.pop(0) if final else None
    out_ref, x1_ref, h2_ref, acc_ref = refs[:4]
    kf = pl.program_id(1)

    @pl.when(kf == 0)
    def _():
        mod = mod_ref[0]
        if merge:
            o_tok, lse = [], []
            for g, (o_g, lse_g) in enumerate(zip(o_refs, lse_refs)):
                dil = o_g.shape[0]
                if dil == 1:
                    o_tok.append(o_g[0].astype(F32))
                    lse.append(lse_g[0])
                    continue
                o_stage, lse_stage = refs[4 + 2 * g], refs[5 + 2 * g]
                rows = o_g.shape[1]
                chunks = o_g.shape[2] // LANES
                for r in range(dil):
                    lse_stage[pl.ds(r, rows, stride=dil), :] = lse_g[r]
                    for c in range(chunks):
                        o_stage[c, pl.ds(r, rows, stride=dil), :] = o_g[r, :, c * LANES:(c + 1) * LANES].astype(F32)
                o_tok.append(jnp.concatenate([o_stage[c] for c in range(chunks)], axis=1))
                lse.append(lse_stage[...])
            top = jnp.maximum(jnp.maximum(lse[0], lse[1]), lse[2])
            ex = [jnp.exp(v - top) for v in lse]
            den = ex[0] + ex[1] + ex[2]
            mixed = jnp.zeros(x1_ref.shape, F32)
            for g in range(3):
                wgt = ex[g] / den
                hi = wgt.astype(BF16)
                lo = (wgt - hi.astype(F32)).astype(BF16)
                wide = (jnp.dot(hi, expand_ref[...], preferred_element_type=F32)
                        + jnp.dot(lo, expand_ref[...], preferred_element_type=F32))
                mixed = mixed + wide * o_tok[g]
            o = mixed.astype(BF16)
        else:
            o = o_ref_in[...]
        mix = jnp.dot(o, wo_ref[...], preferred_element_type=F32)
        x1 = x_ref[...] + mod[:, 2 * D_MODEL:3 * D_MODEL] * mix
        x1_ref[...] = x1
        h2 = _norm_mod(x1, g_ref[0], mod[:, 4 * D_MODEL:5 * D_MODEL], mod[:, 3 * D_MODEL:4 * D_MODEL])
        h2_ref[...] = h2.astype(BF16)
        acc_ref[...] = jnp.zeros(acc_ref.shape, F32)

    a = jnp.dot(h2_ref[...], w1_ref[...], preferred_element_type=F32)
    a = jnp.square(jnp.maximum(a, 0.0)).astype(BF16)
    acc_ref[...] += jnp.dot(a, w2_ref[...], preferred_element_type=F32)

    @pl.when(kf == pl.num_programs(1) - 1)
    def _():
        mod = mod_ref[0]
        x2 = x1_ref[...] + mod[:, 5 * D_MODEL:6 * D_MODEL] * acc_ref[...]
        if final:
            ms = jnp.mean(x2 * x2, axis=-1, keepdims=True)
            x2 = x2 * lax.rsqrt(ms + EPS) * fg_ref[...]
        out_ref[...] = x2


def _head_expand_matrix():
    e = np.zeros((LANES, D_MODEL), np.float32)
    for h in range(HEADS_AB):
        e[h, h * HEAD_DIM_AB:(h + 1) * HEAD_DIM_AB] = 1.0
    return e


def _post(x2d, attn, mods3, norm_g3, wo, w1, w2, final_g, *, layer, mod_row0, seq, merge, final):
    t = x2d.shape[0]
    tm, tf = TM_POST, TF_POST
    per_seq = seq // tm
    row = lambda i, k: (i, 0)
    in_specs = [pl.BlockSpec((tm, D_MODEL), row)]
    args = [x2d]
    scratch = [pltpu.VMEM((tm, D_MODEL), F32), pltpu.VMEM((tm, D_MODEL), BF16), pltpu.VMEM((tm, D_MODEL), F32)]
    if merge:
        outs, lses = attn
        def res_spec(arr):
            dil, width = arr.shape[1], arr.shape[3]
            assert tm % dil == 0
            return pl.BlockSpec((None, dil, tm // dil, width), lambda i, k: (i // per_seq, 0, i % per_seq, 0))
        in_specs += [res_spec(a) for a in outs] + [res_spec(a) for a in lses]
        in_specs += [pl.BlockSpec((LANES, D_MODEL), lambda i, k: (0, 0))]
        args += list(outs) + list(lses) + [jnp.asarray(_head_expand_matrix(), BF16)]
        for _ in outs:
            scratch += [pltpu.VMEM((D_MODEL // LANES, tm, LANES), F32), pltpu.VMEM((tm, LANES), F32)]
    else:
        in_specs += [pl.BlockSpec((tm, D_MODEL), row)]
        args += [attn]
    in_specs += [
        pl.BlockSpec((1, 1, 6 * D_MODEL), lambda i, k: (layer * MOD_ROWS + mod_row0 + i // per_seq, 0, 0)),
        pl.BlockSpec((1, 1, D_MODEL), lambda i, k: (2 * layer + 1, 0, 0)),
        pl.BlockSpec((D_MODEL, D_MODEL), lambda i, k: (0, 0)),
        pl.BlockSpec((D_MODEL, tf), lambda i, k: (0, k)),
        pl.BlockSpec((tf, D_MODEL), lambda i, k: (k, 0)),
    ]
    args += [mods3, norm_g3, wo, w1, w2]
    if final:
        in_specs += [pl.BlockSpec((1, D_MODEL), lambda i, k: (0, 0))]
        args += [final_g.reshape(1, D_MODEL)]
    return pl.pallas_call(
        functools.partial(_post_kernel, merge=merge, final=final),
        grid=(t // tm, D_FF // tf),
        in_specs=in_specs,
        out_specs=pl.BlockSpec((tm, D_MODEL), row),
        out_shape=jax.ShapeDtypeStruct((t, D_MODEL), F32),
        scratch_shapes=scratch,
        compiler_params=_cparams(("parallel", "arbitrary")),
        name="wo_mlp_merge" if merge else "wo_mlp",
    )(*args)


def _trunk(x, mods3, mod_row0, norm_g3, final_g, wts, a_bias, b_masks):
    batch, seq, _ = x.shape
    x2d = x.reshape(batch * seq, D_MODEL)
    t = jnp.arange(seq)
    tabs_b = _rope_tables(t, t)
    tabs_c = _rope_tables(t // GRID_W, t % GRID_W)
    for layer in range(DEPTH):
        kind, j = layer % N_MIXERS, layer // N_MIXERS
        common = dict(layer=layer, mod_row0=mod_row0, seq=seq)
        if kind == 0:
            qkv = _qkv_proj(x2d, mods3, norm_g3, wts["a_qkv"][j], mode="a", **common)
            attn = _attn_a(qkv, a_bias[j], batch=batch, seq=seq)
            wo = wts["a_o"][j]
        elif kind == 1:
            groups = []
            for g, (win, dil) in enumerate(B_PAIRS):
                assert win == 2 * B_SIDE * dil
                qkv = _qkv_proj(x2d, mods3, norm_g3, wts["b_qkv"][j], mode="b", cos=tabs_b[0], sin=tabs_b[1],
                                group=g, dil=dil, **common)
                groups.append(_attn_b_group(qkv, b_masks, group=g))
            attn = ([o for o, _ in groups], [l for _, l in groups])
            wo = wts["b_o"][j]
        else:
            qkv = _qkv_proj(x2d, mods3, norm_g3, wts["c_qkv"][j], mode="c", cos=tabs_c[0], sin=tabs_c[1],
                            qg=wts["c_qg"][j], kg=wts["c_kg"][j], **common)
            attn = _attn_c(qkv, batch=batch, seq=seq)
            wo = wts["c_o"][j]
        x2d = _post(x2d, attn, mods3, norm_g3, wo, wts["w1"][layer], wts["w2"][layer], final_g,
                    merge=(kind == 1), final=(layer == DEPTH - 1), **common)
    return x2d.reshape(batch, seq, D_MODEL)


def kernel(x_prompt, x_sample, c_prompt, c_sample, w_mod, b_mod, norm_g, final_g, a_w_qkv, a_rpb, a_w_o,
           b_w_qkv, b_w_o, c_w_qkv, c_q_g, c_k_g, c_w_o, mlp_w1, mlp_w2):
    nb_p, nb_s = c_prompt.shape[0], c_sample.shape[0]
    assert nb_p + nb_s <= MOD_ROWS
    c_all = jnp.concatenate([c_prompt, c_sample, jnp.zeros((MOD_ROWS - nb_p - nb_s, D_MODEL), F32)], axis=0)
    mods3 = _modulation(c_all, w_mod, b_mod).reshape(DEPTH * MOD_ROWS, 1, 6 * D_MODEL)
    norm_g3 = norm_g.reshape(DEPTH * 2, 1, D_MODEL)

    scale_ab = HEAD_DIM_AB ** -0.5
    a_qkv = a_w_qkv.at[:, :, :D_MODEL].multiply(scale_ab).astype(BF16)
    b_qkv = b_w_qkv.reshape(-1, D_MODEL, len(B_PAIRS), 3, D_MODEL).at[:, :, :, 0].multiply(scale_ab)
    b_qkv = b_qkv.reshape(b_w_qkv.shape).astype(BF16)
    wts = {
        "a_qkv": a_qkv, "a_o": a_w_o.astype(BF16),
        "b_qkv": b_qkv, "b_o": b_w_o.astype(BF16),
        "c_qkv": c_w_qkv.astype(BF16), "c_o": c_w_o.astype(BF16),
        "c_qg": (c_q_g * float(C_HEAD_DIM ** -0.5 * np.log2(np.e))).reshape(-1, 1, C_HEAD_DIM),
        "c_kg": c_k_g.reshape(-1, 1, C_HEAD_DIM),
        "w1": mlp_w1.astype(BF16), "w2": mlp_w2.astype(BF16),
    }
    a_bias = [_attn_a_bias(a_rpb[j]) for j in range(a_rpb.shape[0])]
    b_masks = jnp.asarray(_attn_b_masks())

    y_prompt = _trunk(x_prompt, mods3, 0, norm_g3, final_g, wts, a_bias, b_masks)
    y_sample = _trunk(x_sample, mods3, nb_p, norm_g3, final_g, wts, a_bias, b_masks)
    return (y_prompt, y_sample)
```

```python
import functools

import numpy as np
import jax
import jax.numpy as jnp
from jax import lax
from jax.experimental import pallas as pl
from jax.experimental.pallas import tpu as pltpu

F32 = jnp.float32
BF16 = jnp.bfloat16

D_MODEL = 1024
DEPTH = 4
GRID_W = 64
D_FF = 4 * D_MODEL
EPS = 1e-6
ROPE_THETA = 10000.0
N_MIXERS = 3
HEAD_DIM_AB = 64
HEADS_AB = D_MODEL // HEAD_DIM_AB
A_WIN_ROWS = 8
A_WIN_COLS = 16
B_PAIRS = ((128, 1), (512, 4), (2048, 16))
B_SIDE = 64
C_HEAD_DIM = 128
C_Q_HEADS = 8
C_KV_HEADS = 2
C_GROUP = C_Q_HEADS // C_KV_HEADS
C_QKV_WIDTH = (C_Q_HEADS + 2 * C_KV_HEADS) * C_HEAD_DIM
ROPE_HALF = 32
LANES = 128
MOD_ROWS = 8
MASK_NEG = -1e30
VMEM_LIMIT = 56 * 1024 * 1024

TM_QKV = 1024
TN_QKV = 1024
TN_QKV_C = 512
QKV_HALVES = 2
TM_POST = 1024
TM_POST_MERGE = 512
POST_HALVES = 2
TF_POST = 1024
A_TQ = 4 * GRID_W
B_TQ = 256
C_TQ = 512
C_TK = 1024


def _cparams(sem):
    return pltpu.CompilerParams(dimension_semantics=sem, vmem_limit_bytes=VMEM_LIMIT)


def _mod_kernel(c_ref, w_ref, b_ref, o_ref):
    c = c_ref[...]
    act = (c / (1.0 + jnp.exp(-c))).astype(BF16)
    o_ref[0] = jnp.dot(act, w_ref[0].astype(BF16), preferred_element_type=F32) + b_ref[0]


def _modulation(c_all, w_mod, b_mod):
    tn = 1536
    n = 6 * D_MODEL
    return pl.pallas_call(
        _mod_kernel,
        grid=(DEPTH, n // tn),
        in_specs=[
            pl.BlockSpec((MOD_ROWS, D_MODEL), lambda l, j: (0, 0)),
            pl.BlockSpec((1, D_MODEL, tn), lambda l, j: (l, 0, j)),
            pl.BlockSpec((1, 1, tn), lambda l, j: (l, 0, j)),
        ],
        out_specs=pl.BlockSpec((1, MOD_ROWS, tn), lambda l, j: (l, 0, j)),
        out_shape=jax.ShapeDtypeStruct((DEPTH, MOD_ROWS, n), F32),
        compiler_params=_cparams(("parallel", "parallel")),
        name="adaln_modulation",
    )(c_all, w_mod, b_mod.reshape(DEPTH, 1, n))


def _norm_mod(x, gain, scale, shift):
    ms = jnp.mean(x * x, axis=-1, keepdims=True)
    return (x * lax.rsqrt(ms + EPS) * gain) * (1.0 + scale) + shift


def _rope128(x, cos, sin_signed, low_half):
    up = pltpu.roll(x, LANES - ROPE_HALF, 1)
    down = pltpu.roll(x, ROPE_HALF, 1)
    return x * cos + jnp.where(low_half, up, down) * sin_signed


def _low_half_mask():
    lane = lax.broadcasted_iota(jnp.int32, (1, LANES), 1)
    return (lane % (2 * ROPE_HALF)) < ROPE_HALF


def _rope_tables(pos_a, pos_b):
    inv = ROPE_THETA ** (-jnp.arange(ROPE_HALF, dtype=F32) / ROPE_HALF)
    def one(pos):
        ang = pos.astype(F32)[:, None] * inv[None, :]
        c, s = jnp.cos(ang), jnp.sin(ang)
        return jnp.concatenate([c, c], axis=1), jnp.concatenate([-s, s], axis=1)
    ca, sa = one(pos_a)
    cb, sb = one(pos_b)
    return jnp.concatenate([ca, cb], axis=1), jnp.concatenate([sa, sb], axis=1)


def _qkv_kernel(*refs, mode, dil, halves, tn):
    stage_refs = ()
    if mode == "a":
        x_ref, mod_ref, g_ref, w_ref, o_ref = refs
    elif mode == "b":
        x_ref, mod_ref, g_ref, w_ref, cos_ref, sin_ref, o_ref = refs[:7]
        stage_refs = refs[7:]
    else:
        x_ref, mod_ref, g_ref, w_ref, cos_ref, sin_ref, qg_ref, kg_ref, o_ref = refs
    mod = mod_ref[0]
    hm = x_ref.shape[0] // halves
    n = w_ref.shape[1]
    low = _low_half_mask()

    def normed(hf):
        rows = slice(hf * hm, (hf + 1) * hm)
        return _norm_mod(x_ref[rows, :], g_ref[0], mod[:, D_MODEL:2 * D_MODEL], mod[:, 0:D_MODEL]).astype(BF16)

    def emit(hf, j, acc):
        rows = slice(hf * hm, (hf + 1) * hm)
        if mode == "a":
            o_ref[rows, j * tn:(j + 1) * tn] = acc.astype(BF16)
            return
        cos, sin = cos_ref[rows, :], sin_ref[rows, :]
        for c in range(tn // LANES):
            col = j * tn + c * LANES
            val = acc[:, c * LANES:(c + 1) * LANES]
            if mode == "c":
                head = col // C_HEAD_DIM
                if head < C_Q_HEADS + C_KV_HEADS:
                    gain = qg_ref[...] if head < C_Q_HEADS else kg_ref[...]
                    ms = jnp.mean(val * val, axis=-1, keepdims=True)
                    val = _rope128(val * lax.rsqrt(ms + EPS) * gain, cos, sin, low)
                o_ref[rows, col:col + LANES] = val.astype(BF16)
                continue
            if col < 2 * D_MODEL:
                val = _rope128(val, cos, sin, low)
            sub = hm // dil
            dst = slice(hf * sub, (hf + 1) * sub)
            if dil == 1:
                o_ref[0, 0, dst, col:col + LANES] = val.astype(BF16)
                continue
            stage = stage_refs[hf]
            stage[c] = val
            for r in range(dil):
                o_ref[0, r, dst, col:col + LANES] = stage[c, pl.ds(r, sub, stride=dil), :].astype(BF16)

    hs = [normed(hf) for hf in range(halves)]
    for hf in range(halves):
        for j in range(n // tn):
            emit(hf, j, jnp.dot(hs[hf], w_ref[:, j * tn:(j + 1) * tn], preferred_element_type=F32))


def _qkv_proj(x2d, mods3, norm_g3, w, *, layer, mod_row0, seq, mode, cos=None, sin=None, qg=None, kg=None,
              group=0, dil=1):
    t = x2d.shape[0]
    tm = TM_QKV
    if mode == "b":
        n, col0 = 3 * D_MODEL, group
    else:
        n, col0 = w.shape[1], 0
    tn = TN_QKV_C if mode == "c" else TN_QKV
    hm = tm // QKV_HALVES
    per_seq = seq // tm
    in_specs = [
        pl.BlockSpec((tm, D_MODEL), lambda i: (i, 0)),
        pl.BlockSpec((1, 1, 6 * D_MODEL), lambda i: (layer * MOD_ROWS + mod_row0 + i // per_seq, 0, 0)),
        pl.BlockSpec((1, 1, D_MODEL), lambda i: (2 * layer, 0, 0)),
        pl.BlockSpec((D_MODEL, n), lambda i: (0, col0), pipeline_mode=pl.Buffered(1)),
    ]
    args = [x2d, mods3, norm_g3, w]
    if mode in ("b", "c"):
        in_specs += [pl.BlockSpec((tm, LANES), lambda i: (i % per_seq, 0))] * 2
        args += [cos, sin]
    if mode == "c":
        in_specs += [pl.BlockSpec((1, C_HEAD_DIM), lambda i: (0, 0))] * 2
        args += [qg, kg]
    scratch = []
    if mode == "b":
        assert hm % (dil * 16) == 0
        batch = t // seq
        out_spec = pl.BlockSpec((1, dil, tm // dil, n), lambda i: (i // per_seq, 0, i % per_seq, 0))
        out_shape = jax.ShapeDtypeStruct((batch, dil, seq // dil, n), BF16)
        if dil > 1:
            scratch = [pltpu.VMEM((tn // LANES, hm, LANES), F32) for _ in range(QKV_HALVES)]
    else:
        out_spec = pl.BlockSpec((tm, n), lambda i: (i, 0))
        out_shape = jax.ShapeDtypeStruct((t, n), BF16)
    return pl.pallas_call(
        functools.partial(_qkv_kernel, mode=mode, dil=dil, halves=QKV_HALVES, tn=tn),
        grid=(t // tm,),
        in_specs=in_specs,
        out_specs=out_spec,
        out_shape=out_shape,
        scratch_shapes=scratch,
        compiler_params=_cparams(("parallel",)),
        name="norm_qkv_" + mode,
    )(*args)


def _pair_masks():
    lane = lax.broadcasted_iota(jnp.int32, (1, LANES), 1)
    first = lane < HEAD_DIM_AB
    return first, jnp.logical_not(first)


def _dot_nt(a, b):
    return lax.dot_general(a, b, (((1,), (1,)), ((), ())), preferred_element_type=F32)


def _pipelined_heads(scores, finish):
    nxt = scores(0)
    for head in range(HEADS_AB):
        cur = nxt
        if head + 1 < HEADS_AB:
            nxt = scores(head + 1)
        finish(head, cur)


def _attn_a_kernel(q_ref, kp_ref, kc_ref, kn_ref, vp_ref, vc_ref, vn_ref, bias_ref, o_ref):
    masks = _pair_masks()
    k_refs = (kp_ref, kc_ref, kn_ref)
    v_refs = (vp_ref, vc_ref, vn_ref)
    outs = {}

    def scores(head):
        sl = slice((head // 2) * LANES, (head // 2 + 1) * LANES)
        qp = q_ref[:, sl]
        qm = jnp.where(masks[head % 2], qp, jnp.zeros_like(qp))
        return _dot_nt(qm, jnp.concatenate([r[:, sl] for r in k_refs], axis=0)) + bias_ref[0, head]

    def finish(head, s):
        sl = slice((head // 2) * LANES, (head // 2 + 1) * LANES)
        m = s.max(axis=-1, keepdims=True)
        pr = jnp.exp(s - m)
        l = pr.sum(axis=-1, keepdims=True)
        vcat = jnp.concatenate([r[:, sl] for r in v_refs], axis=0)
        outs[head] = jnp.dot(pr.astype(BF16), vcat, preferred_element_type=F32) / l
        if head % 2 == 1:
            o_ref[:, sl] = jnp.where(masks[0], outs.pop(head - 1), outs.pop(head)).astype(BF16)

    _pipelined_heads(scores, finish)


def _attn_a_bias(rpb):
    rows_per = A_TQ // GRID_W
    n_dr, n_dc = 2 * A_WIN_ROWS - 1, 2 * A_WIN_COLS - 1
    ql, kl = np.arange(rows_per)[:, None], np.arange(3 * rows_per)[None, :]
    c, kc = np.arange(GRID_W)[:, None], np.arange(GRID_W)[None, :]
    dr = kl - rows_per - ql + A_WIN_ROWS - 1
    dc = kc - c + A_WIN_COLS - 1
    onehot_r = (dr[..., None] == np.arange(n_dr)).astype(np.float32)
    onehot_c = (dc[..., None] == np.arange(n_dc)).astype(np.float32)
    table = jnp.einsum("hde,qkd,cje->hqckj", rpb, onehot_r, onehot_c, precision=lax.Precision.HIGHEST)
    cs = np.clip(c - A_WIN_COLS // 2, 0, GRID_W - A_WIN_COLS)
    col_ok = (kc >= cs) & (kc < cs + A_WIN_COLS)
    rows = 8 * rows_per
    out = []
    for i in (0, 1, rows // rows_per - 1):
        r = rows_per * i + ql
        rs = np.clip(r - A_WIN_ROWS // 2, 0, rows - A_WIN_ROWS)
        kr = rows_per * (i - 1) + kl
        row_ok = (kr >= rs) & (kr < rs + A_WIN_ROWS)
        ok = row_ok[:, None, :, None] & col_ok[None, :, None, :]
        out.append(jnp.where(jnp.asarray(ok)[None], table, MASK_NEG).reshape(HEADS_AB, A_TQ, 3 * A_TQ))
    return jnp.stack(out)


def _attn_a(qkv, bias, *, batch, seq):
    tq = A_TQ
    nb = seq // tq
    assert nb >= 3
    q_spec = pl.BlockSpec((tq, D_MODEL), lambda b, i: (b * nb + i, 0))
    def kv_spec(col, off):
        return pl.BlockSpec((tq, D_MODEL), lambda b, i: (b * nb + jnp.clip(i + off, 0, nb - 1), col))
    bias_spec = pl.BlockSpec((1, HEADS_AB, tq, 3 * tq),
                             lambda b, i: (jnp.where(i == 0, 0, jnp.where(i == nb - 1, 2, 1)), 0, 0, 0))
    return pl.pallas_call(
        _attn_a_kernel,
        grid=(batch, nb),
        in_specs=[q_spec, kv_spec(1, -1), kv_spec(1, 0), kv_spec(1, 1),
                  kv_spec(2, -1), kv_spec(2, 0), kv_spec(2, 1), bias_spec],
        out_specs=pl.BlockSpec((tq, D_MODEL), lambda b, i: (b * nb + i, 0)),
        out_shape=jax.ShapeDtypeStruct((batch * seq, D_MODEL), BF16),
        compiler_params=_cparams(("parallel", "arbitrary")),
        name="attn_neighbourhood",
    )(qkv, qkv, qkv, qkv, qkv, qkv, qkv, bias)


def _attn_b_kernel(q_ref, kp_ref, kc_ref, kn_ref, vp_ref, vc_ref, vn_ref, mask_ref, o_ref, lse_ref):
    tq = q_ref.shape[0]
    masks = _pair_masks()
    lane = lax.broadcasted_iota(jnp.int32, (1, LANES), 1)
    outs = {}
    lse_tile = [jnp.zeros((tq, LANES), F32)]

    def window(prev_ref, cur_ref, next_ref, sl):
        return jnp.concatenate([prev_ref[tq - B_SIDE:, sl], cur_ref[:, sl], next_ref[:B_SIDE, sl]], axis=0)

    def scores(head):
        sl = slice((head // 2) * LANES, (head // 2 + 1) * LANES)
        qp = q_ref[:, sl]
        qm = jnp.where(masks[head % 2], qp, jnp.zeros_like(qp))
        return _dot_nt(qm, window(kp_ref, kc_ref, kn_ref, sl)) + mask_ref[0]

    def finish(head, s):
        sl = slice((head // 2) * LANES, (head // 2 + 1) * LANES)
        m = s.max(axis=-1, keepdims=True)
        pr = jnp.exp(s - m)
        l = pr.sum(axis=-1, keepdims=True)
        outs[head] = jnp.dot(pr.astype(BF16), window(vp_ref, vc_ref, vn_ref, sl), preferred_element_type=F32) / l
        lse_tile[0] = jnp.where(lane == head, m + jnp.log(l), lse_tile[0])
        if head % 2 == 1:
            o_ref[:, sl] = jnp.where(masks[0], outs.pop(head - 1), outs.pop(head)).astype(BF16)

    _pipelined_heads(scores, finish)
    lse_ref[...] = lse_tile[0]


def _attn_b_masks():
    qq = np.arange(B_TQ)[:, None]
    jj = np.arange(B_TQ + 2 * B_SIDE)[None, :]
    band = (jj - qq >= 0) & (jj - qq <= 2 * B_SIDE)
    out = []
    for ty in range(4):
        ok = band
        if ty & 1:
            ok = ok & (jj >= B_SIDE)
        if ty & 2:
            ok = ok & (jj < B_SIDE + B_TQ)
        out.append(np.where(ok, 0.0, MASK_NEG))
    return np.stack(out).astype(np.float32)


def _attn_b_group(qkv, masks, *, group):
    batch, dil, sub, _ = qkv.shape
    tq = B_TQ
    nb = sub // tq
    assert sub % tq == 0
    q_spec = pl.BlockSpec((None, None, tq, D_MODEL), lambda b, r, i: (b, r, i, 0))
    def kv_spec(which, off):
        return pl.BlockSpec((None, None, tq, D_MODEL),
                            lambda b, r, i: (b, r, jnp.clip(i + off, 0, nb - 1), which))
    mask_spec = pl.BlockSpec((1, tq, tq + 2 * B_SIDE),
                             lambda b, r, i: ((i == 0).astype(jnp.int32) + 2 * (i == nb - 1).astype(jnp.int32), 0, 0))
    return pl.pallas_call(
        _attn_b_kernel,
        grid=(batch, dil, nb),
        in_specs=[q_spec, kv_spec(1, -1), kv_spec(1, 0), kv_spec(1, 1),
                  kv_spec(2, -1), kv_spec(2, 0), kv_spec(2, 1), mask_spec],
        out_specs=[pl.BlockSpec((None, None, tq, D_MODEL), lambda b, r, i: (b, r, i, 0)),
                   pl.BlockSpec((None, None, tq, LANES), lambda b, r, i: (b, r, i, 0))],
        out_shape=[jax.ShapeDtypeStruct((batch, dil, sub, D_MODEL), BF16),
                   jax.ShapeDtypeStruct((batch, dil, sub, LANES), F32)],
        compiler_params=_cparams(("parallel", "parallel", "arbitrary")),
        name="attn_dilated_g%d" % group,
    )(qkv, qkv, qkv, qkv, qkv, qkv, qkv, masks)


def _attn_c_kernel(q_ref, k_ref, v_ref, o_ref, s_ref, m_ref, l_ref, acc_ref, *, tk):
    nk = k_ref.shape[0] // tk
    m_ref[...] = jnp.full(m_ref.shape, MASK_NEG, F32)
    l_ref[...] = jnp.zeros(l_ref.shape, F32)
    acc_ref[...] = jnp.zeros(acc_ref.shape, F32)

    def scores(h, c):
        start = pl.multiple_of(c * tk, tk)
        return _dot_nt(q_ref[:, h * C_HEAD_DIM:(h + 1) * C_HEAD_DIM], k_ref[pl.ds(start, tk), :])

    s_ref[0] = scores(0, 0)

    def chunk(c, carry):
        v = v_ref[pl.ds(pl.multiple_of(c * tk, tk), tk), :]
        for h in range(C_GROUP):
            if h + 1 < C_GROUP:
                s_ref[(h + 1) % 2] = scores(h + 1, c)
            else:
                s_ref[(h + 1) % 2] = scores(0, jnp.minimum(c + 1, nk - 1))
            s = s_ref[h % 2]
            m_prev = m_ref[h]
            m_new = jnp.maximum(m_prev, s.max(axis=-1, keepdims=True))
            alpha = jnp.exp2(m_prev - m_new)
            pr = jnp.exp2(s - m_new)
            l_ref[h] = alpha * l_ref[h] + pr.sum(axis=-1, keepdims=True)
            acc_ref[h] = alpha * acc_ref[h] + jnp.dot(pr.astype(BF16), v, preferred_element_type=F32)
            m_ref[h] = m_new
        return carry

    lax.fori_loop(0, nk, chunk, 0)
    for h in range(C_GROUP):
        o_ref[:, h * C_HEAD_DIM:(h + 1) * C_HEAD_DIM] = (acc_ref[h] / l_ref[h]).astype(BF16)


def _attn_c(qkv, *, batch, seq):
    tq, tk = C_TQ, C_TK
    nq = seq // tq
    assert seq % tk == 0 and C_GROUP % 2 == 0
    qw = C_GROUP * C_HEAD_DIM
    k_col0 = C_Q_HEADS
    v_col0 = C_Q_HEADS + C_KV_HEADS
    return pl.pallas_call(
        functools.partial(_attn_c_kernel, tk=tk),
        grid=(batch, C_KV_HEADS, nq),
        in_specs=[
            pl.BlockSpec((tq, qw), lambda b, g, i: (b * nq + i, g)),
            pl.BlockSpec((seq, C_HEAD_DIM), lambda b, g, i: (b, k_col0 + g)),
            pl.BlockSpec((seq, C_HEAD_DIM), lambda b, g, i: (b, v_col0 + g)),
        ],
        out_specs=pl.BlockSpec((tq, qw), lambda b, g, i: (b * nq + i, g)),
        out_shape=jax.ShapeDtypeStruct((batch * seq, C_Q_HEADS * C_HEAD_DIM), BF16),
        scratch_shapes=[pltpu.VMEM((2, tq, tk), F32),
                        pltpu.VMEM((C_GROUP, tq, 1), F32), pltpu.VMEM((C_GROUP, tq, 1), F32),
                        pltpu.VMEM((C_GROUP, tq, C_HEAD_DIM), F32)],
        compiler_params=_cparams(("parallel", "parallel", "arbitrary")),
        name="attn_gqa_flash",
    )(qkv, qkv, qkv)


def _post_kernel(*refs, merge, final, halves, tf):
    refs = list(refs)
    x_ref = refs.pop(0)
    if merge:
        o_refs = [refs.pop(0) for _ in range(3)]
        lse_refs = [refs.pop(0) for _ in range(3)]
        expand_ref = refs.pop(0)
    else:
        o_ref_in = refs.pop(0)
    mod_ref, g_ref, wo_ref, w1_ref, w2_ref = [refs.pop(0) for _ in range(5)]
    fg_ref = refs.pop(0) if final else None
    out_ref = refs.pop(0)
    stage_refs = refs
    mod = mod_ref[0]
    hm = x_ref.shape[0] // halves

    def merged_groups(hf):
        o_tok, lse = [], []
        for g, (o_g, lse_g) in enumerate(zip(o_refs, lse_refs)):
            dil = o_g.shape[0]
            rows = hm // dil
            src = slice(hf * rows, (hf + 1) * rows)
            if dil == 1:
                o_tok.append(o_g[0, src, :].astype(F32))
                lse.append(lse_g[0, src, :])
                continue
            o_stage, lse_stage = stage_refs[4 * hf + 2 * (g - 1)], stage_refs[4 * hf + 2 * (g - 1) + 1]
            chunks = o_g.shape[2] // LANES
            for r in range(dil):
                lse_stage[pl.ds(r, rows, stride=dil), :] = lse_g[r, src, :]
                for c in range(chunks):
                    o_stage[c, pl.ds(r, rows, stride=dil), :] = o_g[r, src, c * LANES:(c + 1) * LANES].astype(F32)
            o_tok.append(jnp.concatenate([o_stage[c] for c in range(chunks)], axis=1))
            lse.append(lse_stage[...])
        top = jnp.maximum(jnp.maximum(lse[0], lse[1]), lse[2])
        ex = [jnp.exp(v - top) for v in lse]
        den = ex[0] + ex[1] + ex[2]
        mixed = None
        for g in range(3):
            wgt = ex[g] / den
            hi = wgt.astype(BF16)
            lo = (wgt - hi.astype(F32)).astype(BF16)
            wide = (jnp.dot(hi, expand_ref[...], preferred_element_type=F32)
                    + jnp.dot(lo, expand_ref[...], preferred_element_type=F32))
            mixed = wide * o_tok[g] if mixed is None else mixed + wide * o_tok[g]
        return mixed.astype(BF16)

    def pre(hf):
        rows = slice(hf * hm, (hf + 1) * hm)
        o = merged_groups(hf) if merge else o_ref_in[rows, :]
        mix = jnp.dot(o, wo_ref[...], preferred_element_type=F32)
        x1 = x_ref[rows, :] + mod[:, 2 * D_MODEL:3 * D_MODEL] * mix
        h2 = _norm_mod(x1, g_ref[0], mod[:, 4 * D_MODEL:5 * D_MODEL], mod[:, 3 * D_MODEL:4 * D_MODEL])
        return x1, h2.astype(BF16)

    def mlp(h2):
        acc = None
        for c in range(D_FF // tf):
            a = jnp.dot(h2, w1_ref[:, c * tf:(c + 1) * tf], preferred_element_type=F32)
            a = jnp.square(jnp.maximum(a, 0.0)).astype(BF16)
            d = jnp.dot(a, w2_ref[c * tf:(c + 1) * tf, :], preferred_element_type=F32)
            acc = d if acc is None else acc + d
        return acc

    pres = [pre(hf) for hf in range(halves)]
    for hf, (x1, h2) in enumerate(pres):
        x2 = x1 + mod[:, 5 * D_MODEL:6 * D_MODEL] * mlp(h2)
        if final:
            ms = jnp.mean(x2 * x2, axis=-1, keepdims=True)
            x2 = x2 * lax.rsqrt(ms + EPS) * fg_ref[...]
        out_ref[hf * hm:(hf + 1) * hm, :] = x2


def _head_expand_matrix():
    e = np.zeros((LANES, D_MODEL), np.float32)
    for h in range(HEADS_AB):
        e[h, h * HEAD_DIM_AB:(h + 1) * HEAD_DIM_AB] = 1.0
    return e


def _post(x2d, attn, mods3, norm_g3, wo, w1, w2, final_g, *, layer, mod_row0, seq, merge, final):
    t = x2d.shape[0]
    tm = TM_POST_MERGE if merge else TM_POST
    hm = tm // POST_HALVES
    per_seq = seq // tm
    row = lambda i: (i, 0)
    const = lambda i: (0, 0)
    resident = pl.Buffered(1)
    in_specs = [pl.BlockSpec((tm, D_MODEL), row)]
    args = [x2d]
    scratch = []
    if merge:
        outs, lses = attn
        def res_spec(arr):
            dil, width = arr.shape[1], arr.shape[3]
            assert hm % (dil * 16) == 0
            return pl.BlockSpec((None, dil, tm // dil, width), lambda i: (i // per_seq, 0, i % per_seq, 0))
        in_specs += [res_spec(a) for a in outs] + [res_spec(a) for a in lses]
        in_specs += [pl.BlockSpec((LANES, D_MODEL), const, pipeline_mode=resident)]
        args += list(outs) + list(lses) + [jnp.asarray(_head_expand_matrix(), BF16)]
        for _ in range(POST_HALVES * (len(outs) - 1)):
            scratch += [pltpu.VMEM((D_MODEL // LANES, hm, LANES), F32), pltpu.VMEM((hm, LANES), F32)]
    else:
        in_specs += [pl.BlockSpec((tm, D_MODEL), row)]
        args += [attn]
    in_specs += [
        pl.BlockSpec((1, 1, 6 * D_MODEL), lambda i: (layer * MOD_ROWS + mod_row0 + i // per_seq, 0, 0)),
        pl.BlockSpec((1, 1, D_MODEL), lambda i: (2 * layer + 1, 0, 0)),
        pl.BlockSpec((D_MODEL, D_MODEL), const, pipeline_mode=resident),
        pl.BlockSpec((D_MODEL, D_FF), const, pipeline_mode=resident),
        pl.BlockSpec((D_FF, D_MODEL), const, pipeline_mode=resident),
    ]
    args += [mods3, norm_g3, wo, w1, w2]
    if final:
        in_specs += [pl.BlockSpec((1, D_MODEL), const)]
        args += [final_g.reshape(1, D_MODEL)]
    return pl.pallas_call(
        functools.partial(_post_kernel, merge=merge, final=final, halves=POST_HALVES, tf=TF_POST),
        grid=(t // tm,),
        in_specs=in_specs,
        out_specs=pl.BlockSpec((tm, D_MODEL), row),
        out_shape=jax.ShapeDtypeStruct((t, D_MODEL), F32),
        scratch_shapes=scratch,
        compiler_params=_cparams(("parallel",)),
        name="wo_mlp_merge" if merge else "wo_mlp",
    )(*args)


def _trunk(x, mods3, mod_row0, norm_g3, final_g, wts, a_bias, b_masks):
    batch, seq, _ = x.shape
    x2d = x.reshape(batch * seq, D_MODEL)
    t = jnp.arange(seq)
    tabs_b = _rope_tables(t, t)
    tabs_c = _rope_tables(t // GRID_W, t % GRID_W)
    for layer in range(DEPTH):
        kind, j = layer % N_MIXERS, layer // N_MIXERS
        common = dict(layer=layer, mod_row0=mod_row0, seq=seq)
        if kind == 0:
            qkv = _qkv_proj(x2d, mods3, norm_g3, wts["a_qkv"][j], mode="a", **common)
            attn = _attn_a(qkv, a_bias[j], batch=batch, seq=seq)
            wo = wts["a_o"][j]
        elif kind == 1:
            groups = []
            for g, (win, dil) in enumerate(B_PAIRS):
                assert win == 2 * B_SIDE * dil
                qkv = _qkv_proj(x2d, mods3, norm_g3, wts["b_qkv"][j], mode="b", cos=tabs_b[0], sin=tabs_b[1],
                                group=g, dil=dil, **common)
                groups.append(_attn_b_group(qkv, b_masks, group=g))
            attn = ([o for o, _ in groups], [l for _, l in groups])
            wo = wts["b_o"][j]
        else:
            qkv = _qkv_proj(x2d, mods3, norm_g3, wts["c_qkv"][j], mode="c", cos=tabs_c[0], sin=tabs_c[1],
                            qg=wts["c_qg"][j], kg=wts["c_kg"][j], **common)
            attn = _attn_c(qkv, batch=batch, seq=seq)
            wo = wts["c_o"][j]
        x2d = _post(x2d, attn, mods3, norm_g3, wo, wts["w1"][layer], wts["w2"][layer], final_g,
                    merge=(kind == 1), final=(layer == DEPTH - 1), **common)
    return x2d.reshape(batch, seq, D_MODEL)


def kernel(x_prompt, x_sample, c_prompt, c_sample, w_mod, b_mod, norm_g, final_g, a_w_qkv, a_rpb, a_w_o,
           b_w_qkv, b_w_o, c_w_qkv, c_q_g, c_k_g, c_w_o, mlp_w1, mlp_w2):
    nb_p, nb_s = c_prompt.shape[0], c_sample.shape[0]
    assert nb_p + nb_s <= MOD_ROWS
    c_all = jnp.concatenate([c_prompt, c_sample, jnp.zeros((MOD_ROWS - nb_p - nb_s, D_MODEL), F32)], axis=0)
    mods3 = _modulation(c_all, w_mod, b_mod).reshape(DEPTH * MOD_ROWS, 1, 6 * D_MODEL)
    norm_g3 = norm_g.reshape(DEPTH * 2, 1, D_MODEL)

    scale_ab = HEAD_DIM_AB ** -0.5
    a_qkv = a_w_qkv.at[:, :, :D_MODEL].multiply(scale_ab).astype(BF16)
    b_qkv = b_w_qkv.reshape(-1, D_MODEL, len(B_PAIRS), 3, D_MODEL).at[:, :, :, 0].multiply(scale_ab)
    b_qkv = b_qkv.reshape(b_w_qkv.shape).astype(BF16)
    wts = {
        "a_qkv": a_qkv, "a_o": a_w_o.astype(BF16),
        "b_qkv": b_qkv, "b_o": b_w_o.astype(BF16),
        "c_qkv": c_w_qkv.astype(BF16), "c_o": c_w_o.astype(BF16),
        "c_qg": (c_q_g * float(C_HEAD_DIM ** -0.5 * np.log2(np.e))).reshape(-1, 1, C_HEAD_DIM),
        "c_kg": c_k_g.reshape(-1, 1, C_HEAD_DIM),
        "w1": mlp_w1.astype(BF16), "w2": mlp_w2.astype(BF16),
    }
    a_bias = [_attn_a_bias(a_rpb[j]) for j in range(a_rpb.shape[0])]
    b_masks = jnp.asarray(_attn_b_masks())

    y_prompt = _trunk(x_prompt, mods3, 0, norm_g3, final_g, wts, a_bias, b_masks)
    y_sample = _trunk(x_sample, mods3, nb_p, norm_g3, final_g, wts, a_bias, b_masks)
    return (y_prompt, y_sample)
```

```python
import functools

import numpy as np
import jax
import jax.numpy as jnp
from jax import lax
from jax.experimental import pallas as pl
from jax.experimental.pallas import tpu as pltpu

F32 = jnp.float32
BF16 = jnp.bfloat16

D_MODEL = 1024
DEPTH = 4
GRID_W = 64
D_FF = 4 * D_MODEL
EPS = 1e-6
ROPE_THETA = 10000.0
N_MIXERS = 3
HEAD_DIM_AB = 64
HEADS_AB = D_MODEL // HEAD_DIM_AB
A_WIN_ROWS = 8
A_WIN_COLS = 16
B_PAIRS = ((128, 1), (512, 4), (2048, 16))
B_SIDE = 64
C_HEAD_DIM = 128
C_Q_HEADS = 8
C_KV_HEADS = 2
C_GROUP = C_Q_HEADS // C_KV_HEADS
C_QKV_WIDTH = (C_Q_HEADS + 2 * C_KV_HEADS) * C_HEAD_DIM
ROPE_HALF = 32
LANES = 128
MOD_ROWS = 8
MASK_NEG = -1e30
VMEM_LIMIT = 56 * 1024 * 1024

TM_QKV = 1024
TN_QKV = 1024
TN_QKV_C = 512
QKV_HALVES = 2
TM_POST = 1024
TM_POST_MERGE = 512
POST_HALVES = 2
TF_POST = 1024
A_TQ = 4 * GRID_W
B_TQ = 256
C_TQ = 512
C_TK = 1024


def _cparams(sem):
    return pltpu.CompilerParams(dimension_semantics=sem, vmem_limit_bytes=VMEM_LIMIT)


def _mod_kernel(c_ref, w_ref, b_ref, o_ref):
    c = c_ref[...]
    act = (c / (1.0 + jnp.exp(-c))).astype(BF16)
    o_ref[0] = jnp.dot(act, w_ref[0].astype(BF16), preferred_element_type=F32) + b_ref[0]


def _modulation(c_all, w_mod, b_mod):
    tn = 1536
    n = 6 * D_MODEL
    return pl.pallas_call(
        _mod_kernel,
        grid=(DEPTH, n // tn),
        in_specs=[
            pl.BlockSpec((MOD_ROWS, D_MODEL), lambda l, j: (0, 0)),
            pl.BlockSpec((1, D_MODEL, tn), lambda l, j: (l, 0, j)),
            pl.BlockSpec((1, 1, tn), lambda l, j: (l, 0, j)),
        ],
        out_specs=pl.BlockSpec((1, MOD_ROWS, tn), lambda l, j: (l, 0, j)),
        out_shape=jax.ShapeDtypeStruct((DEPTH, MOD_ROWS, n), F32),
        compiler_params=_cparams(("parallel", "parallel")),
        name="adaln_modulation",
    )(c_all, w_mod, b_mod.reshape(DEPTH, 1, n))


def _norm_mod(x, gain, scale, shift):
    ms = jnp.mean(x * x, axis=-1, keepdims=True)
    return (x * lax.rsqrt(ms + EPS) * gain) * (1.0 + scale) + shift


def _rope128(x, cos, sin_signed, low_half):
    up = pltpu.roll(x, LANES - ROPE_HALF, 1)
    down = pltpu.roll(x, ROPE_HALF, 1)
    return x * cos + jnp.where(low_half, up, down) * sin_signed


def _low_half_mask():
    lane = lax.broadcasted_iota(jnp.int32, (1, LANES), 1)
    return (lane % (2 * ROPE_HALF)) < ROPE_HALF


def _rope_tables(pos_a, pos_b):
    inv = ROPE_THETA ** (-jnp.arange(ROPE_HALF, dtype=F32) / ROPE_HALF)
    def one(pos):
        ang = pos.astype(F32)[:, None] * inv[None, :]
        c, s = jnp.cos(ang), jnp.sin(ang)
        return jnp.concatenate([c, c], axis=1), jnp.concatenate([-s, s], axis=1)
    ca, sa = one(pos_a)
    cb, sb = one(pos_b)
    return jnp.concatenate([ca, cb], axis=1), jnp.concatenate([sa, sb], axis=1)


def _qkv_kernel(*refs, mode, dil, halves, tn):
    stage_refs = ()
    if mode == "a":
        x_ref, mod_ref, g_ref, w_ref, o_ref = refs
    elif mode == "b":
        x_ref, mod_ref, g_ref, w_ref, cos_ref, sin_ref, o_ref = refs[:7]
        stage_refs = refs[7:]
    else:
        x_ref, mod_ref, g_ref, w_ref, cos_ref, sin_ref, qg_ref, kg_ref, o_ref, vt_ref = refs
    mod = mod_ref[0]
    hm = x_ref.shape[0] // halves
    n = w_ref.shape[1]
    low = _low_half_mask()

    def normed(hf):
        rows = slice(hf * hm, (hf + 1) * hm)
        return _norm_mod(x_ref[rows, :], g_ref[0], mod[:, D_MODEL:2 * D_MODEL], mod[:, 0:D_MODEL]).astype(BF16)

    def emit(hf, j, acc):
        rows = slice(hf * hm, (hf + 1) * hm)
        if mode == "a":
            o_ref[rows, j * tn:(j + 1) * tn] = acc.astype(BF16)
            return
        cos, sin = cos_ref[rows, :], sin_ref[rows, :]
        for c in range(tn // LANES):
            col = j * tn + c * LANES
            val = acc[:, c * LANES:(c + 1) * LANES]
            if mode == "c":
                head = col // C_HEAD_DIM
                if head < C_Q_HEADS + C_KV_HEADS:
                    gain = qg_ref[...] if head < C_Q_HEADS else kg_ref[...]
                    ms = jnp.mean(val * val, axis=-1, keepdims=True)
                    val = _rope128(val * lax.rsqrt(ms + EPS) * gain, cos, sin, low)
                    o_ref[rows, col:col + LANES] = val.astype(BF16)
                else:
                    kv = head - C_Q_HEADS - C_KV_HEADS
                    vt_ref[kv * C_HEAD_DIM:(kv + 1) * C_HEAD_DIM, rows] = val.T.astype(BF16)
                continue
            if col < 2 * D_MODEL:
                val = _rope128(val, cos, sin, low)
            sub = hm // dil
            dst = slice(hf * sub, (hf + 1) * sub)
            if dil == 1:
                o_ref[0, 0, dst, col:col + LANES] = val.astype(BF16)
                continue
            stage = stage_refs[hf]
            stage[c] = val
            for r in range(dil):
                o_ref[0, r, dst, col:col + LANES] = stage[c, pl.ds(r, sub, stride=dil), :].astype(BF16)

    hs = [normed(hf) for hf in range(halves)]
    for hf in range(halves):
        for j in range(n // tn):
            emit(hf, j, jnp.dot(hs[hf], w_ref[:, j * tn:(j + 1) * tn], preferred_element_type=F32))


def _qkv_proj(x2d, mods3, norm_g3, w, *, layer, mod_row0, seq, mode, cos=None, sin=None, qg=None, kg=None,
              group=0, dil=1):
    t = x2d.shape[0]
    tm = TM_QKV
    if mode == "b":
        n, col0 = 3 * D_MODEL, group
    else:
        n, col0 = w.shape[1], 0
    tn = TN_QKV_C if mode == "c" else TN_QKV
    hm = tm // QKV_HALVES
    per_seq = seq // tm
    in_specs = [
        pl.BlockSpec((tm, D_MODEL), lambda i: (i, 0)),
        pl.BlockSpec((1, 1, 6 * D_MODEL), lambda i: (layer * MOD_ROWS + mod_row0 + i // per_seq, 0, 0)),
        pl.BlockSpec((1, 1, D_MODEL), lambda i: (2 * layer, 0, 0)),
        pl.BlockSpec((D_MODEL, n), lambda i: (0, col0), pipeline_mode=pl.Buffered(1)),
    ]
    args = [x2d, mods3, norm_g3, w]
    if mode in ("b", "c"):
        in_specs += [pl.BlockSpec((tm, LANES), lambda i: (i % per_seq, 0))] * 2
        args += [cos, sin]
    if mode == "c":
        in_specs += [pl.BlockSpec((1, C_HEAD_DIM), lambda i: (0, 0))] * 2
        args += [qg, kg]
    scratch = []
    if mode == "b":
        assert hm % (dil * 16) == 0
        batch = t // seq
        out_spec = pl.BlockSpec((1, dil, tm // dil, n), lambda i: (i // per_seq, 0, i % per_seq, 0))
        out_shape = jax.ShapeDtypeStruct((batch, dil, seq // dil, n), BF16)
        if dil > 1:
            scratch = [pltpu.VMEM((tn // LANES, hm, LANES), F32) for _ in range(QKV_HALVES)]
    elif mode == "c":
        assert tm == C_TK
        n_qk = (C_Q_HEADS + C_KV_HEADS) * C_HEAD_DIM
        kv_w = C_KV_HEADS * C_HEAD_DIM
        out_spec = [pl.BlockSpec((tm, n_qk), lambda i: (i, 0)),
                    pl.BlockSpec((None, None, kv_w, tm), lambda i: (i // per_seq, i % per_seq, 0, 0))]
        out_shape = [jax.ShapeDtypeStruct((t, n_qk), BF16),
                     jax.ShapeDtypeStruct((t // seq, per_seq, kv_w, tm), BF16)]
    else:
        out_spec = pl.BlockSpec((tm, n), lambda i: (i, 0))
        out_shape = jax.ShapeDtypeStruct((t, n), BF16)
    return pl.pallas_call(
        functools.partial(_qkv_kernel, mode=mode, dil=dil, halves=QKV_HALVES, tn=tn),
        grid=(t // tm,),
        in_specs=in_specs,
        out_specs=out_spec,
        out_shape=out_shape,
        scratch_shapes=scratch,
        compiler_params=_cparams(("parallel",)),
        name="norm_qkv_" + mode,
    )(*args)


def _pair_masks():
    lane = lax.broadcasted_iota(jnp.int32, (1, LANES), 1)
    first = lane < HEAD_DIM_AB
    return first, jnp.logical_not(first)


def _dot_nt(a, b):
    return lax.dot_general(a, b, (((1,), (1,)), ((), ())), preferred_element_type=F32)


def _pipelined_heads(scores, finish):
    nxt = scores(0)
    for head in range(HEADS_AB):
        cur = nxt
        if head + 1 < HEADS_AB:
            nxt = scores(head + 1)
        finish(head, cur)


def _attn_a_kernel(q_ref, kp_ref, kc_ref, kn_ref, vp_ref, vc_ref, vn_ref, bias_ref, o_ref):
    masks = _pair_masks()
    k_refs = (kp_ref, kc_ref, kn_ref)
    v_refs = (vp_ref, vc_ref, vn_ref)
    outs = {}

    def scores(head):
        sl = slice((head // 2) * LANES, (head // 2 + 1) * LANES)
        qp = q_ref[:, sl]
        qm = jnp.where(masks[head % 2], qp, jnp.zeros_like(qp))
        return _dot_nt(qm, jnp.concatenate([r[:, sl] for r in k_refs], axis=0)) + bias_ref[0, head]

    def finish(head, s):
        sl = slice((head // 2) * LANES, (head // 2 + 1) * LANES)
        m = s.max(axis=-1, keepdims=True)
        pr = jnp.exp(s - m)
        l = pr.sum(axis=-1, keepdims=True)
        vcat = jnp.concatenate([r[:, sl] for r in v_refs], axis=0)
        outs[head] = jnp.dot(pr.astype(BF16), vcat, preferred_element_type=F32) / l
        if head % 2 == 1:
            o_ref[:, sl] = jnp.where(masks[0], outs.pop(head - 1), outs.pop(head)).astype(BF16)

    _pipelined_heads(scores, finish)


def _attn_a_bias(rpb):
    rows_per = A_TQ // GRID_W
    n_dr, n_dc = 2 * A_WIN_ROWS - 1, 2 * A_WIN_COLS - 1
    ql, kl = np.arange(rows_per)[:, None], np.arange(3 * rows_per)[None, :]
    c, kc = np.arange(GRID_W)[:, None], np.arange(GRID_W)[None, :]
    dr = kl - rows_per - ql + A_WIN_ROWS - 1
    dc = kc - c + A_WIN_COLS - 1
    onehot_r = (dr[..., None] == np.arange(n_dr)).astype(np.float32)
    onehot_c = (dc[..., None] == np.arange(n_dc)).astype(np.float32)
    table = jnp.einsum("hde,qkd,cje->hqckj", rpb, onehot_r, onehot_c, precision=lax.Precision.HIGHEST)
    cs = np.clip(c - A_WIN_COLS // 2, 0, GRID_W - A_WIN_COLS)
    col_ok = (kc >= cs) & (kc < cs + A_WIN_COLS)
    rows = 8 * rows_per
    out = []
    for i in (0, 1, rows // rows_per - 1):
        r = rows_per * i + ql
        rs = np.clip(r - A_WIN_ROWS // 2, 0, rows - A_WIN_ROWS)
        kr = rows_per * (i - 1) + kl
        row_ok = (kr >= rs) & (kr < rs + A_WIN_ROWS)
        ok = row_ok[:, None, :, None] & col_ok[None, :, None, :]
        out.append(jnp.where(jnp.asarray(ok)[None], table, MASK_NEG).reshape(HEADS_AB, A_TQ, 3 * A_TQ))
    return jnp.stack(out)


def _attn_a(qkv, bias, *, batch, seq):
    tq = A_TQ
    nb = seq // tq
    assert nb >= 3
    q_spec = pl.BlockSpec((tq, D_MODEL), lambda b, i: (b * nb + i, 0))
    def kv_spec(col, off):
        return pl.BlockSpec((tq, D_MODEL), lambda b, i: (b * nb + jnp.clip(i + off, 0, nb - 1), col))
    bias_spec = pl.BlockSpec((1, HEADS_AB, tq, 3 * tq),
                             lambda b, i: (jnp.where(i == 0, 0, jnp.where(i == nb - 1, 2, 1)), 0, 0, 0))
    return pl.pallas_call(
        _attn_a_kernel,
        grid=(batch, nb),
        in_specs=[q_spec, kv_spec(1, -1), kv_spec(1, 0), kv_spec(1, 1),
                  kv_spec(2, -1), kv_spec(2, 0), kv_spec(2, 1), bias_spec],
        out_specs=pl.BlockSpec((tq, D_MODEL), lambda b, i: (b * nb + i, 0)),
        out_shape=jax.ShapeDtypeStruct((batch * seq, D_MODEL), BF16),
        compiler_params=_cparams(("parallel", "arbitrary")),
        name="attn_neighbourhood",
    )(qkv, qkv, qkv, qkv, qkv, qkv, qkv, bias)


def _attn_b_kernel(q_ref, kp_ref, kc_ref, kn_ref, vp_ref, vc_ref, vn_ref, mask_ref, o_ref, lse_ref):
    tq = q_ref.shape[0]
    masks = _pair_masks()
    lane = lax.broadcasted_iota(jnp.int32, (1, LANES), 1)
    outs = {}
    lse_tile = [jnp.zeros((tq, LANES), F32)]

    def window(prev_ref, cur_ref, next_ref, sl):
        return jnp.concatenate([prev_ref[tq - B_SIDE:, sl], cur_ref[:, sl], next_ref[:B_SIDE, sl]], axis=0)

    def scores(head):
        sl = slice((head // 2) * LANES, (head // 2 + 1) * LANES)
        qp = q_ref[:, sl]
        qm = jnp.where(masks[head % 2], qp, jnp.zeros_like(qp))
        return _dot_nt(qm, window(kp_ref, kc_ref, kn_ref, sl)) + mask_ref[0]

    def finish(head, s):
        sl = slice((head // 2) * LANES, (head // 2 + 1) * LANES)
        m = s.max(axis=-1, keepdims=True)
        pr = jnp.exp(s - m)
        l = pr.sum(axis=-1, keepdims=True)
        outs[head] = jnp.dot(pr.astype(BF16), window(vp_ref, vc_ref, vn_ref, sl), preferred_element_type=F32) / l
        lse_tile[0] = jnp.where(lane == head, m + jnp.log(l), lse_tile[0])
        if head % 2 == 1:
            o_ref[:, sl] = jnp.where(masks[0], outs.pop(head - 1), outs.pop(head)).astype(BF16)

    _pipelined_heads(scores, finish)
    lse_ref[...] = lse_tile[0]


def _attn_b_masks():
    qq = np.arange(B_TQ)[:, None]
    jj = np.arange(B_TQ + 2 * B_SIDE)[None, :]
    band = (jj - qq >= 0) & (jj - qq <= 2 * B_SIDE)
    out = []
    for ty in range(4):
        ok = band
        if ty & 1:
            ok = ok & (jj >= B_SIDE)
        if ty & 2:
            ok = ok & (jj < B_SIDE + B_TQ)
        out.append(np.where(ok, 0.0, MASK_NEG))
    return np.stack(out).astype(np.float32)


def _attn_b_group(qkv, masks, *, group):
    batch, dil, sub, _ = qkv.shape
    tq = B_TQ
    nb = sub // tq
    assert sub % tq == 0
    q_spec = pl.BlockSpec((None, None, tq, D_MODEL), lambda b, r, i: (b, r, i, 0))
    def kv_spec(which, off):
        return pl.BlockSpec((None, None, tq, D_MODEL),
                            lambda b, r, i: (b, r, jnp.clip(i + off, 0, nb - 1), which))
    mask_spec = pl.BlockSpec((1, tq, tq + 2 * B_SIDE),
                             lambda b, r, i: ((i == 0).astype(jnp.int32) + 2 * (i == nb - 1).astype(jnp.int32), 0, 0))
    return pl.pallas_call(
        _attn_b_kernel,
        grid=(batch, dil, nb),
        in_specs=[q_spec, kv_spec(1, -1), kv_spec(1, 0), kv_spec(1, 1),
                  kv_spec(2, -1), kv_spec(2, 0), kv_spec(2, 1), mask_spec],
        out_specs=[pl.BlockSpec((None, None, tq, D_MODEL), lambda b, r, i: (b, r, i, 0)),
                   pl.BlockSpec((None, None, tq, LANES), lambda b, r, i: (b, r, i, 0))],
        out_shape=[jax.ShapeDtypeStruct((batch, dil, sub, D_MODEL), BF16),
                   jax.ShapeDtypeStruct((batch, dil, sub, LANES), F32)],
        compiler_params=_cparams(("parallel", "parallel", "arbitrary")),
        name="attn_dilated_g%d" % group,
    )(qkv, qkv, qkv, qkv, qkv, qkv, qkv, masks)


def _attn_c_kernel(q_ref, k_ref, vt_ref, o_ref, s_ref, m_ref, l_ref, acc_ref, *, tk):
    nk = k_ref.shape[0] // tk
    m_ref[...] = jnp.full(m_ref.shape, MASK_NEG, F32)
    l_ref[...] = jnp.zeros(l_ref.shape, F32)
    acc_ref[...] = jnp.zeros(acc_ref.shape, F32)

    def scores(h, c):
        start = pl.multiple_of(c * tk, tk)
        return _dot_nt(k_ref[pl.ds(start, tk), :], q_ref[:, h * C_HEAD_DIM:(h + 1) * C_HEAD_DIM])

    s_ref[0] = scores(0, 0)

    def chunk(c, carry):
        vt = vt_ref[c]
        for h in range(C_GROUP):
            if h + 1 < C_GROUP:
                s_ref[(h + 1) % 2] = scores(h + 1, c)
            else:
                s_ref[(h + 1) % 2] = scores(0, jnp.minimum(c + 1, nk - 1))
            s = s_ref[h % 2]
            m_prev = m_ref[h]
            m_new = jnp.maximum(m_prev, s.max(axis=0, keepdims=True))
            alpha = jnp.exp2(m_prev - m_new)
            pr = jnp.exp2(s - m_new)
            l_ref[h] = alpha * l_ref[h] + pr.sum(axis=0, keepdims=True)
            acc_ref[h] = alpha * acc_ref[h] + jnp.dot(vt, pr.astype(BF16), preferred_element_type=F32)
            m_ref[h] = m_new
        return carry

    lax.fori_loop(0, nk, chunk, 0)
    for h in range(C_GROUP):
        o_ref[:, h * C_HEAD_DIM:(h + 1) * C_HEAD_DIM] = (acc_ref[h] / l_ref[h]).T.astype(BF16)


def _attn_c(qk, vt, *, batch, seq):
    tq, tk = C_TQ, C_TK
    nq = seq // tq
    assert seq % tk == 0 and C_GROUP % 2 == 0
    qw = C_GROUP * C_HEAD_DIM
    return pl.pallas_call(
        functools.partial(_attn_c_kernel, tk=tk),
        grid=(batch, C_KV_HEADS, nq),
        in_specs=[
            pl.BlockSpec((tq, qw), lambda b, g, i: (b * nq + i, g)),
            pl.BlockSpec((seq, C_HEAD_DIM), lambda b, g, i: (b, C_Q_HEADS + g)),
            pl.BlockSpec((None, seq // tk, C_HEAD_DIM, tk), lambda b, g, i: (b, 0, g, 0)),
        ],
        out_specs=pl.BlockSpec((tq, qw), lambda b, g, i: (b * nq + i, g)),
        out_shape=jax.ShapeDtypeStruct((batch * seq, C_Q_HEADS * C_HEAD_DIM), BF16),
        scratch_shapes=[pltpu.VMEM((2, tk, tq), F32),
                        pltpu.VMEM((C_GROUP, 1, tq), F32), pltpu.VMEM((C_GROUP, 1, tq), F32),
                        pltpu.VMEM((C_GROUP, C_HEAD_DIM, tq), F32)],
        compiler_params=_cparams(("parallel", "parallel", "arbitrary")),
        name="attn_gqa_flash",
    )(qk, qk, vt)


def _post_kernel(*refs, merge, final, halves, tf):
    refs = list(refs)
    x_ref = refs.pop(0)
    if merge:
        o_refs = [refs.pop(0) for _ in range(3)]
        lse_refs = [refs.pop(0) for _ in range(3)]
        expand_ref = refs.pop(0)
    else:
        o_ref_in = refs.pop(0)
    mod_ref, g_ref, wo_ref, w1_ref, w2_ref = [refs.pop(0) for _ in range(5)]
    fg_ref = refs.pop(0) if final else None
    out_ref = refs.pop(0)
    stage_refs = refs
    mod = mod_ref[0]
    hm = x_ref.shape[0] // halves

    def merged_groups(hf):
        o_tok, lse = [], []
        for g, (o_g, lse_g) in enumerate(zip(o_refs, lse_refs)):
            dil = o_g.shape[0]
            rows = hm // dil
            src = slice(hf * rows, (hf + 1) * rows)
            if dil == 1:
                o_tok.append(o_g[0, src, :].astype(F32))
                lse.append(lse_g[0, src, :])
                continue
            o_stage, lse_stage = stage_refs[4 * hf + 2 * (g - 1)], stage_refs[4 * hf + 2 * (g - 1) + 1]
            chunks = o_g.shape[2] // LANES
            for r in range(dil):
                lse_stage[pl.ds(r, rows, stride=dil), :] = lse_g[r, src, :]
                for c in range(chunks):
                    o_stage[c, pl.ds(r, rows, stride=dil), :] = o_g[r, src, c * LANES:(c + 1) * LANES].astype(F32)
            o_tok.append(jnp.concatenate([o_stage[c] for c in range(chunks)], axis=1))
            lse.append(lse_stage[...])
        top = jnp.maximum(jnp.maximum(lse[0], lse[1]), lse[2])
        ex = [jnp.exp(v - top) for v in lse]
        den = ex[0] + ex[1] + ex[2]
        mixed = None
        for g in range(3):
            wgt = ex[g] / den
            hi = wgt.astype(BF16)
            lo = (wgt - hi.astype(F32)).astype(BF16)
            wide = (jnp.dot(hi, expand_ref[...], preferred_element_type=F32)
                    + jnp.dot(lo, expand_ref[...], preferred_element_type=F32))
            mixed = wide * o_tok[g] if mixed is None else mixed + wide * o_tok[g]
        return mixed.astype(BF16)

    def pre(hf):
        rows = slice(hf * hm, (hf + 1) * hm)
        o = merged_groups(hf) if merge else o_ref_in[rows, :]
        mix = jnp.dot(o, wo_ref[...], preferred_element_type=F32)
        x1 = x_ref[rows, :] + mod[:, 2 * D_MODEL:3 * D_MODEL] * mix
        h2 = _norm_mod(x1, g_ref[0], mod[:, 4 * D_MODEL:5 * D_MODEL], mod[:, 3 * D_MODEL:4 * D_MODEL])
        return x1, h2.astype(BF16)

    def mlp(h2):
        acc = None
        for c in range(D_FF // tf):
            a = jnp.dot(h2, w1_ref[:, c * tf:(c + 1) * tf], preferred_element_type=F32)
            a = jnp.square(jnp.maximum(a, 0.0)).astype(BF16)
            d = jnp.dot(a, w2_ref[c * tf:(c + 1) * tf, :], preferred_element_type=F32)
            acc = d if acc is None else acc + d
        return acc

    pres = [pre(hf) for hf in range(halves)]
    for hf, (x1, h2) in enumerate(pres):
        x2 = x1 + mod[:, 5 * D_MODEL:6 * D_MODEL] * mlp(h2)
        if final:
            ms = jnp.mean(x2 * x2, axis=-1, keepdims=True)
            x2 = x2 * lax.rsqrt(ms + EPS) * fg_ref[...]
        out_ref[hf * hm:(hf + 1) * hm, :] = x2


def _head_expand_matrix():
    e = np.zeros((LANES, D_MODEL), np.float32)
    for h in range(HEADS_AB):
        e[h, h * HEAD_DIM_AB:(h + 1) * HEAD_DIM_AB] = 1.0
    return e


def _post(x2d, attn, mods3, norm_g3, wo, w1, w2, final_g, *, layer, mod_row0, seq, merge, final):
    t = x2d.shape[0]
    tm = TM_POST_MERGE if merge else TM_POST
    hm = tm // POST_HALVES
    per_seq = seq // tm
    row = lambda i: (i, 0)
    const = lambda i: (0, 0)
    resident = pl.Buffered(1)
    in_specs = [pl.BlockSpec((tm, D_MODEL), row)]
    args = [x2d]
    scratch = []
    if merge:
        outs, lses = attn
        def res_spec(arr):
            dil, width = arr.shape[1], arr.shape[3]
            assert hm % (dil * 16) == 0
            return pl.BlockSpec((None, dil, tm // dil, width), lambda i: (i // per_seq, 0, i % per_seq, 0))
        in_specs += [res_spec(a) for a in outs] + [res_spec(a) for a in lses]
        in_specs += [pl.BlockSpec((LANES, D_MODEL), const, pipeline_mode=resident)]
        args += list(outs) + list(lses) + [jnp.asarray(_head_expand_matrix(), BF16)]
        for _ in range(POST_HALVES * (len(outs) - 1)):
            scratch += [pltpu.VMEM((D_MODEL // LANES, hm, LANES), F32), pltpu.VMEM((hm, LANES), F32)]
    else:
        in_specs += [pl.BlockSpec((tm, D_MODEL), row)]
        args += [attn]
    in_specs += [
        pl.BlockSpec((1, 1, 6 * D_MODEL), lambda i: (layer * MOD_ROWS + mod_row0 + i // per_seq, 0, 0)),
        pl.BlockSpec((1, 1, D_MODEL), lambda i: (2 * layer + 1, 0, 0)),
        pl.BlockSpec((D_MODEL, D_MODEL), const, pipeline_mode=resident),
        pl.BlockSpec((D_MODEL, D_FF), const, pipeline_mode=resident),
        pl.BlockSpec((D_FF, D_MODEL), const, pipeline_mode=resident),
    ]
    args += [mods3, norm_g3, wo, w1, w2]
    if final:
        in_specs += [pl.BlockSpec((1, D_MODEL), const)]
        args += [final_g.reshape(1, D_MODEL)]
    return pl.pallas_call(
        functools.partial(_post_kernel, merge=merge, final=final, halves=POST_HALVES, tf=TF_POST),
        grid=(t // tm,),
        in_specs=in_specs,
        out_specs=pl.BlockSpec((tm, D_MODEL), row),
        out_shape=jax.ShapeDtypeStruct((t, D_MODEL), F32),
        scratch_shapes=scratch,
        compiler_params=_cparams(("parallel",)),
        name="wo_mlp_merge" if merge else "wo_mlp",
    )(*args)


def _trunk(x, mods3, mod_row0, norm_g3, final_g, wts, a_bias, b_masks):
    batch, seq, _ = x.shape
    x2d = x.reshape(batch * seq, D_MODEL)
    t = jnp.arange(seq)
    tabs_b = _rope_tables(t, t)
    tabs_c = _rope_tables(t // GRID_W, t % GRID_W)
    for layer in range(DEPTH):
        kind, j = layer % N_MIXERS, layer // N_MIXERS
        common = dict(layer=layer, mod_row0=mod_row0, seq=seq)
        if kind == 0:
            qkv = _qkv_proj(x2d, mods3, norm_g3, wts["a_qkv"][j], mode="a", **common)
            attn = _attn_a(qkv, a_bias[j], batch=batch, seq=seq)
            wo = wts["a_o"][j]
        elif kind == 1:
            groups = []
            for g, (win, dil) in enumerate(B_PAIRS):
                assert win == 2 * B_SIDE * dil
                qkv = _qkv_proj(x2d, mods3, norm_g3, wts["b_qkv"][j], mode="b", cos=tabs_b[0], sin=tabs_b[1],
                                group=g, dil=dil, **common)
                groups.append(_attn_b_group(qkv, b_masks, group=g))
            attn = ([o for o, _ in groups], [l for _, l in groups])
            wo = wts["b_o"][j]
        else:
            qk, vt = _qkv_proj(x2d, mods3, norm_g3, wts["c_qkv"][j], mode="c", cos=tabs_c[0], sin=tabs_c[1],
                               qg=wts["c_qg"][j], kg=wts["c_kg"][j], **common)
            attn = _attn_c(qk, vt, batch=batch, seq=seq)
            wo = wts["c_o"][j]
        x2d = _post(x2d, attn, mods3, norm_g3, wo, wts["w1"][layer], wts["w2"][layer], final_g,
                    merge=(kind == 1), final=(layer == DEPTH - 1), **common)
    return x2d.reshape(batch, seq, D_MODEL)


def kernel(x_prompt, x_sample, c_prompt, c_sample, w_mod, b_mod, norm_g, final_g, a_w_qkv, a_rpb, a_w_o,
           b_w_qkv, b_w_o, c_w_qkv, c_q_g, c_k_g, c_w_o, mlp_w1, mlp_w2):
    nb_p, nb_s = c_prompt.shape[0], c_sample.shape[0]
    assert nb_p + nb_s <= MOD_ROWS
    c_all = jnp.concatenate([c_prompt, c_sample, jnp.zeros((MOD_ROWS - nb_p - nb_s, D_MODEL), F32)], axis=0)
    mods3 = _modulation(c_all, w_mod, b_mod).reshape(DEPTH * MOD_ROWS, 1, 6 * D_MODEL)
    norm_g3 = norm_g.reshape(DEPTH * 2, 1, D_MODEL)

    scale_ab = HEAD_DIM_AB ** -0.5
    a_qkv = a_w_qkv.at[:, :, :D_MODEL].multiply(scale_ab).astype(BF16)
    b_qkv = b_w_qkv.reshape(-1, D_MODEL, len(B_PAIRS), 3, D_MODEL).at[:, :, :, 0].multiply(scale_ab)
    b_qkv = b_qkv.reshape(b_w_qkv.shape).astype(BF16)
    wts = {
        "a_qkv": a_qkv, "a_o": a_w_o.astype(BF16),
        "b_qkv": b_qkv, "b_o": b_w_o.astype(BF16),
        "c_qkv": c_w_qkv.astype(BF16), "c_o": c_w_o.astype(BF16),
        "c_qg": (c_q_g * float(C_HEAD_DIM ** -0.5 * np.log2(np.e))).reshape(-1, 1, C_HEAD_DIM),
        "c_kg": c_k_g.reshape(-1, 1, C_HEAD_DIM),
        "w1": mlp_w1.astype(BF16), "w2": mlp_w2.astype(BF16),
    }
    a_bias = [_attn_a_bias(a_rpb[j]) for j in range(a_rpb.shape[0])]
    b_masks = jnp.asarray(_attn_b_masks())

    y_prompt = _trunk(x_prompt, mods3, 0, norm_g3, final_g, wts, a_bias, b_masks)
    y_sample = _trunk(x_sample, mods3, nb_p, norm_g3, final_g, wts, a_bias, b_masks)
    return (y_prompt, y_sample)
```

```python
import functools

import numpy as np
import jax
import jax.numpy as jnp
from jax import lax
from jax.experimental import pallas as pl
from jax.experimental.pallas import tpu as pltpu

F32 = jnp.float32
BF16 = jnp.bfloat16

D_MODEL = 1024
DEPTH = 4
GRID_W = 64
D_FF = 4 * D_MODEL
EPS = 1e-6
ROPE_THETA = 10000.0
N_MIXERS = 3
HEAD_DIM_AB = 64
HEADS_AB = D_MODEL // HEAD_DIM_AB
A_WIN_ROWS = 8
A_WIN_COLS = 16
B_PAIRS = ((128, 1), (512, 4), (2048, 16))
B_SIDE = 64
C_HEAD_DIM = 128
C_Q_HEADS = 8
C_KV_HEADS = 2
C_GROUP = C_Q_HEADS // C_KV_HEADS
C_QKV_WIDTH = (C_Q_HEADS + 2 * C_KV_HEADS) * C_HEAD_DIM
ROPE_HALF = 32
LANES = 128
MOD_ROWS = 8
MASK_NEG = -1e30
LOG2E = float(np.log2(np.e))
VMEM_LIMIT = 56 * 1024 * 1024

TM_QKV = 1024
TN_QKV = 1024
TN_QKV_C = 512
QKV_HALVES = 2
TM_POST = 1024
TM_POST_MERGE = 512
POST_HALVES = 2
TF_POST = 1024
A_TQ = 4 * GRID_W
B_TQ = 256
B_HALF = 128
C_TQ = 512
C_TK = 1024


def _cparams(sem):
    return pltpu.CompilerParams(dimension_semantics=sem, vmem_limit_bytes=VMEM_LIMIT)


def _mod_kernel(c_ref, w_ref, b_ref, o_ref):
    c = c_ref[...]
    act = (c / (1.0 + jnp.exp(-c))).astype(BF16)
    o_ref[0] = jnp.dot(act, w_ref[0].astype(BF16), preferred_element_type=F32) + b_ref[0]


def _modulation(c_all, w_mod, b_mod):
    tn = 1536
    n = 6 * D_MODEL
    return pl.pallas_call(
        _mod_kernel,
        grid=(DEPTH, n // tn),
        in_specs=[
            pl.BlockSpec((MOD_ROWS, D_MODEL), lambda l, j: (0, 0)),
            pl.BlockSpec((1, D_MODEL, tn), lambda l, j: (l, 0, j)),
            pl.BlockSpec((1, 1, tn), lambda l, j: (l, 0, j)),
        ],
        out_specs=pl.BlockSpec((1, MOD_ROWS, tn), lambda l, j: (l, 0, j)),
        out_shape=jax.ShapeDtypeStruct((DEPTH, MOD_ROWS, n), F32),
        compiler_params=_cparams(("parallel", "parallel")),
        name="adaln_modulation",
    )(c_all, w_mod, b_mod.reshape(DEPTH, 1, n))


def _norm_mod(x, gain, scale, shift):
    ms = jnp.mean(x * x, axis=-1, keepdims=True)
    return (x * lax.rsqrt(ms + EPS) * gain) * (1.0 + scale) + shift


def _rope128(x, cos, sin_signed, low_half):
    up = pltpu.roll(x, LANES - ROPE_HALF, 1)
    down = pltpu.roll(x, ROPE_HALF, 1)
    return x * cos + jnp.where(low_half, up, down) * sin_signed


def _low_half_mask():
    lane = lax.broadcasted_iota(jnp.int32, (1, LANES), 1)
    return (lane % (2 * ROPE_HALF)) < ROPE_HALF


def _rope_tables(pos_a, pos_b):
    inv = ROPE_THETA ** (-jnp.arange(ROPE_HALF, dtype=F32) / ROPE_HALF)
    def one(pos):
        ang = pos.astype(F32)[:, None] * inv[None, :]
        c, s = jnp.cos(ang), jnp.sin(ang)
        return jnp.concatenate([c, c], axis=1), jnp.concatenate([-s, s], axis=1)
    ca, sa = one(pos_a)
    cb, sb = one(pos_b)
    return jnp.concatenate([ca, cb], axis=1), jnp.concatenate([sa, sb], axis=1)


def _qkv_kernel(*refs, mode, dil, halves, tn):
    stage_refs = ()
    if mode == "a":
        x_ref, mod_ref, g_ref, w_ref, o_ref = refs
    elif mode == "b":
        x_ref, mod_ref, g_ref, w_ref, cos_ref, sin_ref, o_ref = refs[:7]
        stage_refs = refs[7:]
    else:
        x_ref, mod_ref, g_ref, w_ref, cos_ref, sin_ref, qg_ref, kg_ref, o_ref, vt_ref = refs
    mod = mod_ref[0]
    hm = x_ref.shape[0] // halves
    n = w_ref.shape[1]
    low = _low_half_mask()

    def normed(hf):
        rows = slice(hf * hm, (hf + 1) * hm)
        return _norm_mod(x_ref[rows, :], g_ref[0], mod[:, D_MODEL:2 * D_MODEL], mod[:, 0:D_MODEL]).astype(BF16)

    def emit(hf, j, acc):
        rows = slice(hf * hm, (hf + 1) * hm)
        if mode == "a":
            o_ref[rows, j * tn:(j + 1) * tn] = acc.astype(BF16)
            return
        cos, sin = cos_ref[rows, :], sin_ref[rows, :]
        for c in range(tn // LANES):
            col = j * tn + c * LANES
            val = acc[:, c * LANES:(c + 1) * LANES]
            if mode == "c":
                head = col // C_HEAD_DIM
                if head < C_Q_HEADS + C_KV_HEADS:
                    gain = qg_ref[...] if head < C_Q_HEADS else kg_ref[...]
                    ms = jnp.mean(val * val, axis=-1, keepdims=True)
                    val = _rope128(val * lax.rsqrt(ms + EPS) * gain, cos, sin, low)
                    o_ref[rows, col:col + LANES] = val.astype(BF16)
                else:
                    kv = head - C_Q_HEADS - C_KV_HEADS
                    vt_ref[kv * C_HEAD_DIM:(kv + 1) * C_HEAD_DIM, rows] = val.T.astype(BF16)
                continue
            if col < 2 * D_MODEL:
                val = _rope128(val, cos, sin, low)
            sub = hm // dil
            dst = slice(hf * sub, (hf + 1) * sub)
            if dil == 1:
                o_ref[0, 0, dst, col:col + LANES] = val.astype(BF16)
                continue
            stage = stage_refs[hf]
            stage[c] = val
            for r in range(dil):
                o_ref[0, r, dst, col:col + LANES] = stage[c, pl.ds(r, sub, stride=dil), :].astype(BF16)

    hs = [normed(hf) for hf in range(halves)]
    for hf in range(halves):
        for j in range(n // tn):
            emit(hf, j, jnp.dot(hs[hf], w_ref[:, j * tn:(j + 1) * tn], preferred_element_type=F32))


def _qkv_proj(x2d, mods3, norm_g3, w, *, layer, mod_row0, seq, mode, cos=None, sin=None, qg=None, kg=None,
              group=0, dil=1):
    t = x2d.shape[0]
    tm = TM_QKV
    if mode == "b":
        n, col0 = 3 * D_MODEL, group
    else:
        n, col0 = w.shape[1], 0
    tn = TN_QKV_C if mode == "c" else TN_QKV
    hm = tm // QKV_HALVES
    per_seq = seq // tm
    in_specs = [
        pl.BlockSpec((tm, D_MODEL), lambda i: (i, 0)),
        pl.BlockSpec((1, 1, 6 * D_MODEL), lambda i: (layer * MOD_ROWS + mod_row0 + i // per_seq, 0, 0)),
        pl.BlockSpec((1, 1, D_MODEL), lambda i: (2 * layer, 0, 0)),
        pl.BlockSpec((D_MODEL, n), lambda i: (0, col0), pipeline_mode=pl.Buffered(1)),
    ]
    args = [x2d, mods3, norm_g3, w]
    if mode in ("b", "c"):
        in_specs += [pl.BlockSpec((tm, LANES), lambda i: (i % per_seq, 0))] * 2
        args += [cos, sin]
    if mode == "c":
        in_specs += [pl.BlockSpec((1, C_HEAD_DIM), lambda i: (0, 0))] * 2
        args += [qg, kg]
    scratch = []
    if mode == "b":
        assert hm % (dil * 16) == 0
        batch = t // seq
        out_spec = pl.BlockSpec((1, dil, tm // dil, n), lambda i: (i // per_seq, 0, i % per_seq, 0))
        out_shape = jax.ShapeDtypeStruct((batch, dil, seq // dil, n), BF16)
        if dil > 1:
            scratch = [pltpu.VMEM((tn // LANES, hm, LANES), F32) for _ in range(QKV_HALVES)]
    elif mode == "c":
        assert tm == C_TK
        n_qk = (C_Q_HEADS + C_KV_HEADS) * C_HEAD_DIM
        kv_w = C_KV_HEADS * C_HEAD_DIM
        out_spec = [pl.BlockSpec((tm, n_qk), lambda i: (i, 0)),
                    pl.BlockSpec((None, None, kv_w, tm), lambda i: (i // per_seq, i % per_seq, 0, 0))]
        out_shape = [jax.ShapeDtypeStruct((t, n_qk), BF16),
                     jax.ShapeDtypeStruct((t // seq, per_seq, kv_w, tm), BF16)]
    else:
        out_spec = pl.BlockSpec((tm, n), lambda i: (i, 0))
        out_shape = jax.ShapeDtypeStruct((t, n), BF16)
    return pl.pallas_call(
        functools.partial(_qkv_kernel, mode=mode, dil=dil, halves=QKV_HALVES, tn=tn),
        grid=(t // tm,),
        in_specs=in_specs,
        out_specs=out_spec,
        out_shape=out_shape,
        scratch_shapes=scratch,
        compiler_params=_cparams(("parallel",)),
        name="norm_qkv_" + mode,
    )(*args)


def _pair_masks():
    lane = lax.broadcasted_iota(jnp.int32, (1, LANES), 1)
    first = lane < HEAD_DIM_AB
    return first, jnp.logical_not(first)


def _dot_nt(a, b):
    return lax.dot_general(a, b, (((1,), (1,)), ((), ())), preferred_element_type=F32)


def _pipelined(units, scores, finish):
    nxt = scores(units[0])
    for i, unit in enumerate(units):
        cur = nxt
        if i + 1 < len(units):
            nxt = scores(units[i + 1])
        finish(unit, cur)


def _attn_a_kernel(q_ref, kp_ref, kc_ref, kn_ref, vp_ref, vc_ref, vn_ref, bias_ref, o_ref):
    masks = _pair_masks()
    k_refs = (kp_ref, kc_ref, kn_ref)
    v_refs = (vp_ref, vc_ref, vn_ref)
    outs = {}

    def scores(head):
        sl = slice((head // 2) * LANES, (head // 2 + 1) * LANES)
        qp = q_ref[:, sl]
        qm = jnp.where(masks[head % 2], qp, jnp.zeros_like(qp))
        return _dot_nt(qm, jnp.concatenate([r[:, sl] for r in k_refs], axis=0)) + bias_ref[0, head]

    def finish(head, s):
        sl = slice((head // 2) * LANES, (head // 2 + 1) * LANES)
        m = s.max(axis=-1, keepdims=True)
        pr = jnp.exp2(s - m)
        l = pr.sum(axis=-1, keepdims=True)
        vcat = jnp.concatenate([r[:, sl] for r in v_refs], axis=0)
        outs[head] = jnp.dot(pr.astype(BF16), vcat, preferred_element_type=F32) / l
        if head % 2 == 1:
            o_ref[:, sl] = jnp.where(masks[0], outs.pop(head - 1), outs.pop(head)).astype(BF16)

    _pipelined(list(range(HEADS_AB)), scores, finish)


def _attn_a_bias(rpb):
    rows_per = A_TQ // GRID_W
    n_dr, n_dc = 2 * A_WIN_ROWS - 1, 2 * A_WIN_COLS - 1
    ql, kl = np.arange(rows_per)[:, None], np.arange(3 * rows_per)[None, :]
    c, kc = np.arange(GRID_W)[:, None], np.arange(GRID_W)[None, :]
    dr = kl - rows_per - ql + A_WIN_ROWS - 1
    dc = kc - c + A_WIN_COLS - 1
    onehot_r = (dr[..., None] == np.arange(n_dr)).astype(np.float32)
    onehot_c = (dc[..., None] == np.arange(n_dc)).astype(np.float32)
    table = jnp.einsum("hde,qkd,cje->hqckj", rpb * LOG2E, onehot_r, onehot_c, precision=lax.Precision.HIGHEST)
    cs = np.clip(c - A_WIN_COLS // 2, 0, GRID_W - A_WIN_COLS)
    col_ok = (kc >= cs) & (kc < cs + A_WIN_COLS)
    rows = 8 * rows_per
    out = []
    for i in (0, 1, rows // rows_per - 1):
        r = rows_per * i + ql
        rs = np.clip(r - A_WIN_ROWS // 2, 0, rows - A_WIN_ROWS)
        kr = rows_per * (i - 1) + kl
        row_ok = (kr >= rs) & (kr < rs + A_WIN_ROWS)
        ok = row_ok[:, None, :, None] & col_ok[None, :, None, :]
        out.append(jnp.where(jnp.asarray(ok)[None], table, MASK_NEG).reshape(HEADS_AB, A_TQ, 3 * A_TQ))
    return jnp.stack(out)


def _attn_a(qkv, bias, *, batch, seq):
    tq = A_TQ
    nb = seq // tq
    assert nb >= 3
    q_spec = pl.BlockSpec((tq, D_MODEL), lambda b, i: (b * nb + i, 0))
    def kv_spec(col, off):
        return pl.BlockSpec((tq, D_MODEL), lambda b, i: (b * nb + jnp.clip(i + off, 0, nb - 1), col))
    bias_spec = pl.BlockSpec((1, HEADS_AB, tq, 3 * tq),
                             lambda b, i: (jnp.where(i == 0, 0, jnp.where(i == nb - 1, 2, 1)), 0, 0, 0))
    return pl.pallas_call(
        _attn_a_kernel,
        grid=(batch, nb),
        in_specs=[q_spec, kv_spec(1, -1), kv_spec(1, 0), kv_spec(1, 1),
                  kv_spec(2, -1), kv_spec(2, 0), kv_spec(2, 1), bias_spec],
        out_specs=pl.BlockSpec((tq, D_MODEL), lambda b, i: (b * nb + i, 0)),
        out_shape=jax.ShapeDtypeStruct((batch * seq, D_MODEL), BF16),
        compiler_params=_cparams(("parallel", "arbitrary")),
        name="attn_neighbourhood",
    )(qkv, qkv, qkv, qkv, qkv, qkv, qkv, bias)


def _attn_b_kernel(q_ref, kp_ref, kc_ref, kn_ref, vp_ref, vc_ref, vn_ref, mask_ref, o_ref, lse_ref):
    tq = q_ref.shape[0]
    masks = _pair_masks()
    lane = lax.broadcasted_iota(jnp.int32, (1, LANES), 1)
    lse_tiles = [jnp.zeros((B_HALF, LANES), F32) for _ in range(tq // B_HALF)]
    units = [(p, hf) for p in range(HEADS_AB // 2) for hf in range(tq // B_HALF)]

    def window(prev_ref, cur_ref, next_ref, hf, sl):
        lo, hi = hf * B_HALF - B_SIDE, (hf + 1) * B_HALF + B_SIDE
        parts = []
        if lo < 0:
            parts.append(prev_ref[tq + lo:, sl])
        parts.append(cur_ref[max(lo, 0):min(hi, tq), sl])
        if hi > tq:
            parts.append(next_ref[:hi - tq, sl])
        return jnp.concatenate(parts, axis=0)

    def scores(unit):
        p, hf = unit
        sl = slice(p * LANES, (p + 1) * LANES)
        qp = q_ref[hf * B_HALF:(hf + 1) * B_HALF, sl]
        stacked = jnp.concatenate([jnp.where(mk, qp, jnp.zeros_like(qp)) for mk in masks], axis=0)
        return _dot_nt(stacked, window(kp_ref, kc_ref, kn_ref, hf, sl)) + mask_ref[0, hf]

    def finish(unit, s):
        p, hf = unit
        sl = slice(p * LANES, (p + 1) * LANES)
        m = s.max(axis=-1, keepdims=True)
        pr = jnp.exp2(s - m)
        l = pr.sum(axis=-1, keepdims=True)
        o = jnp.dot(pr.astype(BF16), window(vp_ref, vc_ref, vn_ref, hf, sl), preferred_element_type=F32) / l
        o_ref[hf * B_HALF:(hf + 1) * B_HALF, sl] = jnp.where(masks[0], o[:B_HALF], o[B_HALF:]).astype(BF16)
        lse = m + jnp.log2(l)
        tile = jnp.where(lane == 2 * p, lse[:B_HALF], lse_tiles[hf])
        lse_tiles[hf] = jnp.where(lane == 2 * p + 1, lse[B_HALF:], tile)

    _pipelined(units, scores, finish)
    for hf, tile in enumerate(lse_tiles):
        lse_ref[hf * B_HALF:(hf + 1) * B_HALF, :] = tile


def _attn_b_masks():
    qq = np.arange(B_HALF)[:, None]
    jj = np.arange(B_HALF + 2 * B_SIDE)[None, :]
    band = (jj - qq >= 0) & (jj - qq <= 2 * B_SIDE)
    out = []
    for ty in range(4):
        per_half = []
        for hf in range(B_TQ // B_HALF):
            pos = hf * B_HALF - B_SIDE + jj
            ok = band
            if ty & 1:
                ok = ok & (pos >= 0)
            if ty & 2:
                ok = ok & (pos < B_TQ)
            half = np.where(ok, 0.0, MASK_NEG)
            per_half.append(np.concatenate([half, half], axis=0))
        out.append(np.stack(per_half))
    return np.stack(out).astype(np.float32)


def _attn_b_group(qkv, masks, *, group):
    batch, dil, sub, _ = qkv.shape
    tq = B_TQ
    nb = sub // tq
    assert sub % tq == 0
    q_spec = pl.BlockSpec((None, None, tq, D_MODEL), lambda b, r, i: (b, r, i, 0))
    def kv_spec(which, off):
        return pl.BlockSpec((None, None, tq, D_MODEL),
                            lambda b, r, i: (b, r, jnp.clip(i + off, 0, nb - 1), which))
    mask_spec = pl.BlockSpec((1,) + masks.shape[1:],
                             lambda b, r, i: ((i == 0).astype(jnp.int32) + 2 * (i == nb - 1).astype(jnp.int32),
                                              0, 0, 0))
    return pl.pallas_call(
        _attn_b_kernel,
        grid=(batch, dil, nb),
        in_specs=[q_spec, kv_spec(1, -1), kv_spec(1, 0), kv_spec(1, 1),
                  kv_spec(2, -1), kv_spec(2, 0), kv_spec(2, 1), mask_spec],
        out_specs=[pl.BlockSpec((None, None, tq, D_MODEL), lambda b, r, i: (b, r, i, 0)),
                   pl.BlockSpec((None, None, tq, LANES), lambda b, r, i: (b, r, i, 0))],
        out_shape=[jax.ShapeDtypeStruct((batch, dil, sub, D_MODEL), BF16),
                   jax.ShapeDtypeStruct((batch, dil, sub, LANES), F32)],
        compiler_params=_cparams(("parallel", "parallel", "arbitrary")),
        name="attn_dilated_g%d" % group,
    )(qkv, qkv, qkv, qkv, qkv, qkv, qkv, masks)


def _attn_c_kernel(q_ref, k_ref, vt_ref, o_ref, s_ref, m_ref, l_ref, acc_ref, *, tk):
    nk = k_ref.shape[0] // tk
    m_ref[...] = jnp.full(m_ref.shape, MASK_NEG, F32)
    l_ref[...] = jnp.zeros(l_ref.shape, F32)
    acc_ref[...] = jnp.zeros(acc_ref.shape, F32)

    def scores(h, c):
        start = pl.multiple_of(c * tk, tk)
        return _dot_nt(k_ref[pl.ds(start, tk), :], q_ref[:, h * C_HEAD_DIM:(h + 1) * C_HEAD_DIM])

    s_ref[0] = scores(0, 0)

    def chunk(c, carry):
        vt = vt_ref[c]
        for h in range(C_GROUP):
            if h + 1 < C_GROUP:
                s_ref[(h + 1) % 2] = scores(h + 1, c)
            else:
                s_ref[(h + 1) % 2] = scores(0, jnp.minimum(c + 1, nk - 1))
            s = s_ref[h % 2]
            m_prev = m_ref[h]
            m_new = jnp.maximum(m_prev, s.max(axis=0, keepdims=True))
            alpha = jnp.exp2(m_prev - m_new)
            pr = jnp.exp2(s - m_new)
            l_ref[h] = alpha * l_ref[h] + pr.sum(axis=0, keepdims=True)
            acc_ref[h] = alpha * acc_ref[h] + jnp.dot(vt, pr.astype(BF16), preferred_element_type=F32)
            m_ref[h] = m_new
        return carry

    lax.fori_loop(0, nk, chunk, 0)
    for h in range(C_GROUP):
        o_ref[:, h * C_HEAD_DIM:(h + 1) * C_HEAD_DIM] = (acc_ref[h] / l_ref[h]).T.astype(BF16)


def _attn_c(qk, vt, *, batch, seq):
    tq, tk = C_TQ, C_TK
    nq = seq // tq
    assert seq % tk == 0 and C_GROUP % 2 == 0
    qw = C_GROUP * C_HEAD_DIM
    return pl.pallas_call(
        functools.partial(_attn_c_kernel, tk=tk),
        grid=(batch, C_KV_HEADS, nq),
        in_specs=[
            pl.BlockSpec((tq, qw), lambda b, g, i: (b * nq + i, g)),
            pl.BlockSpec((seq, C_HEAD_DIM), lambda b, g, i: (b, C_Q_HEADS + g)),
            pl.BlockSpec((None, seq // tk, C_HEAD_DIM, tk), lambda b, g, i: (b, 0, g, 0)),
        ],
        out_specs=pl.BlockSpec((tq, qw), lambda b, g, i: (b * nq + i, g)),
        out_shape=jax.ShapeDtypeStruct((batch * seq, C_Q_HEADS * C_HEAD_DIM), BF16),
        scratch_shapes=[pltpu.VMEM((2, tk, tq), F32),
                        pltpu.VMEM((C_GROUP, 1, tq), F32), pltpu.VMEM((C_GROUP, 1, tq), F32),
                        pltpu.VMEM((C_GROUP, C_HEAD_DIM, tq), F32)],
        compiler_params=_cparams(("parallel", "parallel", "arbitrary")),
        name="attn_gqa_flash",
    )(qk, qk, vt)


def _post_kernel(*refs, merge, final, halves, tf):
    refs = list(refs)
    x_ref = refs.pop(0)
    if merge:
        o_refs = [refs.pop(0) for _ in range(3)]
        lse_refs = [refs.pop(0) for _ in range(3)]
        expand_ref = refs.pop(0)
    else:
        o_ref_in = refs.pop(0)
    mod_ref, g_ref, wo_ref, w1_ref, w2_ref = [refs.pop(0) for _ in range(5)]
    fg_ref = refs.pop(0) if final else None
    out_ref = refs.pop(0)
    stage_refs = refs
    mod = mod_ref[0]
    hm = x_ref.shape[0] // halves

    def merged_groups(hf):
        o_tok, lse = [], []
        for g, (o_g, lse_g) in enumerate(zip(o_refs, lse_refs)):
            dil = o_g.shape[0]
            rows = hm // dil
            src = slice(hf * rows, (hf + 1) * rows)
            if dil == 1:
                o_tok.append(o_g[0, src, :].astype(F32))
                lse.append(lse_g[0, src, :])
                continue
            o_stage, lse_stage = stage_refs[4 * hf + 2 * (g - 1)], stage_refs[4 * hf + 2 * (g - 1) + 1]
            chunks = o_g.shape[2] // LANES
            for r in range(dil):
                lse_stage[pl.ds(r, rows, stride=dil), :] = lse_g[r, src, :]
                for c in range(chunks):
                    o_stage[c, pl.ds(r, rows, stride=dil), :] = o_g[r, src, c * LANES:(c + 1) * LANES].astype(F32)
            o_tok.append(jnp.concatenate([o_stage[c] for c in range(chunks)], axis=1))
            lse.append(lse_stage[...])
        top = jnp.maximum(jnp.maximum(lse[0], lse[1]), lse[2])
        ex = [jnp.exp2(v - top) for v in lse]
        den = ex[0] + ex[1] + ex[2]
        mixed = None
        for g in range(3):
            wgt = ex[g] / den
            hi = wgt.astype(BF16)
            lo = (wgt - hi.astype(F32)).astype(BF16)
            wide = (jnp.dot(hi, expand_ref[...], preferred_element_type=F32)
                    + jnp.dot(lo, expand_ref[...], preferred_element_type=F32))
            mixed = wide * o_tok[g] if mixed is None else mixed + wide * o_tok[g]
        return mixed.astype(BF16)

    def pre(hf):
        rows = slice(hf * hm, (hf + 1) * hm)
        o = merged_groups(hf) if merge else o_ref_in[rows, :]
        mix = jnp.dot(o, wo_ref[...], preferred_element_type=F32)
        x1 = x_ref[rows, :] + mod[:, 2 * D_MODEL:3 * D_MODEL] * mix
        h2 = _norm_mod(x1, g_ref[0], mod[:, 4 * D_MODEL:5 * D_MODEL], mod[:, 3 * D_MODEL:4 * D_MODEL])
        return x1, h2.astype(BF16)

    def mlp(h2):
        acc = None
        for c in range(D_FF // tf):
            a = jnp.dot(h2, w1_ref[:, c * tf:(c + 1) * tf], preferred_element_type=F32)
            a = jnp.square(jnp.maximum(a, 0.0)).astype(BF16)
            d = jnp.dot(a, w2_ref[c * tf:(c + 1) * tf, :], preferred_element_type=F32)
            acc = d if acc is None else acc + d
        return acc

    pres = [pre(hf) for hf in range(halves)]
    for hf, (x1, h2) in enumerate(pres):
        x2 = x1 + mod[:, 5 * D_MODEL:6 * D_MODEL] * mlp(h2)
        if final:
            ms = jnp.mean(x2 * x2, axis=-1, keepdims=True)
            x2 = x2 * lax.rsqrt(ms + EPS) * fg_ref[...]
        out_ref[hf * hm:(hf + 1) * hm, :] = x2


def _head_expand_matrix():
    e = np.zeros((LANES, D_MODEL), np.float32)
    for h in range(HEADS_AB):
        e[h, h * HEAD_DIM_AB:(h + 1) * HEAD_DIM_AB] = 1.0
    return e


def _post(x2d, attn, mods3, norm_g3, wo, w1, w2, final_g, *, layer, mod_row0, seq, merge, final):
    t = x2d.shape[0]
    tm = TM_POST_MERGE if merge else TM_POST
    hm = tm // POST_HALVES
    per_seq = seq // tm
    row = lambda i: (i, 0)
    const = lambda i: (0, 0)
    resident = pl.Buffered(1)
    in_specs = [pl.BlockSpec((tm, D_MODEL), row)]
    args = [x2d]
    scratch = []
    if merge:
        outs, lses = attn
        def res_spec(arr):
            dil, width = arr.shape[1], arr.shape[3]
            assert hm % (dil * 16) == 0
            return pl.BlockSpec((None, dil, tm // dil, width), lambda i: (i // per_seq, 0, i % per_seq, 0))
        in_specs += [res_spec(a) for a in outs] + [res_spec(a) for a in lses]
        in_specs += [pl.BlockSpec((LANES, D_MODEL), const, pipeline_mode=resident)]
        args += list(outs) + list(lses) + [jnp.asarray(_head_expand_matrix(), BF16)]
        for _ in range(POST_HALVES * (len(outs) - 1)):
            scratch += [pltpu.VMEM((D_MODEL // LANES, hm, LANES), F32), pltpu.VMEM((hm, LANES), F32)]
    else:
        in_specs += [pl.BlockSpec((tm, D_MODEL), row)]
        args += [attn]
    in_specs += [
        pl.BlockSpec((1, 1, 6 * D_MODEL), lambda i: (layer * MOD_ROWS + mod_row0 + i // per_seq, 0, 0)),
        pl.BlockSpec((1, 1, D_MODEL), lambda i: (2 * layer + 1, 0, 0)),
        pl.BlockSpec((D_MODEL, D_MODEL), const, pipeline_mode=resident),
        pl.BlockSpec((D_MODEL, D_FF), const, pipeline_mode=resident),
        pl.BlockSpec((D_FF, D_MODEL), const, pipeline_mode=resident),
    ]
    args += [mods3, norm_g3, wo, w1, w2]
    if final:
        in_specs += [pl.BlockSpec((1, D_MODEL), const)]
        args += [final_g.reshape(1, D_MODEL)]
    return pl.pallas_call(
        functools.partial(_post_kernel, merge=merge, final=final, halves=POST_HALVES, tf=TF_POST),
        grid=(t // tm,),
        in_specs=in_specs,
        out_specs=pl.BlockSpec((tm, D_MODEL), row),
        out_shape=jax.ShapeDtypeStruct((t, D_MODEL), F32),
        scratch_shapes=scratch,
        compiler_params=_cparams(("parallel",)),
        name="wo_mlp_merge" if merge else "wo_mlp",
    )(*args)


def _trunk(x, mods3, mod_row0, norm_g3, final_g, wts, a_bias, b_masks):
    batch, seq, _ = x.shape
    x2d = x.reshape(batch * seq, D_MODEL)
    t = jnp.arange(seq)
    tabs_b = _rope_tables(t, t)
    tabs_c = _rope_tables(t // GRID_W, t % GRID_W)
    for layer in range(DEPTH):
        kind, j = layer % N_MIXERS, layer // N_MIXERS
        common = dict(layer=layer, mod_row0=mod_row0, seq=seq)
        if kind == 0:
            qkv = _qkv_proj(x2d, mods3, norm_g3, wts["a_qkv"][j], mode="a", **common)
            attn = _attn_a(qkv, a_bias[j], batch=batch, seq=seq)
            wo = wts["a_o"][j]
        elif kind == 1:
            groups = []
            for g, (win, dil) in enumerate(B_PAIRS):
                assert win == 2 * B_SIDE * dil
                qkv = _qkv_proj(x2d, mods3, norm_g3, wts["b_qkv"][j], mode="b", cos=tabs_b[0], sin=tabs_b[1],
                                group=g, dil=dil, **common)
                groups.append(_attn_b_group(qkv, b_masks, group=g))
            attn = ([o for o, _ in groups], [l for _, l in groups])
            wo = wts["b_o"][j]
        else:
            qk, vt = _qkv_proj(x2d, mods3, norm_g3, wts["c_qkv"][j], mode="c", cos=tabs_c[0], sin=tabs_c[1],
                               qg=wts["c_qg"][j], kg=wts["c_kg"][j], **common)
            attn = _attn_c(qk, vt, batch=batch, seq=seq)
            wo = wts["c_o"][j]
        x2d = _post(x2d, attn, mods3, norm_g3, wo, wts["w1"][layer], wts["w2"][layer], final_g,
                    merge=(kind == 1), final=(layer == DEPTH - 1), **common)
    return x2d.reshape(batch, seq, D_MODEL)


def kernel(x_prompt, x_sample, c_prompt, c_sample, w_mod, b_mod, norm_g, final_g, a_w_qkv, a_rpb, a_w_o,
           b_w_qkv, b_w_o, c_w_qkv, c_q_g, c_k_g, c_w_o, mlp_w1, mlp_w2):
    nb_p, nb_s = c_prompt.shape[0], c_sample.shape[0]
    assert nb_p + nb_s <= MOD_ROWS
    c_all = jnp.concatenate([c_prompt, c_sample, jnp.zeros((MOD_ROWS - nb_p - nb_s, D_MODEL), F32)], axis=0)
    mods3 = _modulation(c_all, w_mod, b_mod).reshape(DEPTH * MOD_ROWS, 1, 6 * D_MODEL)
    norm_g3 = norm_g.reshape(DEPTH * 2, 1, D_MODEL)

    scale_ab = HEAD_DIM_AB ** -0.5 * LOG2E
    a_qkv = a_w_qkv.at[:, :, :D_MODEL].multiply(scale_ab).astype(BF16)
    b_qkv = b_w_qkv.reshape(-1, D_MODEL, len(B_PAIRS), 3, D_MODEL).at[:, :, :, 0].multiply(scale_ab)
    b_qkv = b_qkv.reshape(b_w_qkv.shape).astype(BF16)
    wts = {
        "a_qkv": a_qkv, "a_o": a_w_o.astype(BF16),
        "b_qkv": b_qkv, "b_o": b_w_o.astype(BF16),
        "c_qkv": c_w_qkv.astype(BF16), "c_o": c_w_o.astype(BF16),
        "c_qg": (c_q_g * (C_HEAD_DIM ** -0.5 * LOG2E)).reshape(-1, 1, C_HEAD_DIM),
        "c_kg": c_k_g.reshape(-1, 1, C_HEAD_DIM),
        "w1": mlp_w1.astype(BF16), "w2": mlp_w2.astype(BF16),
    }
    a_bias = [_attn_a_bias(a_rpb[j]) for j in range(a_rpb.shape[0])]
    b_masks = jnp.asarray(_attn_b_masks())

    y_prompt = _trunk(x_prompt, mods3, 0, norm_g3, final_g, wts, a_bias, b_masks)
    y_sample = _trunk(x_sample, mods3, nb_p, norm_g3, final_g, wts, a_bias, b_masks)
    return (y_prompt, y_sample)
```

```python
import functools

import numpy as np
import jax
import jax.numpy as jnp
from jax import lax
from jax.experimental import pallas as pl
from jax.experimental.pallas import tpu as pltpu

F32 = jnp.float32
BF16 = jnp.bfloat16

D_MODEL = 1024
DEPTH = 4
GRID_W = 64
D_FF = 4 * D_MODEL
EPS = 1e-6
ROPE_THETA = 10000.0
N_MIXERS = 3
HEAD_DIM_AB = 64
HEADS_AB = D_MODEL // HEAD_DIM_AB
A_WIN_ROWS = 8
A_WIN_COLS = 16
B_PAIRS = ((128, 1), (512, 4), (2048, 16))
B_SIDE = 64
C_HEAD_DIM = 128
C_Q_HEADS = 8
C_KV_HEADS = 2
C_GROUP = C_Q_HEADS // C_KV_HEADS
C_QKV_WIDTH = (C_Q_HEADS + 2 * C_KV_HEADS) * C_HEAD_DIM
ROPE_HALF = 32
LANES = 128
MOD_ROWS = 8
MASK_NEG = -1e30
LOG2E = float(np.log2(np.e))
VMEM_LIMIT = 56 * 1024 * 1024

TM_QKV = 1024
TN_QKV = 1024
TN_QKV_C = 512
QKV_HALVES = 2
TM_POST = 1024
TM_POST_MERGE = 512
POST_HALVES = 2
TF_POST = 1024
A_TQ = 4 * GRID_W
B_TQ = 256
B_HALF = 128
C_TQ = 512
C_TK = 1024
C_AHEAD = 2
C_VT_ROWS = C_HEAD_DIM + 16


def _cparams(sem):
    return pltpu.CompilerParams(dimension_semantics=sem, vmem_limit_bytes=VMEM_LIMIT)


def _mod_kernel(c_ref, w_ref, b_ref, o_ref):
    c = c_ref[...]
    act = (c / (1.0 + jnp.exp(-c))).astype(BF16)
    o_ref[0] = jnp.dot(act, w_ref[0].astype(BF16), preferred_element_type=F32) + b_ref[0]


def _modulation(c_all, w_mod, b_mod):
    tn = 1536
    n = 6 * D_MODEL
    return pl.pallas_call(
        _mod_kernel,
        grid=(DEPTH, n // tn),
        in_specs=[
            pl.BlockSpec((MOD_ROWS, D_MODEL), lambda l, j: (0, 0)),
            pl.BlockSpec((1, D_MODEL, tn), lambda l, j: (l, 0, j)),
            pl.BlockSpec((1, 1, tn), lambda l, j: (l, 0, j)),
        ],
        out_specs=pl.BlockSpec((1, MOD_ROWS, tn), lambda l, j: (l, 0, j)),
        out_shape=jax.ShapeDtypeStruct((DEPTH, MOD_ROWS, n), F32),
        compiler_params=_cparams(("parallel", "parallel")),
        name="adaln_modulation",
    )(c_all, w_mod, b_mod.reshape(DEPTH, 1, n))


def _norm_mod(x, gain, scale, shift):
    ms = jnp.mean(x * x, axis=-1, keepdims=True)
    return (x * lax.rsqrt(ms + EPS) * gain) * (1.0 + scale) + shift


def _rope128(x, cos, sin_signed, low_half):
    up = pltpu.roll(x, LANES - ROPE_HALF, 1)
    down = pltpu.roll(x, ROPE_HALF, 1)
    return x * cos + jnp.where(low_half, up, down) * sin_signed


def _low_half_mask():
    lane = lax.broadcasted_iota(jnp.int32, (1, LANES), 1)
    return (lane % (2 * ROPE_HALF)) < ROPE_HALF


def _rope_tables(pos_a, pos_b):
    inv = ROPE_THETA ** (-jnp.arange(ROPE_HALF, dtype=F32) / ROPE_HALF)
    def one(pos):
        ang = pos.astype(F32)[:, None] * inv[None, :]
        c, s = jnp.cos(ang), jnp.sin(ang)
        return jnp.concatenate([c, c], axis=1), jnp.concatenate([-s, s], axis=1)
    ca, sa = one(pos_a)
    cb, sb = one(pos_b)
    return jnp.concatenate([ca, cb], axis=1), jnp.concatenate([sa, sb], axis=1)


def _qkv_kernel(*refs, mode, dil, halves, tn):
    stage_refs = ()
    if mode == "a":
        x_ref, mod_ref, g_ref, w_ref, o_ref = refs
    elif mode == "b":
        x_ref, mod_ref, g_ref, w_ref, cos_ref, sin_ref, o_ref = refs[:7]
        stage_refs = refs[7:]
    else:
        x_ref, mod_ref, g_ref, w_ref, cos_ref, sin_ref, qg_ref, kg_ref, o_ref, vt_ref = refs
    mod = mod_ref[0]
    hm = x_ref.shape[0] // halves
    n = w_ref.shape[1]
    low = _low_half_mask()

    def normed(hf):
        rows = slice(hf * hm, (hf + 1) * hm)
        return _norm_mod(x_ref[rows, :], g_ref[0], mod[:, D_MODEL:2 * D_MODEL], mod[:, 0:D_MODEL]).astype(BF16)

    def emit(hf, j, acc):
        rows = slice(hf * hm, (hf + 1) * hm)
        if mode == "a":
            o_ref[rows, j * tn:(j + 1) * tn] = acc.astype(BF16)
            return
        cos, sin = cos_ref[rows, :], sin_ref[rows, :]
        for c in range(tn // LANES):
            col = j * tn + c * LANES
            val = acc[:, c * LANES:(c + 1) * LANES]
            if mode == "c":
                head = col // C_HEAD_DIM
                if head < C_Q_HEADS + C_KV_HEADS:
                    gain = qg_ref[...] if head < C_Q_HEADS else kg_ref[...]
                    ms = jnp.mean(val * val, axis=-1, keepdims=True)
                    val = _rope128(val * lax.rsqrt(ms + EPS) * gain, cos, sin, low)
                    o_ref[rows, col:col + LANES] = val.astype(BF16)
                else:
                    kv = head - C_Q_HEADS - C_KV_HEADS
                    vt_ref[kv * C_VT_ROWS:kv * C_VT_ROWS + C_HEAD_DIM, rows] = val.T.astype(BF16)
                    vt_ref[kv * C_VT_ROWS + C_HEAD_DIM:(kv + 1) * C_VT_ROWS, rows] = jnp.ones(
                        (C_VT_ROWS - C_HEAD_DIM, hm), BF16)
                continue
            if col < 2 * D_MODEL:
                val = _rope128(val, cos, sin, low)
            sub = hm // dil
            dst = slice(hf * sub, (hf + 1) * sub)
            if dil == 1:
                o_ref[0, 0, dst, col:col + LANES] = val.astype(BF16)
                continue
            stage = stage_refs[hf]
            stage[c] = val
            for r in range(dil):
                o_ref[0, r, dst, col:col + LANES] = stage[c, pl.ds(r, sub, stride=dil), :].astype(BF16)

    hs = [normed(hf) for hf in range(halves)]
    for hf in range(halves):
        for j in range(n // tn):
            emit(hf, j, jnp.dot(hs[hf], w_ref[:, j * tn:(j + 1) * tn], preferred_element_type=F32))


def _qkv_proj(x2d, mods3, norm_g3, w, *, layer, mod_row0, seq, mode, cos=None, sin=None, qg=None, kg=None,
              group=0, dil=1):
    t = x2d.shape[0]
    tm = TM_QKV
    if mode == "b":
        n, col0 = 3 * D_MODEL, group
    else:
        n, col0 = w.shape[1], 0
    tn = TN_QKV_C if mode == "c" else TN_QKV
    hm = tm // QKV_HALVES
    per_seq = seq // tm
    in_specs = [
        pl.BlockSpec((tm, D_MODEL), lambda i: (i, 0)),
        pl.BlockSpec((1, 1, 6 * D_MODEL), lambda i: (layer * MOD_ROWS + mod_row0 + i // per_seq, 0, 0)),
        pl.BlockSpec((1, 1, D_MODEL), lambda i: (2 * layer, 0, 0)),
        pl.BlockSpec((D_MODEL, n), lambda i: (0, col0), pipeline_mode=pl.Buffered(1)),
    ]
    args = [x2d, mods3, norm_g3, w]
    if mode in ("b", "c"):
        in_specs += [pl.BlockSpec((tm, LANES), lambda i: (i % per_seq, 0))] * 2
        args += [cos, sin]
    if mode == "c":
        in_specs += [pl.BlockSpec((1, C_HEAD_DIM), lambda i: (0, 0))] * 2
        args += [qg, kg]
    scratch = []
    if mode == "b":
        assert hm % (dil * 16) == 0
        batch = t // seq
        out_spec = pl.BlockSpec((1, dil, tm // dil, n), lambda i: (i // per_seq, 0, i % per_seq, 0))
        out_shape = jax.ShapeDtypeStruct((batch, dil, seq // dil, n), BF16)
        if dil > 1:
            scratch = [pltpu.VMEM((tn // LANES, hm, LANES), F32) for _ in range(QKV_HALVES)]
    elif mode == "c":
        assert tm == C_TK
        n_qk = (C_Q_HEADS + C_KV_HEADS) * C_HEAD_DIM
        kv_w = C_KV_HEADS * C_VT_ROWS
        out_spec = [pl.BlockSpec((tm, n_qk), lambda i: (i, 0)),
                    pl.BlockSpec((None, None, kv_w, tm), lambda i: (i // per_seq, i % per_seq, 0, 0))]
        out_shape = [jax.ShapeDtypeStruct((t, n_qk), BF16),
                     jax.ShapeDtypeStruct((t // seq, per_seq, kv_w, tm), BF16)]
    else:
        out_spec = pl.BlockSpec((tm, n), lambda i: (i, 0))
        out_shape = jax.ShapeDtypeStruct((t, n), BF16)
    return pl.pallas_call(
        functools.partial(_qkv_kernel, mode=mode, dil=dil, halves=QKV_HALVES, tn=tn),
        grid=(t // tm,),
        in_specs=in_specs,
        out_specs=out_spec,
        out_shape=out_shape,
        scratch_shapes=scratch,
        compiler_params=_cparams(("parallel",)),
        name="norm_qkv_" + mode,
    )(*args)


def _pair_masks():
    lane = lax.broadcasted_iota(jnp.int32, (1, LANES), 1)
    first = lane < HEAD_DIM_AB
    return first, jnp.logical_not(first)


def _dot_nt(a, b):
    return lax.dot_general(a, b, (((1,), (1,)), ((), ())), preferred_element_type=F32)


def _pipelined(units, scores, finish):
    nxt = scores(units[0])
    for i, unit in enumerate(units):
        cur = nxt
        if i + 1 < len(units):
            nxt = scores(units[i + 1])
        finish(unit, cur)


def _attn_a_kernel(q_ref, kp_ref, kc_ref, kn_ref, vp_ref, vc_ref, vn_ref, bias_ref, o_ref):
    masks = _pair_masks()
    k_refs = (kp_ref, kc_ref, kn_ref)
    v_refs = (vp_ref, vc_ref, vn_ref)
    outs = {}

    def scores(head):
        sl = slice((head // 2) * LANES, (head // 2 + 1) * LANES)
        qp = q_ref[:, sl]
        qm = jnp.where(masks[head % 2], qp, jnp.zeros_like(qp))
        return _dot_nt(qm, jnp.concatenate([r[:, sl] for r in k_refs], axis=0)) + bias_ref[0, head]

    def finish(head, s):
        sl = slice((head // 2) * LANES, (head // 2 + 1) * LANES)
        m = s.max(axis=-1, keepdims=True)
        pr = jnp.exp2(s - m)
        l = pr.sum(axis=-1, keepdims=True)
        vcat = jnp.concatenate([r[:, sl] for r in v_refs], axis=0)
        outs[head] = jnp.dot(pr.astype(BF16), vcat, preferred_element_type=F32) / l
        if head % 2 == 1:
            o_ref[:, sl] = jnp.where(masks[0], outs.pop(head - 1), outs.pop(head)).astype(BF16)

    _pipelined(list(range(HEADS_AB)), scores, finish)


def _attn_a_bias(rpb):
    rows_per = A_TQ // GRID_W
    n_dr, n_dc = 2 * A_WIN_ROWS - 1, 2 * A_WIN_COLS - 1
    ql, kl = np.arange(rows_per)[:, None], np.arange(3 * rows_per)[None, :]
    c, kc = np.arange(GRID_W)[:, None], np.arange(GRID_W)[None, :]
    dr = kl - rows_per - ql + A_WIN_ROWS - 1
    dc = kc - c + A_WIN_COLS - 1
    onehot_r = (dr[..., None] == np.arange(n_dr)).astype(np.float32)
    onehot_c = (dc[..., None] == np.arange(n_dc)).astype(np.float32)
    table = jnp.einsum("hde,qkd,cje->hqckj", rpb * LOG2E, onehot_r, onehot_c, precision=lax.Precision.HIGHEST)
    cs = np.clip(c - A_WIN_COLS // 2, 0, GRID_W - A_WIN_COLS)
    col_ok = (kc >= cs) & (kc < cs + A_WIN_COLS)
    rows = 8 * rows_per
    out = []
    for i in (0, 1, rows // rows_per - 1):
        r = rows_per * i + ql
        rs = np.clip(r - A_WIN_ROWS // 2, 0, rows - A_WIN_ROWS)
        kr = rows_per * (i - 1) + kl
        row_ok = (kr >= rs) & (kr < rs + A_WIN_ROWS)
        ok = row_ok[:, None, :, None] & col_ok[None, :, None, :]
        out.append(jnp.where(jnp.asarray(ok)[None], table, MASK_NEG).reshape(HEADS_AB, A_TQ, 3 * A_TQ))
    return jnp.stack(out)


def _attn_a(qkv, bias, *, batch, seq):
    tq = A_TQ
    nb = seq // tq
    assert nb >= 3
    q_spec = pl.BlockSpec((tq, D_MODEL), lambda b, i: (b * nb + i, 0))
    def kv_spec(col, off):
        return pl.BlockSpec((tq, D_MODEL), lambda b, i: (b * nb + jnp.clip(i + off, 0, nb - 1), col))
    bias_spec = pl.BlockSpec((1, HEADS_AB, tq, 3 * tq),
                             lambda b, i: (jnp.where(i == 0, 0, jnp.where(i == nb - 1, 2, 1)), 0, 0, 0))
    return pl.pallas_call(
        _attn_a_kernel,
        grid=(batch, nb),
        in_specs=[q_spec, kv_spec(1, -1), kv_spec(1, 0), kv_spec(1, 1),
                  kv_spec(2, -1), kv_spec(2, 0), kv_spec(2, 1), bias_spec],
        out_specs=pl.BlockSpec((tq, D_MODEL), lambda b, i: (b * nb + i, 0)),
        out_shape=jax.ShapeDtypeStruct((batch * seq, D_MODEL), BF16),
        compiler_params=_cparams(("parallel", "arbitrary")),
        name="attn_neighbourhood",
    )(qkv, qkv, qkv, qkv, qkv, qkv, qkv, bias)


def _attn_b_kernel(q_ref, kp_ref, kc_ref, kn_ref, vp_ref, vc_ref, vn_ref, mask_ref, o_ref, lse_ref):
    tq = q_ref.shape[0]
    masks = _pair_masks()
    lane = lax.broadcasted_iota(jnp.int32, (1, LANES), 1)
    lse_tiles = [jnp.zeros((B_HALF, LANES), F32) for _ in range(tq // B_HALF)]
    units = [(p, hf) for p in range(HEADS_AB // 2) for hf in range(tq // B_HALF)]

    def window(prev_ref, cur_ref, next_ref, hf, sl):
        lo, hi = hf * B_HALF - B_SIDE, (hf + 1) * B_HALF + B_SIDE
        parts = []
        if lo < 0:
            parts.append(prev_ref[tq + lo:, sl])
        parts.append(cur_ref[max(lo, 0):min(hi, tq), sl])
        if hi > tq:
            parts.append(next_ref[:hi - tq, sl])
        return jnp.concatenate(parts, axis=0)

    def scores(unit):
        p, hf = unit
        sl = slice(p * LANES, (p + 1) * LANES)
        qp = q_ref[hf * B_HALF:(hf + 1) * B_HALF, sl]
        stacked = jnp.concatenate([jnp.where(mk, qp, jnp.zeros_like(qp)) for mk in masks], axis=0)
        return _dot_nt(stacked, window(kp_ref, kc_ref, kn_ref, hf, sl)) + mask_ref[0, hf]

    def finish(unit, s):
        p, hf = unit
        sl = slice(p * LANES, (p + 1) * LANES)
        m = s.max(axis=-1, keepdims=True)
        pr = jnp.exp2(s - m)
        l = pr.sum(axis=-1, keepdims=True)
        o = jnp.dot(pr.astype(BF16), window(vp_ref, vc_ref, vn_ref, hf, sl), preferred_element_type=F32) / l
        o_ref[hf * B_HALF:(hf + 1) * B_HALF, sl] = jnp.where(masks[0], o[:B_HALF], o[B_HALF:]).astype(BF16)
        lse = m + jnp.log2(l)
        tile = jnp.where(lane == 2 * p, lse[:B_HALF], lse_tiles[hf])
        lse_tiles[hf] = jnp.where(lane == 2 * p + 1, lse[B_HALF:], tile)

    _pipelined(units, scores, finish)
    for hf, tile in enumerate(lse_tiles):
        lse_ref[hf * B_HALF:(hf + 1) * B_HALF, :] = tile


def _attn_b_masks():
    qq = np.arange(B_HALF)[:, None]
    jj = np.arange(B_HALF + 2 * B_SIDE)[None, :]
    band = (jj - qq >= 0) & (jj - qq <= 2 * B_SIDE)
    out = []
    for ty in range(4):
        per_half = []
        for hf in range(B_TQ // B_HALF):
            pos = hf * B_HALF - B_SIDE + jj
            ok = band
            if ty & 1:
                ok = ok & (pos >= 0)
            if ty & 2:
                ok = ok & (pos < B_TQ)
            half = np.where(ok, 0.0, MASK_NEG)
            per_half.append(np.concatenate([half, half], axis=0))
        out.append(np.stack(per_half))
    return np.stack(out).astype(np.float32)


def _attn_b_group(qkv, masks, *, group):
    batch, dil, sub, _ = qkv.shape
    tq = B_TQ
    nb = sub // tq
    assert sub % tq == 0
    q_spec = pl.BlockSpec((None, None, tq, D_MODEL), lambda b, r, i: (b, r, i, 0))
    def kv_spec(which, off):
        return pl.BlockSpec((None, None, tq, D_MODEL),
                            lambda b, r, i: (b, r, jnp.clip(i + off, 0, nb - 1), which))
    mask_spec = pl.BlockSpec((1,) + masks.shape[1:],
                             lambda b, r, i: ((i == 0).astype(jnp.int32) + 2 * (i == nb - 1).astype(jnp.int32),
                                              0, 0, 0))
    return pl.pallas_call(
        _attn_b_kernel,
        grid=(batch, dil, nb),
        in_specs=[q_spec, kv_spec(1, -1), kv_spec(1, 0), kv_spec(1, 1),
                  kv_spec(2, -1), kv_spec(2, 0), kv_spec(2, 1), mask_spec],
        out_specs=[pl.BlockSpec((None, None, tq, D_MODEL), lambda b, r, i: (b, r, i, 0)),
                   pl.BlockSpec((None, None, tq, LANES), lambda b, r, i: (b, r, i, 0))],
        out_shape=[jax.ShapeDtypeStruct((batch, dil, sub, D_MODEL), BF16),
                   jax.ShapeDtypeStruct((batch, dil, sub, LANES), F32)],
        compiler_params=_cparams(("parallel", "parallel", "arbitrary")),
        name="attn_dilated_g%d" % group,
    )(qkv, qkv, qkv, qkv, qkv, qkv, qkv, masks)


def _attn_c_kernel(q_ref, k_ref, vt_ref, o_ref, s_ref, m_ref, acc_ref, *, tk):
    nk = k_ref.shape[0] // tk
    m_ref[...] = jnp.full(m_ref.shape, MASK_NEG, F32)
    acc_ref[...] = jnp.zeros(acc_ref.shape, F32)

    def issue_scores(h, c):
        start = pl.multiple_of(c * tk, tk)
        s_ref[h] = _dot_nt(k_ref[pl.ds(start, tk), :], q_ref[:, h * C_HEAD_DIM:(h + 1) * C_HEAD_DIM])

    def chunk(c, last):
        vt = vt_ref[c]
        for h in range(C_GROUP):
            ahead = h + C_AHEAD
            if ahead < C_GROUP:
                issue_scores(ahead, c)
            elif not last:
                issue_scores(ahead - C_GROUP, c + 1)
            s = s_ref[h]
            m_prev = m_ref[h]
            m_new = jnp.maximum(m_prev, s.max(axis=0, keepdims=True))
            alpha = jnp.exp2(m_prev - m_new)
            pr = jnp.exp2((s - m_new).astype(BF16))
            acc_ref[h] = alpha * acc_ref[h] + jnp.dot(vt, pr, preferred_element_type=F32)
            m_ref[h] = m_new

    for h in range(C_AHEAD):
        issue_scores(h, 0)

    def body(c, carry):
        chunk(c, last=False)
        return carry

    lax.fori_loop(0, nk - 1, body, 0)
    chunk(nk - 1, last=True)
    for h in range(C_GROUP):
        acc = acc_ref[h]
        out = acc[:C_HEAD_DIM] / acc[C_HEAD_DIM:C_HEAD_DIM + 1]
        o_ref[:, h * C_HEAD_DIM:(h + 1) * C_HEAD_DIM] = out.T.astype(BF16)


def _attn_c(qk, vt, *, batch, seq):
    tq, tk = C_TQ, C_TK
    nq = seq // tq
    assert seq % tk == 0 and C_AHEAD < C_GROUP
    qw = C_GROUP * C_HEAD_DIM
    return pl.pallas_call(
        functools.partial(_attn_c_kernel, tk=tk),
        grid=(batch, C_KV_HEADS, nq),
        in_specs=[
            pl.BlockSpec((tq, qw), lambda b, g, i: (b * nq + i, g)),
            pl.BlockSpec((seq, C_HEAD_DIM), lambda b, g, i: (b, C_Q_HEADS + g)),
            pl.BlockSpec((None, seq // tk, C_VT_ROWS, tk), lambda b, g, i: (b, 0, g, 0)),
        ],
        out_specs=pl.BlockSpec((tq, qw), lambda b, g, i: (b * nq + i, g)),
        out_shape=jax.ShapeDtypeStruct((batch * seq, C_Q_HEADS * C_HEAD_DIM), BF16),
        scratch_shapes=[pltpu.VMEM((C_GROUP, tk, tq), F32),
                        pltpu.VMEM((C_GROUP, 1, tq), F32),
                        pltpu.VMEM((C_GROUP, C_VT_ROWS, tq), F32)],
        compiler_params=_cparams(("parallel", "parallel", "arbitrary")),
        name="attn_gqa_flash",
    )(qk, qk, vt)


def _post_kernel(*refs, merge, final, halves, tf):
    refs = list(refs)
    x_ref = refs.pop(0)
    if merge:
        o_refs = [refs.pop(0) for _ in range(3)]
        lse_refs = [refs.pop(0) for _ in range(3)]
        expand_ref = refs.pop(0)
    else:
        o_ref_in = refs.pop(0)
    mod_ref, g_ref, wo_ref, w1_ref, w2_ref = [refs.pop(0) for _ in range(5)]
    fg_ref = refs.pop(0) if final else None
    out_ref = refs.pop(0)
    stage_refs = refs
    mod = mod_ref[0]
    hm = x_ref.shape[0] // halves

    def merged_groups(hf):
        o_tok, lse = [], []
        for g, (o_g, lse_g) in enumerate(zip(o_refs, lse_refs)):
            dil = o_g.shape[0]
            rows = hm // dil
            src = slice(hf * rows, (hf + 1) * rows)
            if dil == 1:
                o_tok.append(o_g[0, src, :].astype(F32))
                lse.append(lse_g[0, src, :])
                continue
            o_stage, lse_stage = stage_refs[4 * hf + 2 * (g - 1)], stage_refs[4 * hf + 2 * (g - 1) + 1]
            chunks = o_g.shape[2] // LANES
            for r in range(dil):
                lse_stage[pl.ds(r, rows, stride=dil), :] = lse_g[r, src, :]
                for c in range(chunks):
                    o_stage[c, pl.ds(r, rows, stride=dil), :] = o_g[r, src, c * LANES:(c + 1) * LANES].astype(F32)
            o_tok.append(jnp.concatenate([o_stage[c] for c in range(chunks)], axis=1))
            lse.append(lse_stage[...])
        top = jnp.maximum(jnp.maximum(lse[0], lse[1]), lse[2])
        ex = [jnp.exp2(v - top) for v in lse]
        den = ex[0] + ex[1] + ex[2]
        mixed = None
        for g in range(3):
            wgt = ex[g] / den
            hi = wgt.astype(BF16)
            lo = (wgt - hi.astype(F32)).astype(BF16)
            wide = (jnp.dot(hi, expand_ref[...], preferred_element_type=F32)
                    + jnp.dot(lo, expand_ref[...], preferred_element_type=F32))
            mixed = wide * o_tok[g] if mixed is None else mixed + wide * o_tok[g]
        return mixed.astype(BF16)

    def pre(hf):
        rows = slice(hf * hm, (hf + 1) * hm)
        o = merged_groups(hf) if merge else o_ref_in[rows, :]
        mix = jnp.dot(o, wo_ref[...], preferred_element_type=F32)
        x1 = x_ref[rows, :] + mod[:, 2 * D_MODEL:3 * D_MODEL] * mix
        h2 = _norm_mod(x1, g_ref[0], mod[:, 4 * D_MODEL:5 * D_MODEL], mod[:, 3 * D_MODEL:4 * D_MODEL])
        return x1, h2.astype(BF16)

    def mlp(h2):
        acc = None
        for c in range(D_FF // tf):
            a = jnp.dot(h2, w1_ref[:, c * tf:(c + 1) * tf], preferred_element_type=F32)
            a = jnp.square(jnp.maximum(a, 0.0)).astype(BF16)
            d = jnp.dot(a, w2_ref[c * tf:(c + 1) * tf, :], preferred_element_type=F32)
            acc = d if acc is None else acc + d
        return acc

    pres = [pre(hf) for hf in range(halves)]
    for hf, (x1, h2) in enumerate(pres):
        x2 = x1 + mod[:, 5 * D_MODEL:6 * D_MODEL] * mlp(h2)
        if final:
            ms = jnp.mean(x2 * x2, axis=-1, keepdims=True)
            x2 = x2 * lax.rsqrt(ms + EPS) * fg_ref[...]
        out_ref[hf * hm:(hf + 1) * hm, :] = x2


def _head_expand_matrix():
    e = np.zeros((LANES, D_MODEL), np.float32)
    for h in range(HEADS_AB):
        e[h, h * HEAD_DIM_AB:(h + 1) * HEAD_DIM_AB] = 1.0
    return e


def _post(x2d, attn, mods3, norm_g3, wo, w1, w2, final_g, *, layer, mod_row0, seq, merge, final):
    t = x2d.shape[0]
    tm = TM_POST_MERGE if merge else TM_POST
    hm = tm // POST_HALVES
    per_seq = seq // tm
    row = lambda i: (i, 0)
    const = lambda i: (0, 0)
    resident = pl.Buffered(1)
    in_specs = [pl.BlockSpec((tm, D_MODEL), row)]
    args = [x2d]
    scratch = []
    if merge:
        outs, lses = attn
        def res_spec(arr):
            dil, width = arr.shape[1], arr.shape[3]
            assert hm % (dil * 16) == 0
            return pl.BlockSpec((None, dil, tm // dil, width), lambda i: (i // per_seq, 0, i % per_seq, 0))
        in_specs += [res_spec(a) for a in outs] + [res_spec(a) for a in lses]
        in_specs += [pl.BlockSpec((LANES, D_MODEL), const, pipeline_mode=resident)]
        args += list(outs) + list(lses) + [jnp.asarray(_head_expand_matrix(), BF16)]
        for _ in range(POST_HALVES * (len(outs) - 1)):
            scratch += [pltpu.VMEM((D_MODEL // LANES, hm, LANES), F32), pltpu.VMEM((hm, LANES), F32)]
    else:
        in_specs += [pl.BlockSpec((tm, D_MODEL), row)]
        args += [attn]
    in_specs += [
        pl.BlockSpec((1, 1, 6 * D_MODEL), lambda i: (layer * MOD_ROWS + mod_row0 + i // per_seq, 0, 0)),
        pl.BlockSpec((1, 1, D_MODEL), lambda i: (2 * layer + 1, 0, 0)),
        pl.BlockSpec((D_MODEL, D_MODEL), const, pipeline_mode=resident),
        pl.BlockSpec((D_MODEL, D_FF), const, pipeline_mode=resident),
        pl.BlockSpec((D_FF, D_MODEL), const, pipeline_mode=resident),
    ]
    args += [mods3, norm_g3, wo, w1, w2]
    if final:
        in_specs += [pl.BlockSpec((1, D_MODEL), const)]
        args += [final_g.reshape(1, D_MODEL)]
    return pl.pallas_call(
        functools.partial(_post_kernel, merge=merge, final=final, halves=POST_HALVES, tf=TF_POST),
        grid=(t // tm,),
        in_specs=in_specs,
        out_specs=pl.BlockSpec((tm, D_MODEL), row),
        out_shape=jax.ShapeDtypeStruct((t, D_MODEL), F32),
        scratch_shapes=scratch,
        compiler_params=_cparams(("parallel",)),
        name="wo_mlp_merge" if merge else "wo_mlp",
    )(*args)


def _trunk(x, mods3, mod_row0, norm_g3, final_g, wts, a_bias, b_masks):
    batch, seq, _ = x.shape
    x2d = x.reshape(batch * seq, D_MODEL)
    t = jnp.arange(seq)
    tabs_b = _rope_tables(t, t)
    tabs_c = _rope_tables(t // GRID_W, t % GRID_W)
    for layer in range(DEPTH):
        kind, j = layer % N_MIXERS, layer // N_MIXERS
        common = dict(layer=layer, mod_row0=mod_row0, seq=seq)
        if kind == 0:
            qkv = _qkv_proj(x2d, mods3, norm_g3, wts["a_qkv"][j], mode="a", **common)
            attn = _attn_a(qkv, a_bias[j], batch=batch, seq=seq)
            wo = wts["a_o"][j]
        elif kind == 1:
            groups = []
            for g, (win, dil) in enumerate(B_PAIRS):
                assert win == 2 * B_SIDE * dil
                qkv = _qkv_proj(x2d, mods3, norm_g3, wts["b_qkv"][j], mode="b", cos=tabs_b[0], sin=tabs_b[1],
                                group=g, dil=dil, **common)
                groups.append(_attn_b_group(qkv, b_masks, group=g))
            attn = ([o for o, _ in groups], [l for _, l in groups])
            wo = wts["b_o"][j]
        else:
            qk, vt = _qkv_proj(x2d, mods3, norm_g3, wts["c_qkv"][j], mode="c", cos=tabs_c[0], sin=tabs_c[1],
                               qg=wts["c_qg"][j], kg=wts["c_kg"][j], **common)
            attn = _attn_c(qk, vt, batch=batch, seq=seq)
            wo = wts["c_o"][j]
        x2d = _post(x2d, attn, mods3, norm_g3, wo, wts["w1"][layer], wts["w2"][layer], final_g,
                    merge=(kind == 1), final=(layer == DEPTH - 1), **common)
    return x2d.reshape(batch, seq, D_MODEL)


def kernel(x_prompt, x_sample, c_prompt, c_sample, w_mod, b_mod, norm_g, final_g, a_w_qkv, a_rpb, a_w_o,
           b_w_qkv, b_w_o, c_w_qkv, c_q_g, c_k_g, c_w_o, mlp_w1, mlp_w2):
    nb_p, nb_s = c_prompt.shape[0], c_sample.shape[0]
    assert nb_p + nb_s <= MOD_ROWS
    c_all = jnp.concatenate([c_prompt, c_sample, jnp.zeros((MOD_ROWS - nb_p - nb_s, D_MODEL), F32)], axis=0)
    mods3 = _modulation(c_all, w_mod, b_mod).reshape(DEPTH * MOD_ROWS, 1, 6 * D_MODEL)
    norm_g3 = norm_g.reshape(DEPTH * 2, 1, D_MODEL)

    scale_ab = HEAD_DIM_AB ** -0.5 * LOG2E
    a_qkv = a_w_qkv.at[:, :, :D_MODEL].multiply(scale_ab).astype(BF16)
    b_qkv = b_w_qkv.reshape(-1, D_MODEL, len(B_PAIRS), 3, D_MODEL).at[:, :, :, 0].multiply(scale_ab)
    b_qkv = b_qkv.reshape(b_w_qkv.shape).astype(BF16)
    wts = {
        "a_qkv": a_qkv, "a_o": a_w_o.astype(BF16),
        "b_qkv": b_qkv, "b_o": b_w_o.astype(BF16),
        "c_qkv": c_w_qkv.astype(BF16), "c_o": c_w_o.astype(BF16),
        "c_qg": (c_q_g * (C_HEAD_DIM ** -0.5 * LOG2E)).reshape(-1, 1, C_HEAD_DIM),
        "c_kg": c_k_g.reshape(-1, 1, C_HEAD_DIM),
        "w1": mlp_w1.astype(BF16), "w2": mlp_w2.astype(BF16),
    }
    a_bias = [_attn_a_bias(a_rpb[j]) for j in range(a_rpb.shape[0])]
    b_masks = jnp.asarray(_attn_b_masks())

    y_prompt = _trunk(x_prompt, mods3, 0, norm_g3, final_g, wts, a_bias, b_masks)
    y_sample = _trunk(x_sample, mods3, nb_p, norm_g3, final_g, wts, a_bias, b_masks)
    return (y_prompt, y_sample)
```

```python
import functools

import numpy as np
import jax
import jax.numpy as jnp
from jax import lax
from jax.experimental import pallas as pl
from jax.experimental.pallas import tpu as pltpu

F32 = jnp.float32
BF16 = jnp.bfloat16

D_MODEL = 1024
DEPTH = 4
GRID_W = 64
D_FF = 4 * D_MODEL
EPS = 1e-6
ROPE_THETA = 10000.0
N_MIXERS = 3
HEAD_DIM_AB = 64
HEADS_AB = D_MODEL // HEAD_DIM_AB
A_WIN_ROWS = 8
A_WIN_COLS = 16
B_PAIRS = ((128, 1), (512, 4), (2048, 16))
B_SIDE = 64
C_HEAD_DIM = 128
C_Q_HEADS = 8
C_KV_HEADS = 2
C_GROUP = C_Q_HEADS // C_KV_HEADS
C_QKV_WIDTH = (C_Q_HEADS + 2 * C_KV_HEADS) * C_HEAD_DIM
ROPE_HALF = 32
LANES = 128
MOD_ROWS = 8
MASK_NEG = -1e30
LOG2E = float(np.log2(np.e))
VMEM_LIMIT = 56 * 1024 * 1024

TM_QKV = 1024
TN_QKV = 256
TN_QKV_C = 512
QKV_HALVES = 2
TM_POST = 1024
TM_POST_MERGE = 512
POST_HALVES = 2
TF_POST = 1024
A_TQ = 4 * GRID_W
B_TQ = 256
B_HALF = 128
C_TQ = 512
C_TK = 1024


def _cparams(sem):
    return pltpu.CompilerParams(dimension_semantics=sem, vmem_limit_bytes=VMEM_LIMIT)


def _mod_kernel(c_ref, w_ref, b_ref, o_ref):
    c = c_ref[...]
    act = (c / (1.0 + jnp.exp(-c))).astype(BF16)
    o_ref[0] = jnp.dot(act, w_ref[0].astype(BF16), preferred_element_type=F32) + b_ref[0]


def _modulation(c_all, w_mod, b_mod):
    tn = 1536
    n = 6 * D_MODEL
    return pl.pallas_call(
        _mod_kernel,
        grid=(DEPTH, n // tn),
        in_specs=[
            pl.BlockSpec((MOD_ROWS, D_MODEL), lambda l, j: (0, 0)),
            pl.BlockSpec((1, D_MODEL, tn), lambda l, j: (l, 0, j)),
            pl.BlockSpec((1, 1, tn), lambda l, j: (l, 0, j)),
        ],
        out_specs=pl.BlockSpec((1, MOD_ROWS, tn), lambda l, j: (l, 0, j)),
        out_shape=jax.ShapeDtypeStruct((DEPTH, MOD_ROWS, n), F32),
        compiler_params=_cparams(("parallel", "parallel")),
        name="adaln_modulation",
    )(c_all, w_mod, b_mod.reshape(DEPTH, 1, n))


def _norm_mod(x, gain, scale, shift):
    ms = jnp.mean(x * x, axis=-1, keepdims=True)
    return (x * lax.rsqrt(ms + EPS) * gain) * (1.0 + scale) + shift


def _rope128(x, cos, sin_signed, low_half):
    up = pltpu.roll(x, LANES - ROPE_HALF, 1)
    down = pltpu.roll(x, ROPE_HALF, 1)
    return x * cos + jnp.where(low_half, up, down) * sin_signed


def _low_half_mask():
    lane = lax.broadcasted_iota(jnp.int32, (1, LANES), 1)
    return (lane % (2 * ROPE_HALF)) < ROPE_HALF


def _rope_tables(pos_a, pos_b):
    inv = ROPE_THETA ** (-jnp.arange(ROPE_HALF, dtype=F32) / ROPE_HALF)
    def one(pos):
        ang = pos.astype(F32)[:, None] * inv[None, :]
        c, s = jnp.cos(ang), jnp.sin(ang)
        return jnp.concatenate([c, c], axis=1), jnp.concatenate([-s, s], axis=1)
    ca, sa = one(pos_a)
    cb, sb = one(pos_b)
    return jnp.concatenate([ca, cb], axis=1), jnp.concatenate([sa, sb], axis=1)


def _qkv_kernel(*refs, mode, dil, halves, tn):
    if mode == "a":
        x_ref, mod_ref, g_ref, w_ref, o_ref = refs
    elif mode == "b":
        x_ref, mod_ref, g_ref, w_ref, cos_ref, sin_ref, o_ref = refs
    else:
        x_ref, mod_ref, g_ref, w_ref, cos_ref, sin_ref, qg_ref, kg_ref, o_ref, vt_ref = refs
    mod = mod_ref[0]
    hm = x_ref.shape[0] // halves
    n = w_ref.shape[1]
    low = _low_half_mask()

    def normed(hf):
        rows = slice(hf * hm, (hf + 1) * hm)
        return _norm_mod(x_ref[rows, :], g_ref[0], mod[:, D_MODEL:2 * D_MODEL], mod[:, 0:D_MODEL]).astype(BF16)

    def emit(hf, j, acc):
        rows = slice(hf * hm, (hf + 1) * hm)
        if mode == "a":
            o_ref[rows, j * tn:(j + 1) * tn] = acc.astype(BF16)
            return
        cos, sin = cos_ref[rows, :], sin_ref[rows, :]
        for c in range(tn // LANES):
            col = j * tn + c * LANES
            val = acc[:, c * LANES:(c + 1) * LANES]
            if mode == "c":
                head = col // C_HEAD_DIM
                if head < C_Q_HEADS + C_KV_HEADS:
                    gain = qg_ref[...] if head < C_Q_HEADS else kg_ref[...]
                    ms = jnp.mean(val * val, axis=-1, keepdims=True)
                    val = _rope128(val * lax.rsqrt(ms + EPS) * gain, cos, sin, low)
                    o_ref[rows, col:col + LANES] = val.astype(BF16)
                else:
                    kv = head - C_Q_HEADS - C_KV_HEADS
                    vt_ref[kv * C_HEAD_DIM:(kv + 1) * C_HEAD_DIM, rows] = val.T.astype(BF16)
                continue
            if col < 2 * D_MODEL:
                val = _rope128(val, cos, sin, low)
            sub = hm // dil
            dst = slice(hf * sub, (hf + 1) * sub)
            if dil == 1:
                o_ref[0, 0, dst, col:col + LANES] = val.astype(BF16)
                continue
            by_residue = jnp.swapaxes(val.reshape(sub, dil, LANES), 0, 1)
            o_ref[0, :, dst, col:col + LANES] = by_residue.astype(BF16)

    hs = [normed(hf) for hf in range(halves)]
    for hf in range(halves):
        for j in range(n // tn):
            emit(hf, j, jnp.dot(hs[hf], w_ref[:, j * tn:(j + 1) * tn], preferred_element_type=F32))


def _qkv_proj(x2d, mods3, norm_g3, w, *, layer, mod_row0, seq, mode, cos=None, sin=None, qg=None, kg=None,
              group=0, dil=1):
    t = x2d.shape[0]
    tm = TM_QKV
    if mode == "b":
        n, col0 = 3 * D_MODEL, group
    else:
        n, col0 = w.shape[1], 0
    tn = TN_QKV_C if mode == "c" else TN_QKV
    hm = tm // QKV_HALVES
    per_seq = seq // tm
    in_specs = [
        pl.BlockSpec((tm, D_MODEL), lambda i: (i, 0)),
        pl.BlockSpec((1, 1, 6 * D_MODEL), lambda i: (layer * MOD_ROWS + mod_row0 + i // per_seq, 0, 0)),
        pl.BlockSpec((1, 1, D_MODEL), lambda i: (2 * layer, 0, 0)),
        pl.BlockSpec((D_MODEL, n), lambda i: (0, col0), pipeline_mode=pl.Buffered(1)),
    ]
    args = [x2d, mods3, norm_g3, w]
    if mode in ("b", "c"):
        in_specs += [pl.BlockSpec((tm, LANES), lambda i: (i % per_seq, 0))] * 2
        args += [cos, sin]
    if mode == "c":
        in_specs += [pl.BlockSpec((1, C_HEAD_DIM), lambda i: (0, 0))] * 2
        args += [qg, kg]
    scratch = []
    if mode == "b":
        assert hm % (dil * 16) == 0
        batch = t // seq
        out_spec = pl.BlockSpec((1, dil, tm // dil, n), lambda i: (i // per_seq, 0, i % per_seq, 0))
        out_shape = jax.ShapeDtypeStruct((batch, dil, seq // dil, n), BF16)
    elif mode == "c":
        assert tm == C_TK
        n_qk = (C_Q_HEADS + C_KV_HEADS) * C_HEAD_DIM
        kv_w = C_KV_HEADS * C_HEAD_DIM
        out_spec = [pl.BlockSpec((tm, n_qk), lambda i: (i, 0)),
                    pl.BlockSpec((None, None, kv_w, tm), lambda i: (i // per_seq, i % per_seq, 0, 0))]
        out_shape = [jax.ShapeDtypeStruct((t, n_qk), BF16),
                     jax.ShapeDtypeStruct((t // seq, per_seq, kv_w, tm), BF16)]
    else:
        out_spec = pl.BlockSpec((tm, n), lambda i: (i, 0))
        out_shape = jax.ShapeDtypeStruct((t, n), BF16)
    return pl.pallas_call(
        functools.partial(_qkv_kernel, mode=mode, dil=dil, halves=QKV_HALVES, tn=tn),
        grid=(t // tm,),
        in_specs=in_specs,
        out_specs=out_spec,
        out_shape=out_shape,
        scratch_shapes=scratch,
        compiler_params=_cparams(("parallel",)),
        name="norm_qkv_" + mode,
    )(*args)


def _pair_masks():
    lane = lax.broadcasted_iota(jnp.int32, (1, LANES), 1)
    first = lane < HEAD_DIM_AB
    return first, jnp.logical_not(first)


def _dot_nt(a, b):
    return lax.dot_general(a, b, (((1,), (1,)), ((), ())), preferred_element_type=F32)


def _pipelined(units, scores, finish):
    nxt = scores(units[0])
    for i, unit in enumerate(units):
        cur = nxt
        if i + 1 < len(units):
            nxt = scores(units[i + 1])
        finish(unit, cur)


def _attn_a_kernel(q_ref, kp_ref, kc_ref, kn_ref, vp_ref, vc_ref, vn_ref, bias_ref, o_ref):
    masks = _pair_masks()
    k_refs = (kp_ref, kc_ref, kn_ref)
    v_refs = (vp_ref, vc_ref, vn_ref)
    outs = {}

    def scores(head):
        sl = slice((head // 2) * LANES, (head // 2 + 1) * LANES)
        qp = q_ref[:, sl]
        qm = jnp.where(masks[head % 2], qp, jnp.zeros_like(qp))
        s = _dot_nt(qm, jnp.concatenate([r[:, sl] for r in k_refs], axis=0)) + bias_ref[0, head]
        return s, s.max(axis=-1, keepdims=True)

    def finish(head, scored):
        s, m = scored
        sl = slice((head // 2) * LANES, (head // 2 + 1) * LANES)
        pr = jnp.exp2(s - m)
        l = pr.sum(axis=-1, keepdims=True)
        vcat = jnp.concatenate([r[:, sl] for r in v_refs], axis=0)
        outs[head] = jnp.dot(pr.astype(BF16), vcat, preferred_element_type=F32) / l
        if head % 2 == 1:
            o_ref[:, sl] = jnp.where(masks[0], outs.pop(head - 1), outs.pop(head)).astype(BF16)

    _pipelined(list(range(HEADS_AB)), scores, finish)


def _attn_a_bias(rpb):
    rows_per = A_TQ // GRID_W
    n_dr, n_dc = 2 * A_WIN_ROWS - 1, 2 * A_WIN_COLS - 1
    ql, kl = np.arange(rows_per)[:, None], np.arange(3 * rows_per)[None, :]
    c, kc = np.arange(GRID_W)[:, None], np.arange(GRID_W)[None, :]
    dr = kl - rows_per - ql + A_WIN_ROWS - 1
    dc = kc - c + A_WIN_COLS - 1
    onehot_r = (dr[..., None] == np.arange(n_dr)).astype(np.float32)
    onehot_c = (dc[..., None] == np.arange(n_dc)).astype(np.float32)
    table = jnp.einsum("hde,qkd,cje->hqckj", rpb * LOG2E, onehot_r, onehot_c, precision=lax.Precision.HIGHEST)
    cs = np.clip(c - A_WIN_COLS // 2, 0, GRID_W - A_WIN_COLS)
    col_ok = (kc >= cs) & (kc < cs + A_WIN_COLS)
    rows = 8 * rows_per
    out = []
    for i in (0, 1, rows // rows_per - 1):
        r = rows_per * i + ql
        rs = np.clip(r - A_WIN_ROWS // 2, 0, rows - A_WIN_ROWS)
        kr = rows_per * (i - 1) + kl
        row_ok = (kr >= rs) & (kr < rs + A_WIN_ROWS)
        ok = row_ok[:, None, :, None] & col_ok[None, :, None, :]
        out.append(jnp.where(jnp.asarray(ok)[None], table, MASK_NEG).reshape(HEADS_AB, A_TQ, 3 * A_TQ))
    return jnp.stack(out)


def _attn_a(qkv, bias, *, batch, seq):
    tq = A_TQ
    nb = seq // tq
    assert nb >= 3
    q_spec = pl.BlockSpec((tq, D_MODEL), lambda b, i: (b * nb + i, 0))
    def kv_spec(col, off):
        return pl.BlockSpec((tq, D_MODEL), lambda b, i: (b * nb + jnp.clip(i + off, 0, nb - 1), col))
    bias_spec = pl.BlockSpec((1, HEADS_AB, tq, 3 * tq),
                             lambda b, i: (jnp.where(i == 0, 0, jnp.where(i == nb - 1, 2, 1)), 0, 0, 0))
    return pl.pallas_call(
        _attn_a_kernel,
        grid=(batch, nb),
        in_specs=[q_spec, kv_spec(1, -1), kv_spec(1, 0), kv_spec(1, 1),
                  kv_spec(2, -1), kv_spec(2, 0), kv_spec(2, 1), bias_spec],
        out_specs=pl.BlockSpec((tq, D_MODEL), lambda b, i: (b * nb + i, 0)),
        out_shape=jax.ShapeDtypeStruct((batch * seq, D_MODEL), BF16),
        compiler_params=_cparams(("parallel", "arbitrary")),
        name="attn_neighbourhood",
    )(qkv, qkv, qkv, qkv, qkv, qkv, qkv, bias)


def _attn_b_kernel(q_ref, kp_ref, kc_ref, kn_ref, vp_ref, vc_ref, vn_ref, mask_ref, o_ref, lse_ref):
    tq = q_ref.shape[0]
    masks = _pair_masks()
    lane = lax.broadcasted_iota(jnp.int32, (1, LANES), 1)
    lse_tiles = [jnp.zeros((B_HALF, LANES), F32) for _ in range(tq // B_HALF)]
    units = [(p, hf) for p in range(HEADS_AB // 2) for hf in range(tq // B_HALF)]

    def window(prev_ref, cur_ref, next_ref, hf, sl):
        lo, hi = hf * B_HALF - B_SIDE, (hf + 1) * B_HALF + B_SIDE
        parts = []
        if lo < 0:
            parts.append(prev_ref[tq + lo:, sl])
        parts.append(cur_ref[max(lo, 0):min(hi, tq), sl])
        if hi > tq:
            parts.append(next_ref[:hi - tq, sl])
        return jnp.concatenate(parts, axis=0)

    def scores(unit):
        p, hf = unit
        sl = slice(p * LANES, (p + 1) * LANES)
        qp = q_ref[hf * B_HALF:(hf + 1) * B_HALF, sl]
        stacked = jnp.concatenate([jnp.where(mk, qp, jnp.zeros_like(qp)) for mk in masks], axis=0)
        s = _dot_nt(stacked, window(kp_ref, kc_ref, kn_ref, hf, sl)) + mask_ref[0, hf]
        return s, s.max(axis=-1, keepdims=True)

    def finish(unit, scored):
        s, m = scored
        p, hf = unit
        sl = slice(p * LANES, (p + 1) * LANES)
        pr = jnp.exp2(s - m)
        l = pr.sum(axis=-1, keepdims=True)
        o = jnp.dot(pr.astype(BF16), window(vp_ref, vc_ref, vn_ref, hf, sl), preferred_element_type=F32) / l
        o_ref[hf * B_HALF:(hf + 1) * B_HALF, sl] = jnp.where(masks[0], o[:B_HALF], o[B_HALF:]).astype(BF16)
        lse = m + jnp.log2(l)
        tile = jnp.where(lane == 2 * p, lse[:B_HALF], lse_tiles[hf])
        lse_tiles[hf] = jnp.where(lane == 2 * p + 1, lse[B_HALF:], tile)

    _pipelined(units, scores, finish)
    for hf, tile in enumerate(lse_tiles):
        lse_ref[hf * B_HALF:(hf + 1) * B_HALF, :] = tile


def _attn_b_masks():
    qq = np.arange(B_HALF)[:, None]
    jj = np.arange(B_HALF + 2 * B_SIDE)[None, :]
    band = (jj - qq >= 0) & (jj - qq <= 2 * B_SIDE)
    out = []
    for ty in range(4):
        per_half = []
        for hf in range(B_TQ // B_HALF):
            pos = hf * B_HALF - B_SIDE + jj
            ok = band
            if ty & 1:
                ok = ok & (pos >= 0)
            if ty & 2:
                ok = ok & (pos < B_TQ)
            half = np.where(ok, 0.0, MASK_NEG)
            per_half.append(np.concatenate([half, half], axis=0))
        out.append(np.stack(per_half))
    return np.stack(out).astype(np.float32)


def _attn_b_group(qkv, masks, *, group):
    batch, dil, sub, _ = qkv.shape
    tq = B_TQ
    nb = sub // tq
    assert sub % tq == 0
    q_spec = pl.BlockSpec((None, None, tq, D_MODEL), lambda b, r, i: (b, r, i, 0))
    def kv_spec(which, off):
        return pl.BlockSpec((None, None, tq, D_MODEL),
                            lambda b, r, i: (b, r, jnp.clip(i + off, 0, nb - 1), which))
    mask_spec = pl.BlockSpec((1,) + masks.shape[1:],
                             lambda b, r, i: ((i == 0).astype(jnp.int32) + 2 * (i == nb - 1).astype(jnp.int32),
                                              0, 0, 0))
    return pl.pallas_call(
        _attn_b_kernel,
        grid=(batch, dil, nb),
        in_specs=[q_spec, kv_spec(1, -1), kv_spec(1, 0), kv_spec(1, 1),
                  kv_spec(2, -1), kv_spec(2, 0), kv_spec(2, 1), mask_spec],
        out_specs=[pl.BlockSpec((None, None, tq, D_MODEL), lambda b, r, i: (b, r, i, 0)),
                   pl.BlockSpec((None, None, tq, LANES), lambda b, r, i: (b, r, i, 0))],
        out_shape=[jax.ShapeDtypeStruct((batch, dil, sub, D_MODEL), BF16),
                   jax.ShapeDtypeStruct((batch, dil, sub, LANES), F32)],
        compiler_params=_cparams(("parallel", "parallel", "arbitrary")),
        name="attn_dilated_g%d" % group,
    )(qkv, qkv, qkv, qkv, qkv, qkv, qkv, masks)


def _attn_c_kernel(q_ref, k_ref, vt_ref, o_ref, s_ref, smax_ref, m_ref, l_ref, acc_ref, *, tk):
    nk = k_ref.shape[0] // tk
    m_ref[...] = jnp.full(m_ref.shape, MASK_NEG, F32)
    l_ref[...] = jnp.zeros(l_ref.shape, F32)
    acc_ref[...] = jnp.zeros(acc_ref.shape, F32)

    def issue_scores(h, c, slot):
        start = pl.multiple_of(c * tk, tk)
        s = _dot_nt(k_ref[pl.ds(start, tk), :], q_ref[:, h * C_HEAD_DIM:(h + 1) * C_HEAD_DIM])
        s_ref[slot] = s
        smax_ref[slot] = s.max(axis=0, keepdims=True)

    issue_scores(0, 0, 0)

    def chunk(c, carry):
        vt = vt_ref[c]
        for h in range(C_GROUP):
            if h + 1 < C_GROUP:
                issue_scores(h + 1, c, (h + 1) % 2)
            else:
                issue_scores(0, jnp.minimum(c + 1, nk - 1), (h + 1) % 2)
            s = s_ref[h % 2]
            m_prev = m_ref[h]
            m_new = jnp.maximum(m_prev, smax_ref[h % 2])
            alpha = jnp.exp2(m_prev - m_new)
            pr = jnp.exp2(s - m_new)
            l_ref[h] = alpha * l_ref[h] + pr.sum(axis=0, keepdims=True)
            acc_ref[h] = alpha * acc_ref[h] + jnp.dot(vt, pr.astype(BF16), preferred_element_type=F32)
            m_ref[h] = m_new
        return carry

    lax.fori_loop(0, nk, chunk, 0)
    for h in range(C_GROUP):
        o_ref[:, h * C_HEAD_DIM:(h + 1) * C_HEAD_DIM] = (acc_ref[h] / l_ref[h]).T.astype(BF16)


def _attn_c(qk, vt, *, batch, seq):
    tq, tk = C_TQ, C_TK
    nq = seq // tq
    assert seq % tk == 0 and C_GROUP % 2 == 0
    qw = C_GROUP * C_HEAD_DIM
    return pl.pallas_call(
        functools.partial(_attn_c_kernel, tk=tk),
        grid=(batch, C_KV_HEADS, nq),
        in_specs=[
            pl.BlockSpec((tq, qw), lambda b, g, i: (b * nq + i, g)),
            pl.BlockSpec((seq, C_HEAD_DIM), lambda b, g, i: (b, C_Q_HEADS + g)),
            pl.BlockSpec((None, seq // tk, C_HEAD_DIM, tk), lambda b, g, i: (b, 0, g, 0)),
        ],
        out_specs=pl.BlockSpec((tq, qw), lambda b, g, i: (b * nq + i, g)),
        out_shape=jax.ShapeDtypeStruct((batch * seq, C_Q_HEADS * C_HEAD_DIM), BF16),
        scratch_shapes=[pltpu.VMEM((2, tk, tq), F32), pltpu.VMEM((2, 1, tq), F32),
                        pltpu.VMEM((C_GROUP, 1, tq), F32), pltpu.VMEM((C_GROUP, 1, tq), F32),
                        pltpu.VMEM((C_GROUP, C_HEAD_DIM, tq), F32)],
        compiler_params=_cparams(("parallel", "parallel", "arbitrary")),
        name="attn_gqa_flash",
    )(qk, qk, vt)


def _post_kernel(*refs, merge, final, halves, tf):
    refs = list(refs)
    x_ref = refs.pop(0)
    if merge:
        o_refs = [refs.pop(0) for _ in range(3)]
        lse_refs = [refs.pop(0) for _ in range(3)]
        expand_ref = refs.pop(0)
    else:
        o_ref_in = refs.pop(0)
    mod_ref, g_ref, wo_ref, w1_ref, w2_ref = [refs.pop(0) for _ in range(5)]
    fg_ref = refs.pop(0) if final else None
    out_ref = refs.pop(0)
    mod = mod_ref[0]
    hm = x_ref.shape[0] // halves

    def token_order(ref, src):
        dil, _, width = ref.shape
        rows = src.stop - src.start
        if dil == 1:
            return ref[0, src, :].astype(F32)
        cols = []
        for c in range(width // LANES):
            part = ref[:, src, c * LANES:(c + 1) * LANES].astype(F32)
            cols.append(jnp.swapaxes(part, 0, 1).reshape(rows * dil, LANES))
        return cols[0] if len(cols) == 1 else jnp.concatenate(cols, axis=1)

    def merged_groups(hf):
        o_tok, lse = [], []
        for o_g, lse_g in zip(o_refs, lse_refs):
            rows = hm // o_g.shape[0]
            src = slice(hf * rows, (hf + 1) * rows)
            o_tok.append(token_order(o_g, src))
            lse.append(token_order(lse_g, src))
        top = jnp.maximum(jnp.maximum(lse[0], lse[1]), lse[2])
        ex = [jnp.exp2(v - top) for v in lse]
        den = ex[0] + ex[1] + ex[2]
        mixed = None
        for g in range(3):
            wgt = ex[g] / den
            hi = wgt.astype(BF16)
            lo = (wgt - hi.astype(F32)).astype(BF16)
            wide = (jnp.dot(hi, expand_ref[...], preferred_element_type=F32)
                    + jnp.dot(lo, expand_ref[...], preferred_element_type=F32))
            mixed = wide * o_tok[g] if mixed is None else mixed + wide * o_tok[g]
        return mixed.astype(BF16)

    def pre(hf):
        rows = slice(hf * hm, (hf + 1) * hm)
        o = merged_groups(hf) if merge else o_ref_in[rows, :]
        mix = jnp.dot(o, wo_ref[...], preferred_element_type=F32)
        x1 = x_ref[rows, :] + mod[:, 2 * D_MODEL:3 * D_MODEL] * mix
        h2 = _norm_mod(x1, g_ref[0], mod[:, 4 * D_MODEL:5 * D_MODEL], mod[:, 3 * D_MODEL:4 * D_MODEL])
        return x1, h2.astype(BF16)

    def mlp(h2):
        acc = None
        for c in range(D_FF // tf):
            a = jnp.dot(h2, w1_ref[:, c * tf:(c + 1) * tf], preferred_element_type=F32)
            a = jnp.square(jnp.maximum(a, 0.0)).astype(BF16)
            d = jnp.dot(a, w2_ref[c * tf:(c + 1) * tf, :], preferred_element_type=F32)
            acc = d if acc is None else acc + d
        return acc

    pres = [pre(hf) for hf in range(halves)]
    for hf, (x1, h2) in enumerate(pres):
        x2 = x1 + mod[:, 5 * D_MODEL:6 * D_MODEL] * mlp(h2)
        if final:
            ms = jnp.mean(x2 * x2, axis=-1, keepdims=True)
            x2 = x2 * lax.rsqrt(ms + EPS) * fg_ref[...]
        out_ref[hf * hm:(hf + 1) * hm, :] = x2


def _head_expand_matrix():
    e = np.zeros((LANES, D_MODEL), np.float32)
    for h in range(HEADS_AB):
        e[h, h * HEAD_DIM_AB:(h + 1) * HEAD_DIM_AB] = 1.0
    return e


def _post(x2d, attn, mods3, norm_g3, wo, w1, w2, final_g, *, layer, mod_row0, seq, merge, final):
    t = x2d.shape[0]
    tm = TM_POST_MERGE if merge else TM_POST
    hm = tm // POST_HALVES
    per_seq = seq // tm
    row = lambda i: (i, 0)
    const = lambda i: (0, 0)
    resident = pl.Buffered(1)
    in_specs = [pl.BlockSpec((tm, D_MODEL), row)]
    args = [x2d]
    scratch = []
    if merge:
        outs, lses = attn
        def res_spec(arr):
            dil, width = arr.shape[1], arr.shape[3]
            assert hm % (dil * 16) == 0
            return pl.BlockSpec((None, dil, tm // dil, width), lambda i: (i // per_seq, 0, i % per_seq, 0))
        in_specs += [res_spec(a) for a in outs] + [res_spec(a) for a in lses]
        in_specs += [pl.BlockSpec((LANES, D_MODEL), const, pipeline_mode=resident)]
        args += list(outs) + list(lses) + [jnp.asarray(_head_expand_matrix(), BF16)]
    else:
        in_specs += [pl.BlockSpec((tm, D_MODEL), row)]
        args += [attn]
    in_specs += [
        pl.BlockSpec((1, 1, 6 * D_MODEL), lambda i: (layer * MOD_ROWS + mod_row0 + i // per_seq, 0, 0)),
        pl.BlockSpec((1, 1, D_MODEL), lambda i: (2 * layer + 1, 0, 0)),
        pl.BlockSpec((D_MODEL, D_MODEL), const, pipeline_mode=resident),
        pl.BlockSpec((D_MODEL, D_FF), const, pipeline_mode=resident),
        pl.BlockSpec((D_FF, D_MODEL), const, pipeline_mode=resident),
    ]
    args += [mods3, norm_g3, wo, w1, w2]
    if final:
        in_specs += [pl.BlockSpec((1, D_MODEL), const)]
        args += [final_g.reshape(1, D_MODEL)]
    return pl.pallas_call(
        functools.partial(_post_kernel, merge=merge, final=final, halves=POST_HALVES, tf=TF_POST),
        grid=(t // tm,),
        in_specs=in_specs,
        out_specs=pl.BlockSpec((tm, D_MODEL), row),
        out_shape=jax.ShapeDtypeStruct((t, D_MODEL), F32),
        scratch_shapes=scratch,
        compiler_params=_cparams(("parallel",)),
        name="wo_mlp_merge" if merge else "wo_mlp",
    )(*args)


def _trunk(x, mods3, mod_row0, norm_g3, final_g, wts, a_bias, b_masks, tabs_b, tabs_c):
    batch, seq, _ = x.shape
    assert seq <= tabs_b[0].shape[0]
    x2d = x.reshape(batch * seq, D_MODEL)
    for layer in range(DEPTH):
        kind, j = layer % N_MIXERS, layer // N_MIXERS
        common = dict(layer=layer, mod_row0=mod_row0, seq=seq)
        if kind == 0:
            qkv = _qkv_proj(x2d, mods3, norm_g3, wts["a_qkv"][j], mode="a", **common)
            attn = _attn_a(qkv, a_bias[j], batch=batch, seq=seq)
            wo = wts["a_o"][j]
        elif kind == 1:
            groups = []
            for g, (win, dil) in enumerate(B_PAIRS):
                assert win == 2 * B_SIDE * dil
                qkv = _qkv_proj(x2d, mods3, norm_g3, wts["b_qkv"][j], mode="b", cos=tabs_b[0], sin=tabs_b[1],
                                group=g, dil=dil, **common)
                groups.append(_attn_b_group(qkv, b_masks, group=g))
            attn = ([o for o, _ in groups], [l for _, l in groups])
            wo = wts["b_o"][j]
        else:
            qk, vt = _qkv_proj(x2d, mods3, norm_g3, wts["c_qkv"][j], mode="c", cos=tabs_c[0], sin=tabs_c[1],
                               qg=wts["c_qg"][j], kg=wts["c_kg"][j], **common)
            attn = _attn_c(qk, vt, batch=batch, seq=seq)
            wo = wts["c_o"][j]
        x2d = _post(x2d, attn, mods3, norm_g3, wo, wts["w1"][layer], wts["w2"][layer], final_g,
                    merge=(kind == 1), final=(layer == DEPTH - 1), **common)
    return x2d.reshape(batch, seq, D_MODEL)


def kernel(x_prompt, x_sample, c_prompt, c_sample, w_mod, b_mod, norm_g, final_g, a_w_qkv, a_rpb, a_w_o,
           b_w_qkv, b_w_o, c_w_qkv, c_q_g, c_k_g, c_w_o, mlp_w1, mlp_w2):
    nb_p, nb_s = c_prompt.shape[0], c_sample.shape[0]
    assert nb_p + nb_s <= MOD_ROWS
    c_all = jnp.concatenate([c_prompt, c_sample, jnp.zeros((MOD_ROWS - nb_p - nb_s, D_MODEL), F32)], axis=0)
    mods3 = _modulation(c_all, w_mod, b_mod).reshape(DEPTH * MOD_ROWS, 1, 6 * D_MODEL)
    norm_g3 = norm_g.reshape(DEPTH * 2, 1, D_MODEL)

    scale_ab = HEAD_DIM_AB ** -0.5 * LOG2E
    col_scale = np.ones((3 * D_MODEL,), np.float32)
    col_scale[:D_MODEL] = scale_ab
    a_qkv = (a_w_qkv * col_scale).astype(BF16)
    b_qkv = (b_w_qkv * np.tile(col_scale, len(B_PAIRS))).astype(BF16)
    wts = {
        "a_qkv": a_qkv, "a_o": a_w_o.astype(BF16),
        "b_qkv": b_qkv, "b_o": b_w_o.astype(BF16),
        "c_qkv": c_w_qkv.astype(BF16), "c_o": c_w_o.astype(BF16),
        "c_qg": (c_q_g * (C_HEAD_DIM ** -0.5 * LOG2E)).reshape(-1, 1, C_HEAD_DIM),
        "c_kg": c_k_g.reshape(-1, 1, C_HEAD_DIM),
        "w1": mlp_w1.astype(BF16), "w2": mlp_w2.astype(BF16),
    }
    a_bias = [_attn_a_bias(a_rpb[j]) for j in range(a_rpb.shape[0])]
    b_masks = jnp.asarray(_attn_b_masks())

    t = jnp.arange(max(x_prompt.shape[1], x_sample.shape[1]))
    tabs_b = _rope_tables(t, t)
    tabs_c = _rope_tables(t // GRID_W, t % GRID_W)
    shared = (norm_g3, final_g, wts, a_bias, b_masks, tabs_b, tabs_c)
    y_prompt = _trunk(x_prompt, mods3, 0, *shared)
    y_sample = _trunk(x_sample, mods3, nb_p, *shared)
    return (y_prompt, y_sample)
```

```python
import functools

import numpy as np
import jax
import jax.numpy as jnp
from jax import lax
from jax.experimental import pallas as pl
from jax.experimental.pallas import tpu as pltpu

F32 = jnp.float32
BF16 = jnp.bfloat16

D_MODEL = 1024
DEPTH = 4
GRID_W = 64
D_FF = 4 * D_MODEL
EPS = 1e-6
ROPE_THETA = 10000.0
N_MIXERS = 3
HEAD_DIM_AB = 64
HEADS_AB = D_MODEL // HEAD_DIM_AB
A_WIN_ROWS = 8
A_WIN_COLS = 16
B_PAIRS = ((128, 1), (512, 4), (2048, 16))
B_SIDE = 64
C_HEAD_DIM = 128
C_Q_HEADS = 8
C_KV_HEADS = 2
C_GROUP = C_Q_HEADS // C_KV_HEADS
C_QKV_WIDTH = (C_Q_HEADS + 2 * C_KV_HEADS) * C_HEAD_DIM
ROPE_HALF = 32
LANES = 128
MOD_ROWS = 8
MASK_NEG = -1e30
LOG2E = float(np.log2(np.e))
VMEM_LIMIT = 56 * 1024 * 1024

TM_QKV = 1024
TN_QKV = 256
TN_QKV_C = 512
QKV_HALVES = 2
TM_POST = 1024
TM_POST_MERGE = 512
POST_HALVES = 2
TF_POST = 1024
A_TQ = 4 * GRID_W
A_AHEAD = 3
B_AHEAD = 1
B_TQ = 256
B_HALF = 128
C_TQ = 512
C_TK = 1024


def _cparams(sem):
    return pltpu.CompilerParams(dimension_semantics=sem, vmem_limit_bytes=VMEM_LIMIT)


def _mod_kernel(c_ref, w_ref, b_ref, o_ref):
    c = c_ref[...]
    act = (c / (1.0 + jnp.exp(-c))).astype(BF16)
    o_ref[0] = jnp.dot(act, w_ref[0].astype(BF16), preferred_element_type=F32) + b_ref[0]


def _modulation(c_all, w_mod, b_mod):
    tn = 1536
    n = 6 * D_MODEL
    return pl.pallas_call(
        _mod_kernel,
        grid=(DEPTH, n // tn),
        in_specs=[
            pl.BlockSpec((MOD_ROWS, D_MODEL), lambda l, j: (0, 0)),
            pl.BlockSpec((1, D_MODEL, tn), lambda l, j: (l, 0, j)),
            pl.BlockSpec((1, 1, tn), lambda l, j: (l, 0, j)),
        ],
        out_specs=pl.BlockSpec((1, MOD_ROWS, tn), lambda l, j: (l, 0, j)),
        out_shape=jax.ShapeDtypeStruct((DEPTH, MOD_ROWS, n), F32),
        compiler_params=_cparams(("parallel", "parallel")),
        name="adaln_modulation",
    )(c_all, w_mod, b_mod.reshape(DEPTH, 1, n))


def _norm_mod(x, gain, scale, shift):
    ms = jnp.mean(x * x, axis=-1, keepdims=True)
    return (x * lax.rsqrt(ms + EPS) * gain) * (1.0 + scale) + shift


def _rope128(x, cos, sin_signed, low_half):
    up = pltpu.roll(x, LANES - ROPE_HALF, 1)
    down = pltpu.roll(x, ROPE_HALF, 1)
    return x * cos + jnp.where(low_half, up, down) * sin_signed


def _low_half_mask():
    lane = lax.broadcasted_iota(jnp.int32, (1, LANES), 1)
    return (lane % (2 * ROPE_HALF)) < ROPE_HALF


def _rope_tables(pos_a, pos_b):
    inv = ROPE_THETA ** (-jnp.arange(ROPE_HALF, dtype=F32) / ROPE_HALF)
    def one(pos):
        ang = pos.astype(F32)[:, None] * inv[None, :]
        c, s = jnp.cos(ang), jnp.sin(ang)
        return jnp.concatenate([c, c], axis=1), jnp.concatenate([-s, s], axis=1)
    ca, sa = one(pos_a)
    cb, sb = one(pos_b)
    return jnp.concatenate([ca, cb], axis=1), jnp.concatenate([sa, sb], axis=1)


def _qkv_kernel(*refs, mode, dil, halves, tn):
    if mode == "a":
        x_ref, mod_ref, g_ref, w_ref, o_ref = refs
    elif mode == "b":
        x_ref, mod_ref, g_ref, w_ref, cos_ref, sin_ref, o_ref = refs
    else:
        x_ref, mod_ref, g_ref, w_ref, cos_ref, sin_ref, qg_ref, kg_ref, o_ref, vt_ref = refs
    mod = mod_ref[0]
    hm = x_ref.shape[0] // halves
    n = w_ref.shape[1]
    low = _low_half_mask()

    def normed(hf):
        rows = slice(hf * hm, (hf + 1) * hm)
        return _norm_mod(x_ref[rows, :], g_ref[0], mod[:, D_MODEL:2 * D_MODEL], mod[:, 0:D_MODEL]).astype(BF16)

    def emit(hf, j, acc):
        rows = slice(hf * hm, (hf + 1) * hm)
        if mode == "a":
            o_ref[rows, j * tn:(j + 1) * tn] = acc.astype(BF16)
            return
        cos, sin = cos_ref[rows, :], sin_ref[rows, :]
        for c in range(tn // LANES):
            col = j * tn + c * LANES
            val = acc[:, c * LANES:(c + 1) * LANES]
            if mode == "c":
                head = col // C_HEAD_DIM
                if head < C_Q_HEADS + C_KV_HEADS:
                    gain = qg_ref[...] if head < C_Q_HEADS else kg_ref[...]
                    ms = jnp.mean(val * val, axis=-1, keepdims=True)
                    val = _rope128(val * lax.rsqrt(ms + EPS) * gain, cos, sin, low)
                    o_ref[rows, col:col + LANES] = val.astype(BF16)
                else:
                    kv = head - C_Q_HEADS - C_KV_HEADS
                    vt_ref[kv * C_HEAD_DIM:(kv + 1) * C_HEAD_DIM, rows] = val.T.astype(BF16)
                continue
            if col < 2 * D_MODEL:
                val = _rope128(val, cos, sin, low)
            sub = hm // dil
            dst = slice(hf * sub, (hf + 1) * sub)
            if dil == 1:
                o_ref[0, 0, dst, col:col + LANES] = val.astype(BF16)
                continue
            by_residue = jnp.swapaxes(val.reshape(sub, dil, LANES), 0, 1)
            o_ref[0, :, dst, col:col + LANES] = by_residue.astype(BF16)

    hs = [normed(hf) for hf in range(halves)]
    for hf in range(halves):
        for j in range(n // tn):
            emit(hf, j, jnp.dot(hs[hf], w_ref[:, j * tn:(j + 1) * tn], preferred_element_type=F32))


def _qkv_proj(x2d, mods3, norm_g3, w, *, layer, mod_row0, seq, mode, cos=None, sin=None, qg=None, kg=None,
              group=0, dil=1):
    t = x2d.shape[0]
    tm = TM_QKV
    if mode == "b":
        n, col0 = 3 * D_MODEL, group
    else:
        n, col0 = w.shape[1], 0
    tn = TN_QKV_C if mode == "c" else TN_QKV
    hm = tm // QKV_HALVES
    per_seq = seq // tm
    in_specs = [
        pl.BlockSpec((tm, D_MODEL), lambda i: (i, 0)),
        pl.BlockSpec((1, 1, 6 * D_MODEL), lambda i: (layer * MOD_ROWS + mod_row0 + i // per_seq, 0, 0)),
        pl.BlockSpec((1, 1, D_MODEL), lambda i: (2 * layer, 0, 0)),
        pl.BlockSpec((D_MODEL, n), lambda i: (0, col0), pipeline_mode=pl.Buffered(1)),
    ]
    args = [x2d, mods3, norm_g3, w]
    if mode in ("b", "c"):
        in_specs += [pl.BlockSpec((tm, LANES), lambda i: (i % per_seq, 0))] * 2
        args += [cos, sin]
    if mode == "c":
        in_specs += [pl.BlockSpec((1, C_HEAD_DIM), lambda i: (0, 0))] * 2
        args += [qg, kg]
    scratch = []
    if mode == "b":
        assert hm % (dil * 16) == 0
        batch = t // seq
        out_spec = pl.BlockSpec((1, dil, tm // dil, n), lambda i: (i // per_seq, 0, i % per_seq, 0))
        out_shape = jax.ShapeDtypeStruct((batch, dil, seq // dil, n), BF16)
    elif mode == "c":
        assert tm == C_TK
        n_qk = (C_Q_HEADS + C_KV_HEADS) * C_HEAD_DIM
        kv_w = C_KV_HEADS * C_HEAD_DIM
        out_spec = [pl.BlockSpec((tm, n_qk), lambda i: (i, 0)),
                    pl.BlockSpec((None, None, kv_w, tm), lambda i: (i // per_seq, i % per_seq, 0, 0))]
        out_shape = [jax.ShapeDtypeStruct((t, n_qk), BF16),
                     jax.ShapeDtypeStruct((t // seq, per_seq, kv_w, tm), BF16)]
    else:
        out_spec = pl.BlockSpec((tm, n), lambda i: (i, 0))
        out_shape = jax.ShapeDtypeStruct((t, n), BF16)
    return pl.pallas_call(
        functools.partial(_qkv_kernel, mode=mode, dil=dil, halves=QKV_HALVES, tn=tn),
        grid=(t // tm,),
        in_specs=in_specs,
        out_specs=out_spec,
        out_shape=out_shape,
        scratch_shapes=scratch,
        compiler_params=_cparams(("parallel",)),
        name="norm_qkv_" + mode,
    )(*args)


def _pair_masks():
    lane = lax.broadcasted_iota(jnp.int32, (1, LANES), 1)
    first = lane < HEAD_DIM_AB
    return first, jnp.logical_not(first)


def _dot_nt(a, b):
    return lax.dot_general(a, b, (((1,), (1,)), ((), ())), preferred_element_type=F32)


def _pipelined(units, scores, finish, ahead=1):
    pending = [scores(u) for u in units[:ahead]]
    for i, unit in enumerate(units):
        if i + ahead < len(units):
            pending.append(scores(units[i + ahead]))
        finish(unit, pending.pop(0))


def _attn_a_kernel(q_ref, kp_ref, kc_ref, kn_ref, vp_ref, vc_ref, vn_ref, bias_ref, o_ref, kbuf, vbuf):
    tq = q_ref.shape[0]
    rows_per = tq // GRID_W
    win_keys = A_WIN_ROWS * GRID_W
    first = pl.program_id(1) == 0
    last = pl.program_id(1) == pl.num_programs(1) - 1
    for n, (k_r, v_r) in enumerate(((kp_ref, vp_ref), (kc_ref, vc_ref), (kn_ref, vn_ref))):
        kbuf[n * tq:(n + 1) * tq, :] = k_r[...]
        vbuf[n * tq:(n + 1) * tq, :] = v_r[...]
    masks = _pair_masks()
    units = [(p, rl) for p in range(HEADS_AB // 2) for rl in range(rows_per)]

    def window_of(rl):
        start = jnp.where(first, rows_per, jnp.where(last, 2 * rows_per - A_WIN_ROWS,
                                                     rl + rows_per - A_WIN_ROWS // 2))
        return pl.multiple_of(start * GRID_W, GRID_W), start - rows_per - rl + A_WIN_ROWS - 1

    def scores(unit):
        p, rl = unit
        sl = slice(p * LANES, (p + 1) * LANES)
        start, win = window_of(rl)
        qp = q_ref[rl * GRID_W:(rl + 1) * GRID_W, sl]
        stacked = jnp.concatenate([jnp.where(mk, qp, jnp.zeros_like(qp)) for mk in masks], axis=0)
        s = _dot_nt(stacked, kbuf[pl.ds(start, win_keys), sl]) + bias_ref[p, win]
        return s, s.max(axis=-1, keepdims=True)

    def finish(unit, scored):
        s, m = scored
        p, rl = unit
        sl = slice(p * LANES, (p + 1) * LANES)
        start, _ = window_of(rl)
        pr = jnp.exp2(s - m)
        l = pr.sum(axis=-1, keepdims=True)
        o = jnp.dot(pr.astype(BF16), vbuf[pl.ds(start, win_keys), sl], preferred_element_type=F32) / l
        o_ref[rl * GRID_W:(rl + 1) * GRID_W, sl] = jnp.where(masks[0], o[:GRID_W], o[GRID_W:]).astype(BF16)

    _pipelined(units, scores, finish, ahead=A_AHEAD)


def _attn_a_bias(rpb):
    n_dc = 2 * A_WIN_COLS - 1
    c, kc = np.arange(GRID_W)[:, None], np.arange(GRID_W)[None, :]
    onehot_c = ((kc - c + A_WIN_COLS - 1)[..., None] == np.arange(n_dc)).astype(np.float32)
    toe = jnp.einsum("hde,cje->hdcj", rpb * LOG2E, onehot_c, precision=lax.Precision.HIGHEST)
    cs = np.clip(c - A_WIN_COLS // 2, 0, GRID_W - A_WIN_COLS)
    col_ok = (kc >= cs) & (kc < cs + A_WIN_COLS)
    toe = jnp.where(jnp.asarray(col_ok)[None, None], toe, MASK_NEG)
    wins = jnp.stack([jnp.concatenate([toe[:, w + m] for m in range(A_WIN_ROWS)], axis=-1)
                      for w in range(A_WIN_ROWS)], axis=1)
    wins = wins.reshape(HEADS_AB // 2, 2, A_WIN_ROWS, GRID_W, A_WIN_ROWS * GRID_W)
    return jnp.transpose(wins, (0, 2, 1, 3, 4)).reshape(HEADS_AB // 2, A_WIN_ROWS, 2 * GRID_W, A_WIN_ROWS * GRID_W)


def _attn_a(qkv, bias, *, batch, seq):
    tq = A_TQ
    nb = seq // tq
    rows_per = tq // GRID_W
    assert nb >= 3 and A_WIN_ROWS // 2 <= rows_per <= A_WIN_ROWS
    q_spec = pl.BlockSpec((tq, D_MODEL), lambda b, i: (b * nb + i, 0))
    def kv_spec(col, off):
        return pl.BlockSpec((tq, D_MODEL), lambda b, i: (b * nb + jnp.clip(i + off, 0, nb - 1), col))
    bias_spec = pl.BlockSpec(bias.shape, lambda b, i: (0, 0, 0, 0), pipeline_mode=pl.Buffered(1))
    return pl.pallas_call(
        _attn_a_kernel,
        grid=(batch, nb),
        in_specs=[q_spec, kv_spec(1, -1), kv_spec(1, 0), kv_spec(1, 1),
                  kv_spec(2, -1), kv_spec(2, 0), kv_spec(2, 1), bias_spec],
        out_specs=pl.BlockSpec((tq, D_MODEL), lambda b, i: (b * nb + i, 0)),
        out_shape=jax.ShapeDtypeStruct((batch * seq, D_MODEL), BF16),
        scratch_shapes=[pltpu.VMEM((3 * tq, D_MODEL), BF16), pltpu.VMEM((3 * tq, D_MODEL), BF16)],
        compiler_params=_cparams(("parallel", "arbitrary")),
        name="attn_neighbourhood",
    )(qkv, qkv, qkv, qkv, qkv, qkv, qkv, bias)


def _attn_b_kernel(q_ref, kp_ref, kc_ref, kn_ref, vp_ref, vc_ref, vn_ref, mask_ref, o_ref, lse_ref):
    tq = q_ref.shape[0]
    masks = _pair_masks()
    lane = lax.broadcasted_iota(jnp.int32, (1, LANES), 1)
    lse_tiles = [jnp.zeros((B_HALF, LANES), F32) for _ in range(tq // B_HALF)]
    units = [(p, hf) for p in range(HEADS_AB // 2) for hf in range(tq // B_HALF)]

    def window(prev_ref, cur_ref, next_ref, hf, sl):
        lo, hi = hf * B_HALF - B_SIDE, (hf + 1) * B_HALF + B_SIDE
        parts = []
        if lo < 0:
            parts.append(prev_ref[tq + lo:, sl])
        parts.append(cur_ref[max(lo, 0):min(hi, tq), sl])
        if hi > tq:
            parts.append(next_ref[:hi - tq, sl])
        return jnp.concatenate(parts, axis=0)

    def scores(unit):
        p, hf = unit
        sl = slice(p * LANES, (p + 1) * LANES)
        qp = q_ref[hf * B_HALF:(hf + 1) * B_HALF, sl]
        stacked = jnp.concatenate([jnp.where(mk, qp, jnp.zeros_like(qp)) for mk in masks], axis=0)
        s = _dot_nt(stacked, window(kp_ref, kc_ref, kn_ref, hf, sl)) + mask_ref[0, hf]
        return s, s.max(axis=-1, keepdims=True)

    def finish(unit, scored):
        s, m = scored
        p, hf = unit
        sl = slice(p * LANES, (p + 1) * LANES)
        pr = jnp.exp2(s - m)
        l = pr.sum(axis=-1, keepdims=True)
        o = jnp.dot(pr.astype(BF16), window(vp_ref, vc_ref, vn_ref, hf, sl), preferred_element_type=F32) / l
        o_ref[hf * B_HALF:(hf + 1) * B_HALF, sl] = jnp.where(masks[0], o[:B_HALF], o[B_HALF:]).astype(BF16)
        lse = m + jnp.log2(l)
        tile = jnp.where(lane == 2 * p, lse[:B_HALF], lse_tiles[hf])
        lse_tiles[hf] = jnp.where(lane == 2 * p + 1, lse[B_HALF:], tile)

    _pipelined(units, scores, finish, ahead=B_AHEAD)
    for hf, tile in enumerate(lse_tiles):
        lse_ref[hf * B_HALF:(hf + 1) * B_HALF, :] = tile


def _attn_b_masks():
    qq = np.arange(B_HALF)[:, None]
    jj = np.arange(B_HALF + 2 * B_SIDE)[None, :]
    band = (jj - qq >= 0) & (jj - qq <= 2 * B_SIDE)
    out = []
    for ty in range(4):
        per_half = []
        for hf in range(B_TQ // B_HALF):
            pos = hf * B_HALF - B_SIDE + jj
            ok = band
            if ty & 1:
                ok = ok & (pos >= 0)
            if ty & 2:
                ok = ok & (pos < B_TQ)
            half = np.where(ok, 0.0, MASK_NEG)
            per_half.append(np.concatenate([half, half], axis=0))
        out.append(np.stack(per_half))
    return np.stack(out).astype(np.float32)


def _attn_b_group(qkv, masks, *, group):
    batch, dil, sub, _ = qkv.shape
    tq = B_TQ
    nb = sub // tq
    assert sub % tq == 0
    q_spec = pl.BlockSpec((None, None, tq, D_MODEL), lambda b, r, i: (b, r, i, 0))
    def kv_spec(which, off):
        return pl.BlockSpec((None, None, tq, D_MODEL),
                            lambda b, r, i: (b, r, jnp.clip(i + off, 0, nb - 1), which))
    mask_spec = pl.BlockSpec((1,) + masks.shape[1:],
                             lambda b, r, i: ((i == 0).astype(jnp.int32) + 2 * (i == nb - 1).astype(jnp.int32),
                                              0, 0, 0))
    return pl.pallas_call(
        _attn_b_kernel,
        grid=(batch, dil, nb),
        in_specs=[q_spec, kv_spec(1, -1), kv_spec(1, 0), kv_spec(1, 1),
                  kv_spec(2, -1), kv_spec(2, 0), kv_spec(2, 1), mask_spec],
        out_specs=[pl.BlockSpec((None, None, tq, D_MODEL), lambda b, r, i: (b, r, i, 0)),
                   pl.BlockSpec((None, None, tq, LANES), lambda b, r, i: (b, r, i, 0))],
        out_shape=[jax.ShapeDtypeStruct((batch, dil, sub, D_MODEL), BF16),
                   jax.ShapeDtypeStruct((batch, dil, sub, LANES), F32)],
        compiler_params=_cparams(("parallel", "parallel", "arbitrary")),
        name="attn_dilated_g%d" % group,
    )(qkv, qkv, qkv, qkv, qkv, qkv, qkv, masks)


def _attn_c_kernel(q_ref, k_ref, vt_ref, o_ref, s_ref, smax_ref, m_ref, l_ref, acc_ref, *, tk):
    nk = k_ref.shape[0] // tk
    m_ref[...] = jnp.full(m_ref.shape, MASK_NEG, F32)
    l_ref[...] = jnp.zeros(l_ref.shape, F32)
    acc_ref[...] = jnp.zeros(acc_ref.shape, F32)

    def issue_scores(h, c, slot):
        start = pl.multiple_of(c * tk, tk)
        s = _dot_nt(k_ref[pl.ds(start, tk), :], q_ref[:, h * C_HEAD_DIM:(h + 1) * C_HEAD_DIM])
        s_ref[slot] = s
        smax_ref[slot] = s.max(axis=0, keepdims=True)

    issue_scores(0, 0, 0)

    def chunk(c, carry):
        vt = vt_ref[c]
        for h in range(C_GROUP):
            if h + 1 < C_GROUP:
                issue_scores(h + 1, c, (h + 1) % 2)
            else:
                issue_scores(0, jnp.minimum(c + 1, nk - 1), (h + 1) % 2)
            s = s_ref[h % 2]
            m_prev = m_ref[h]
            m_new = jnp.maximum(m_prev, smax_ref[h % 2])
            alpha = jnp.exp2(m_prev - m_new)
            pr = jnp.exp2(s - m_new)
            l_ref[h] = alpha * l_ref[h] + pr.sum(axis=0, keepdims=True)
            acc_ref[h] = alpha * acc_ref[h] + jnp.dot(vt, pr.astype(BF16), preferred_element_type=F32)
            m_ref[h] = m_new
        return carry

    lax.fori_loop(0, nk, chunk, 0)
    for h in range(C_GROUP):
        o_ref[:, h * C_HEAD_DIM:(h + 1) * C_HEAD_DIM] = (acc_ref[h] / l_ref[h]).T.astype(BF16)


def _attn_c(qk, vt, *, batch, seq):
    tq, tk = C_TQ, C_TK
    nq = seq // tq
    assert seq % tk == 0 and C_GROUP % 2 == 0
    qw = C_GROUP * C_HEAD_DIM
    return pl.pallas_call(
        functools.partial(_attn_c_kernel, tk=tk),
        grid=(batch, C_KV_HEADS, nq),
        in_specs=[
            pl.BlockSpec((tq, qw), lambda b, g, i: (b * nq + i, g)),
            pl.BlockSpec((seq, C_HEAD_DIM), lambda b, g, i: (b, C_Q_HEADS + g)),
            pl.BlockSpec((None, seq // tk, C_HEAD_DIM, tk), lambda b, g, i: (b, 0, g, 0)),
        ],
        out_specs=pl.BlockSpec((tq, qw), lambda b, g, i: (b * nq + i, g)),
        out_shape=jax.ShapeDtypeStruct((batch * seq, C_Q_HEADS * C_HEAD_DIM), BF16),
        scratch_shapes=[pltpu.VMEM((2, tk, tq), F32), pltpu.VMEM((2, 1, tq), F32),
                        pltpu.VMEM((C_GROUP, 1, tq), F32), pltpu.VMEM((C_GROUP, 1, tq), F32),
                        pltpu.VMEM((C_GROUP, C_HEAD_DIM, tq), F32)],
        compiler_params=_cparams(("parallel", "parallel", "arbitrary")),
        name="attn_gqa_flash",
    )(qk, qk, vt)


def _post_kernel(*refs, merge, final, halves, tf):
    refs = list(refs)
    x_ref = refs.pop(0)
    if merge:
        o_refs = [refs.pop(0) for _ in range(3)]
        lse_refs = [refs.pop(0) for _ in range(3)]
        expand_ref = refs.pop(0)
    else:
        o_ref_in = refs.pop(0)
    mod_ref, g_ref, wo_ref, w1_ref, w2_ref = [refs.pop(0) for _ in range(5)]
    fg_ref = refs.pop(0) if final else None
    out_ref = refs.pop(0)
    mod = mod_ref[0]
    hm = x_ref.shape[0] // halves

    def token_order(ref, src):
        dil, _, width = ref.shape
        rows = src.stop - src.start
        if dil == 1:
            return ref[0, src, :].astype(F32)
        cols = []
        for c in range(width // LANES):
            part = ref[:, src, c * LANES:(c + 1) * LANES].astype(F32)
            cols.append(jnp.swapaxes(part, 0, 1).reshape(rows * dil, LANES))
        return cols[0] if len(cols) == 1 else jnp.concatenate(cols, axis=1)

    def merged_groups(hf):
        o_tok, lse = [], []
        for o_g, lse_g in zip(o_refs, lse_refs):
            rows = hm // o_g.shape[0]
            src = slice(hf * rows, (hf + 1) * rows)
            o_tok.append(token_order(o_g, src))
            lse.append(token_order(lse_g, src))
        top = jnp.maximum(jnp.maximum(lse[0], lse[1]), lse[2])
        ex = [jnp.exp2(v - top) for v in lse]
        den = ex[0] + ex[1] + ex[2]
        mixed = None
        for g in range(3):
            wgt = ex[g] / den
            hi = wgt.astype(BF16)
            lo = (wgt - hi.astype(F32)).astype(BF16)
            wide = (jnp.dot(hi, expand_ref[...], preferred_element_type=F32)
                    + jnp.dot(lo, expand_ref[...], preferred_element_type=F32))
            mixed = wide * o_tok[g] if mixed is None else mixed + wide * o_tok[g]
        return mixed.astype(BF16)

    def pre(hf):
        rows = slice(hf * hm, (hf + 1) * hm)
        o = merged_groups(hf) if merge else o_ref_in[rows, :]
        mix = jnp.dot(o, wo_ref[...], preferred_element_type=F32)
        x1 = x_ref[rows, :] + mod[:, 2 * D_MODEL:3 * D_MODEL] * mix
        h2 = _norm_mod(x1, g_ref[0], mod[:, 4 * D_MODEL:5 * D_MODEL], mod[:, 3 * D_MODEL:4 * D_MODEL])
        return x1, h2.astype(BF16)

    def mlp(h2):
        acc = None
        for c in range(D_FF // tf):
            a = jnp.dot(h2, w1_ref[:, c * tf:(c + 1) * tf], preferred_element_type=F32)
            a = jnp.square(jnp.maximum(a, 0.0)).astype(BF16)
            d = jnp.dot(a, w2_ref[c * tf:(c + 1) * tf, :], preferred_element_type=F32)
            acc = d if acc is None else acc + d
        return acc

    pres = [pre(hf) for hf in range(halves)]
    for hf, (x1, h2) in enumerate(pres):
        x2 = x1 + mod[:, 5 * D_MODEL:6 * D_MODEL] * mlp(h2)
        if final:
            ms = jnp.mean(x2 * x2, axis=-1, keepdims=True)
            x2 = x2 * lax.rsqrt(ms + EPS) * fg_ref[...]
        out_ref[hf * hm:(hf + 1) * hm, :] = x2


def _head_expand_matrix():
    e = np.zeros((LANES, D_MODEL), np.float32)
    for h in range(HEADS_AB):
        e[h, h * HEAD_DIM_AB:(h + 1) * HEAD_DIM_AB] = 1.0
    return e


def _post(x2d, attn, mods3, norm_g3, wo, w1, w2, final_g, *, layer, mod_row0, seq, merge, final):
    t = x2d.shape[0]
    tm = TM_POST_MERGE if merge else TM_POST
    hm = tm // POST_HALVES
    per_seq = seq // tm
    row = lambda i: (i, 0)
    const = lambda i: (0, 0)
    resident = pl.Buffered(1)
    in_specs = [pl.BlockSpec((tm, D_MODEL), row)]
    args = [x2d]
    scratch = []
    if merge:
        outs, lses = attn
        def res_spec(arr):
            dil, width = arr.shape[1], arr.shape[3]
            assert hm % (dil * 16) == 0
            return pl.BlockSpec((None, dil, tm // dil, width), lambda i: (i // per_seq, 0, i % per_seq, 0))
        in_specs += [res_spec(a) for a in outs] + [res_spec(a) for a in lses]
        in_specs += [pl.BlockSpec((LANES, D_MODEL), const, pipeline_mode=resident)]
        args += list(outs) + list(lses) + [jnp.asarray(_head_expand_matrix(), BF16)]
    else:
        in_specs += [pl.BlockSpec((tm, D_MODEL), row)]
        args += [attn]
    in_specs += [
        pl.BlockSpec((1, 1, 6 * D_MODEL), lambda i: (layer * MOD_ROWS + mod_row0 + i // per_seq, 0, 0)),
        pl.BlockSpec((1, 1, D_MODEL), lambda i: (2 * layer + 1, 0, 0)),
        pl.BlockSpec((D_MODEL, D_MODEL), const, pipeline_mode=resident),
        pl.BlockSpec((D_MODEL, D_FF), const, pipeline_mode=resident),
        pl.BlockSpec((D_FF, D_MODEL), const, pipeline_mode=resident),
    ]
    args += [mods3, norm_g3, wo, w1, w2]
    if final:
        in_specs += [pl.BlockSpec((1, D_MODEL), const)]
        args += [final_g.reshape(1, D_MODEL)]
    return pl.pallas_call(
        functools.partial(_post_kernel, merge=merge, final=final, halves=POST_HALVES, tf=TF_POST),
        grid=(t // tm,),
        in_specs=in_specs,
        out_specs=pl.BlockSpec((tm, D_MODEL), row),
        out_shape=jax.ShapeDtypeStruct((t, D_MODEL), F32),
        scratch_shapes=scratch,
        compiler_params=_cparams(("parallel",)),
        name="wo_mlp_merge" if merge else "wo_mlp",
    )(*args)


def _trunk(x, mods3, mod_row0, norm_g3, final_g, wts, a_bias, b_masks, tabs_b, tabs_c):
    batch, seq, _ = x.shape
    assert seq <= tabs_b[0].shape[0]
    x2d = x.reshape(batch * seq, D_MODEL)
    for layer in range(DEPTH):
        kind, j = layer % N_MIXERS, layer // N_MIXERS
        common = dict(layer=layer, mod_row0=mod_row0, seq=seq)
        if kind == 0:
            qkv = _qkv_proj(x2d, mods3, norm_g3, wts["a_qkv"][j], mode="a", **common)
            attn = _attn_a(qkv, a_bias[j], batch=batch, seq=seq)
            wo = wts["a_o"][j]
        elif kind == 1:
            groups = []
            for g, (win, dil) in enumerate(B_PAIRS):
                assert win == 2 * B_SIDE * dil
                qkv = _qkv_proj(x2d, mods3, norm_g3, wts["b_qkv"][j], mode="b", cos=tabs_b[0], sin=tabs_b[1],
                                group=g, dil=dil, **common)
                groups.append(_attn_b_group(qkv, b_masks, group=g))
            attn = ([o for o, _ in groups], [l for _, l in groups])
            wo = wts["b_o"][j]
        else:
            qk, vt = _qkv_proj(x2d, mods3, norm_g3, wts["c_qkv"][j], mode="c", cos=tabs_c[0], sin=tabs_c[1],
                               qg=wts["c_qg"][j], kg=wts["c_kg"][j], **common)
            attn = _attn_c(qk, vt, batch=batch, seq=seq)
            wo = wts["c_o"][j]
        x2d = _post(x2d, attn, mods3, norm_g3, wo, wts["w1"][layer], wts["w2"][layer], final_g,
                    merge=(kind == 1), final=(layer == DEPTH - 1), **common)
    return x2d.reshape(batch, seq, D_MODEL)


def kernel(x_prompt, x_sample, c_prompt, c_sample, w_mod, b_mod, norm_g, final_g, a_w_qkv, a_rpb, a_w_o,
           b_w_qkv, b_w_o, c_w_qkv, c_q_g, c_k_g, c_w_o, mlp_w1, mlp_w2):
    nb_p, nb_s = c_prompt.shape[0], c_sample.shape[0]
    assert nb_p + nb_s <= MOD_ROWS
    c_all = jnp.concatenate([c_prompt, c_sample, jnp.zeros((MOD_ROWS - nb_p - nb_s, D_MODEL), F32)], axis=0)
    mods3 = _modulation(c_all, w_mod, b_mod).reshape(DEPTH * MOD_ROWS, 1, 6 * D_MODEL)
    norm_g3 = norm_g.reshape(DEPTH * 2, 1, D_MODEL)

    scale_ab = HEAD_DIM_AB ** -0.5 * LOG2E
    col_scale = np.ones((3 * D_MODEL,), np.float32)
    col_scale[:D_MODEL] = scale_ab
    a_qkv = (a_w_qkv * col_scale).astype(BF16)
    b_qkv = (b_w_qkv * np.tile(col_scale, len(B_PAIRS))).astype(BF16)
    wts = {
        "a_qkv": a_qkv, "a_o": a_w_o.astype(BF16),
        "b_qkv": b_qkv, "b_o": b_w_o.astype(BF16),
        "c_qkv": c_w_qkv.astype(BF16), "c_o": c_w_o.astype(BF16),
        "c_qg": (c_q_g * (C_HEAD_DIM ** -0.5 * LOG2E)).reshape(-1, 1, C_HEAD_DIM),
        "c_kg": c_k_g.reshape(-1, 1, C_HEAD_DIM),
        "w1": mlp_w1.astype(BF16), "w2": mlp_w2.astype(BF16),
    }
    a_bias = [_attn_a_bias(a_rpb[j]) for j in range(a_rpb.shape[0])]
    b_masks = jnp.asarray(_attn_b_masks())

    t = jnp.arange(max(x_prompt.shape[1], x_sample.shape[1]))
    tabs_b = _rope_tables(t, t)
    tabs_c = _rope_tables(t // GRID_W, t % GRID_W)
    shared = (norm_g3, final_g, wts, a_bias, b_masks, tabs_b, tabs_c)
    y_prompt = _trunk(x_prompt, mods3, 0, *shared)
    y_sample = _trunk(x_sample, mods3, nb_p, *shared)
    return (y_prompt, y_sample)
```

```python
import functools

import numpy as np
import jax
import jax.numpy as jnp
from jax import lax
from jax.experimental import pallas as pl
from jax.experimental.pallas import tpu as pltpu

F32 = jnp.float32
BF16 = jnp.bfloat16

D_MODEL = 1024
DEPTH = 4
GRID_W = 64
D_FF = 4 * D_MODEL
EPS = 1e-6
ROPE_THETA = 10000.0
N_MIXERS = 3
HEAD_DIM_AB = 64
HEADS_AB = D_MODEL // HEAD_DIM_AB
A_WIN_ROWS = 8
A_WIN_COLS = 16
B_PAIRS = ((128, 1), (512, 4), (2048, 16))
B_SIDE = 64
C_HEAD_DIM = 128
C_Q_HEADS = 8
C_KV_HEADS = 2
C_GROUP = C_Q_HEADS // C_KV_HEADS
C_QKV_WIDTH = (C_Q_HEADS + 2 * C_KV_HEADS) * C_HEAD_DIM
ROPE_HALF = 32
LANES = 128
MOD_ROWS = 8
MASK_NEG = -1e30
LOG2E = float(np.log2(np.e))
VMEM_LIMIT = 56 * 1024 * 1024

TM_QKV = 1024
TN_QKV = 256
TN_QKV_C = 512
QKV_HALVES = 2
TM_POST = 1024
TM_POST_MERGE = 512
POST_HALVES = 2
TF_POST = 1024
POST_LEAD = 0
POST_LEAD_MERGE = 2
A_TQ = 4 * GRID_W
A_AHEAD = 3
B_AHEAD = 1
B_TQ = 256
B_HALF = 128
C_TQ = 512
C_TK = 1024


def _cparams(sem):
    return pltpu.CompilerParams(dimension_semantics=sem, vmem_limit_bytes=VMEM_LIMIT)


def _mod_kernel(c_ref, w_ref, b_ref, o_ref):
    c = c_ref[...]
    act = (c / (1.0 + jnp.exp(-c))).astype(BF16)
    o_ref[0] = jnp.dot(act, w_ref[0].astype(BF16), preferred_element_type=F32) + b_ref[0]


def _modulation(c_all, w_mod, b_mod):
    tn = 1536
    n = 6 * D_MODEL
    return pl.pallas_call(
        _mod_kernel,
        grid=(DEPTH, n // tn),
        in_specs=[
            pl.BlockSpec((MOD_ROWS, D_MODEL), lambda l, j: (0, 0)),
            pl.BlockSpec((1, D_MODEL, tn), lambda l, j: (l, 0, j)),
            pl.BlockSpec((1, 1, tn), lambda l, j: (l, 0, j)),
        ],
        out_specs=pl.BlockSpec((1, MOD_ROWS, tn), lambda l, j: (l, 0, j)),
        out_shape=jax.ShapeDtypeStruct((DEPTH, MOD_ROWS, n), F32),
        compiler_params=_cparams(("parallel", "parallel")),
        name="adaln_modulation",
    )(c_all, w_mod, b_mod.reshape(DEPTH, 1, n))


def _norm_mod(x, gain, scale, shift):
    ms = jnp.mean(x * x, axis=-1, keepdims=True)
    return (x * lax.rsqrt(ms + EPS) * gain) * (1.0 + scale) + shift


def _rope128(x, cos, sin_signed, low_half):
    up = pltpu.roll(x, LANES - ROPE_HALF, 1)
    down = pltpu.roll(x, ROPE_HALF, 1)
    return x * cos + jnp.where(low_half, up, down) * sin_signed


def _low_half_mask():
    lane = lax.broadcasted_iota(jnp.int32, (1, LANES), 1)
    return (lane % (2 * ROPE_HALF)) < ROPE_HALF


def _rope_tables(seq):
    inv = ROPE_THETA ** (-jnp.arange(ROPE_HALF, dtype=F32) / ROPE_HALF)
    rows = seq // GRID_W

    def trig(pos):
        ang = pos.astype(F32)[:, None] * inv[None, :]
        return jnp.cos(ang), jnp.sin(ang)

    per_row = lambda x: jnp.repeat(x, GRID_W, axis=0)
    per_col = lambda x: jnp.tile(x, (rows, 1))
    c_row, s_row = (per_row(x) for x in trig(jnp.arange(rows)))
    c_col, s_col = (per_col(x) for x in trig(jnp.arange(GRID_W)))
    c_blk, s_blk = (per_row(x) for x in trig(jnp.arange(rows) * GRID_W))
    c_tok = c_blk * c_col - s_blk * s_col
    s_tok = s_blk * c_col + c_blk * s_col

    def table(ca, sa, cb, sb):
        return jnp.concatenate([ca, ca, cb, cb], axis=1), jnp.concatenate([-sa, sa, -sb, sb], axis=1)

    return table(c_tok, s_tok, c_tok, s_tok), table(c_row, s_row, c_col, s_col)


def _qkv_kernel(*refs, mode, dil, halves, tn):
    if mode == "a":
        x_ref, mod_ref, g_ref, w_ref, o_ref = refs
    elif mode == "b":
        x_ref, mod_ref, g_ref, w_ref, cos_ref, sin_ref, o_ref = refs
    else:
        x_ref, mod_ref, g_ref, w_ref, cos_ref, sin_ref, qg_ref, kg_ref, o_ref, vt_ref = refs
    mod = mod_ref[0]
    hm = x_ref.shape[0] // halves
    n = w_ref.shape[1]
    low = _low_half_mask()

    def normed(hf):
        rows = slice(hf * hm, (hf + 1) * hm)
        return _norm_mod(x_ref[rows, :], g_ref[0], mod[:, D_MODEL:2 * D_MODEL], mod[:, 0:D_MODEL]).astype(BF16)

    def emit(hf, j, acc):
        rows = slice(hf * hm, (hf + 1) * hm)
        if mode == "a":
            o_ref[rows, j * tn:(j + 1) * tn] = acc.astype(BF16)
            return
        cos, sin = cos_ref[rows, :], sin_ref[rows, :]
        for c in range(tn // LANES):
            col = j * tn + c * LANES
            val = acc[:, c * LANES:(c + 1) * LANES]
            if mode == "c":
                head = col // C_HEAD_DIM
                if head < C_Q_HEADS + C_KV_HEADS:
                    gain = qg_ref[...] if head < C_Q_HEADS else kg_ref[...]
                    ms = jnp.mean(val * val, axis=-1, keepdims=True)
                    val = _rope128(val * lax.rsqrt(ms + EPS) * gain, cos, sin, low)
                    o_ref[rows, col:col + LANES] = val.astype(BF16)
                else:
                    kv = head - C_Q_HEADS - C_KV_HEADS
                    vt_ref[kv * C_HEAD_DIM:(kv + 1) * C_HEAD_DIM, rows] = val.T.astype(BF16)
                continue
            if col < 2 * D_MODEL:
                val = _rope128(val, cos, sin, low)
            sub = hm // dil
            dst = slice(hf * sub, (hf + 1) * sub)
            if dil == 1:
                o_ref[0, 0, dst, col:col + LANES] = val.astype(BF16)
                continue
            by_residue = jnp.swapaxes(val.reshape(sub, dil, LANES), 0, 1)
            o_ref[0, :, dst, col:col + LANES] = by_residue.astype(BF16)

    hs = [normed(hf) for hf in range(halves)]
    for hf in range(halves):
        for j in range(n // tn):
            emit(hf, j, jnp.dot(hs[hf], w_ref[:, j * tn:(j + 1) * tn], preferred_element_type=F32))


def _qkv_proj(x2d, mods3, norm_g3, w, *, layer, mod_row0, seq, mode, cos=None, sin=None, qg=None, kg=None,
              group=0, dil=1):
    t = x2d.shape[0]
    tm = TM_QKV
    if mode == "b":
        n, col0 = 3 * D_MODEL, group
    else:
        n, col0 = w.shape[1], 0
    tn = TN_QKV_C if mode == "c" else TN_QKV
    hm = tm // QKV_HALVES
    per_seq = seq // tm
    in_specs = [
        pl.BlockSpec((tm, D_MODEL), lambda i: (i, 0)),
        pl.BlockSpec((1, 1, 6 * D_MODEL), lambda i: (layer * MOD_ROWS + mod_row0 + i // per_seq, 0, 0)),
        pl.BlockSpec((1, 1, D_MODEL), lambda i: (2 * layer, 0, 0)),
        pl.BlockSpec((D_MODEL, n), lambda i: (0, col0), pipeline_mode=pl.Buffered(1)),
    ]
    args = [x2d, mods3, norm_g3, w]
    if mode in ("b", "c"):
        in_specs += [pl.BlockSpec((tm, LANES), lambda i: (i % per_seq, 0))] * 2
        args += [cos, sin]
    if mode == "c":
        in_specs += [pl.BlockSpec((1, C_HEAD_DIM), lambda i: (0, 0))] * 2
        args += [qg, kg]
    scratch = []
    if mode == "b":
        assert hm % (dil * 16) == 0
        batch = t // seq
        out_spec = pl.BlockSpec((1, dil, tm // dil, n), lambda i: (i // per_seq, 0, i % per_seq, 0))
        out_shape = jax.ShapeDtypeStruct((batch, dil, seq // dil, n), BF16)
    elif mode == "c":
        assert tm == C_TK
        n_qk = (C_Q_HEADS + C_KV_HEADS) * C_HEAD_DIM
        kv_w = C_KV_HEADS * C_HEAD_DIM
        out_spec = [pl.BlockSpec((tm, n_qk), lambda i: (i, 0)),
                    pl.BlockSpec((None, None, kv_w, tm), lambda i: (i // per_seq, i % per_seq, 0, 0))]
        out_shape = [jax.ShapeDtypeStruct((t, n_qk), BF16),
                     jax.ShapeDtypeStruct((t // seq, per_seq, kv_w, tm), BF16)]
    else:
        out_spec = pl.BlockSpec((tm, n), lambda i: (i, 0))
        out_shape = jax.ShapeDtypeStruct((t, n), BF16)
    return pl.pallas_call(
        functools.partial(_qkv_kernel, mode=mode, dil=dil, halves=QKV_HALVES, tn=tn),
        grid=(t // tm,),
        in_specs=in_specs,
        out_specs=out_spec,
        out_shape=out_shape,
        scratch_shapes=scratch,
        compiler_params=_cparams(("parallel",)),
        name="norm_qkv_" + mode,
    )(*args)


def _pair_masks():
    lane = lax.broadcasted_iota(jnp.int32, (1, LANES), 1)
    first = lane < HEAD_DIM_AB
    return first, jnp.logical_not(first)


def _dot_nt(a, b):
    return lax.dot_general(a, b, (((1,), (1,)), ((), ())), preferred_element_type=F32)


def _pipelined(units, scores, finish, ahead=1):
    pending = [scores(u) for u in units[:ahead]]
    for i, unit in enumerate(units):
        if i + ahead < len(units):
            pending.append(scores(units[i + ahead]))
        finish(unit, pending.pop(0))


def _attn_a_kernel(q_ref, kp_ref, kc_ref, kn_ref, vp_ref, vc_ref, vn_ref, bias_ref, o_ref, kbuf, vbuf):
    tq = q_ref.shape[0]
    rows_per = tq // GRID_W
    win_keys = A_WIN_ROWS * GRID_W
    first = pl.program_id(1) == 0
    last = pl.program_id(1) == pl.num_programs(1) - 1
    for n, (k_r, v_r) in enumerate(((kp_ref, vp_ref), (kc_ref, vc_ref), (kn_ref, vn_ref))):
        kbuf[n * tq:(n + 1) * tq, :] = k_r[...]
        vbuf[n * tq:(n + 1) * tq, :] = v_r[...]
    masks = _pair_masks()
    units = [(p, rl) for p in range(HEADS_AB // 2) for rl in range(rows_per)]

    def window_of(rl):
        start = jnp.where(first, rows_per, jnp.where(last, 2 * rows_per - A_WIN_ROWS,
                                                     rl + rows_per - A_WIN_ROWS // 2))
        return pl.multiple_of(start * GRID_W, GRID_W), start - rows_per - rl + A_WIN_ROWS - 1

    def scores(unit):
        p, rl = unit
        sl = slice(p * LANES, (p + 1) * LANES)
        start, win = window_of(rl)
        qp = q_ref[rl * GRID_W:(rl + 1) * GRID_W, sl]
        stacked = jnp.concatenate([jnp.where(mk, qp, jnp.zeros_like(qp)) for mk in masks], axis=0)
        s = _dot_nt(stacked, kbuf[pl.ds(start, win_keys), sl]) + bias_ref[p, win]
        return s, s.max(axis=-1, keepdims=True)

    def finish(unit, scored):
        s, m = scored
        p, rl = unit
        sl = slice(p * LANES, (p + 1) * LANES)
        start, _ = window_of(rl)
        pr = jnp.exp2(s - m)
        l = pr.sum(axis=-1, keepdims=True)
        o = jnp.dot(pr.astype(BF16), vbuf[pl.ds(start, win_keys), sl], preferred_element_type=F32) / l
        o_ref[rl * GRID_W:(rl + 1) * GRID_W, sl] = jnp.where(masks[0], o[:GRID_W], o[GRID_W:]).astype(BF16)

    _pipelined(units, scores, finish, ahead=A_AHEAD)


def _attn_a_bias(rpb):
    n_dc = 2 * A_WIN_COLS - 1
    c, kc = np.arange(GRID_W)[:, None], np.arange(GRID_W)[None, :]
    onehot_c = ((kc - c + A_WIN_COLS - 1)[..., None] == np.arange(n_dc)).astype(np.float32)
    toe = jnp.einsum("hde,cje->hdcj", rpb * LOG2E, onehot_c, precision=lax.Precision.HIGHEST)
    cs = np.clip(c - A_WIN_COLS // 2, 0, GRID_W - A_WIN_COLS)
    col_ok = (kc >= cs) & (kc < cs + A_WIN_COLS)
    toe = jnp.where(jnp.asarray(col_ok)[None, None], toe, MASK_NEG)
    wins = jnp.stack([jnp.concatenate([toe[:, w + m] for m in range(A_WIN_ROWS)], axis=-1)
                      for w in range(A_WIN_ROWS)], axis=1)
    wins = wins.reshape(HEADS_AB // 2, 2, A_WIN_ROWS, GRID_W, A_WIN_ROWS * GRID_W)
    return jnp.transpose(wins, (0, 2, 1, 3, 4)).reshape(HEADS_AB // 2, A_WIN_ROWS, 2 * GRID_W, A_WIN_ROWS * GRID_W)


def _attn_a(qkv, bias, *, batch, seq):
    tq = A_TQ
    nb = seq // tq
    rows_per = tq // GRID_W
    assert nb >= 3 and A_WIN_ROWS // 2 <= rows_per <= A_WIN_ROWS
    q_spec = pl.BlockSpec((tq, D_MODEL), lambda b, i: (b * nb + i, 0))
    def kv_spec(col, off):
        return pl.BlockSpec((tq, D_MODEL), lambda b, i: (b * nb + jnp.clip(i + off, 0, nb - 1), col))
    bias_spec = pl.BlockSpec(bias.shape, lambda b, i: (0, 0, 0, 0), pipeline_mode=pl.Buffered(1))
    return pl.pallas_call(
        _attn_a_kernel,
        grid=(batch, nb),
        in_specs=[q_spec, kv_spec(1, -1), kv_spec(1, 0), kv_spec(1, 1),
                  kv_spec(2, -1), kv_spec(2, 0), kv_spec(2, 1), bias_spec],
        out_specs=pl.BlockSpec((tq, D_MODEL), lambda b, i: (b * nb + i, 0)),
        out_shape=jax.ShapeDtypeStruct((batch * seq, D_MODEL), BF16),
        scratch_shapes=[pltpu.VMEM((3 * tq, D_MODEL), BF16), pltpu.VMEM((3 * tq, D_MODEL), BF16)],
        compiler_params=_cparams(("parallel", "arbitrary")),
        name="attn_neighbourhood",
    )(qkv, qkv, qkv, qkv, qkv, qkv, qkv, bias)


def _attn_b_kernel(q_ref, kp_ref, kc_ref, kn_ref, vp_ref, vc_ref, vn_ref, mask_ref, o_ref, lse_ref):
    tq = q_ref.shape[0]
    masks = _pair_masks()
    lane = lax.broadcasted_iota(jnp.int32, (1, LANES), 1)
    lse_tiles = [jnp.zeros((B_HALF, LANES), F32) for _ in range(tq // B_HALF)]
    units = [(p, hf) for p in range(HEADS_AB // 2) for hf in range(tq // B_HALF)]

    def window(prev_ref, cur_ref, next_ref, hf, sl):
        lo, hi = hf * B_HALF - B_SIDE, (hf + 1) * B_HALF + B_SIDE
        parts = []
        if lo < 0:
            parts.append(prev_ref[tq + lo:, sl])
        parts.append(cur_ref[max(lo, 0):min(hi, tq), sl])
        if hi > tq:
            parts.append(next_ref[:hi - tq, sl])
        return jnp.concatenate(parts, axis=0)

    def scores(unit):
        p, hf = unit
        sl = slice(p * LANES, (p + 1) * LANES)
        qp = q_ref[hf * B_HALF:(hf + 1) * B_HALF, sl]
        stacked = jnp.concatenate([jnp.where(mk, qp, jnp.zeros_like(qp)) for mk in masks], axis=0)
        s = _dot_nt(stacked, window(kp_ref, kc_ref, kn_ref, hf, sl)) + mask_ref[0, hf]
        return s, s.max(axis=-1, keepdims=True)

    def finish(unit, scored):
        s, m = scored
        p, hf = unit
        sl = slice(p * LANES, (p + 1) * LANES)
        pr = jnp.exp2(s - m)
        l = pr.sum(axis=-1, keepdims=True)
        o = jnp.dot(pr.astype(BF16), window(vp_ref, vc_ref, vn_ref, hf, sl), preferred_element_type=F32) / l
        o_ref[hf * B_HALF:(hf + 1) * B_HALF, sl] = jnp.where(masks[0], o[:B_HALF], o[B_HALF:]).astype(BF16)
        lse = m + jnp.log2(l)
        tile = jnp.where(lane == 2 * p, lse[:B_HALF], lse_tiles[hf])
        lse_tiles[hf] = jnp.where(lane == 2 * p + 1, lse[B_HALF:], tile)

    _pipelined(units, scores, finish, ahead=B_AHEAD)
    for hf, tile in enumerate(lse_tiles):
        lse_ref[hf * B_HALF:(hf + 1) * B_HALF, :] = tile


def _attn_b_masks():
    qq = np.arange(B_HALF)[:, None]
    jj = np.arange(B_HALF + 2 * B_SIDE)[None, :]
    band = (jj - qq >= 0) & (jj - qq <= 2 * B_SIDE)
    out = []
    for ty in range(4):
        per_half = []
        for hf in range(B_TQ // B_HALF):
            pos = hf * B_HALF - B_SIDE + jj
            ok = band
            if ty & 1:
                ok = ok & (pos >= 0)
            if ty & 2:
                ok = ok & (pos < B_TQ)
            half = np.where(ok, 0.0, MASK_NEG)
            per_half.append(np.concatenate([half, half], axis=0))
        out.append(np.stack(per_half))
    return np.stack(out).astype(np.float32)


def _attn_b_group(qkv, masks, *, group):
    batch, dil, sub, _ = qkv.shape
    tq = B_TQ
    nb = sub // tq
    assert sub % tq == 0
    q_spec = pl.BlockSpec((None, None, tq, D_MODEL), lambda b, r, i: (b, r, i, 0))
    def kv_spec(which, off):
        return pl.BlockSpec((None, None, tq, D_MODEL),
                            lambda b, r, i: (b, r, jnp.clip(i + off, 0, nb - 1), which))
    mask_spec = pl.BlockSpec((1,) + masks.shape[1:],
                             lambda b, r, i: ((i == 0).astype(jnp.int32) + 2 * (i == nb - 1).astype(jnp.int32),
                                              0, 0, 0))
    return pl.pallas_call(
        _attn_b_kernel,
        grid=(batch, dil, nb),
        in_specs=[q_spec, kv_spec(1, -1), kv_spec(1, 0), kv_spec(1, 1),
                  kv_spec(2, -1), kv_spec(2, 0), kv_spec(2, 1), mask_spec],
        out_specs=[pl.BlockSpec((None, None, tq, D_MODEL), lambda b, r, i: (b, r, i, 0)),
                   pl.BlockSpec((None, None, tq, LANES), lambda b, r, i: (b, r, i, 0))],
        out_shape=[jax.ShapeDtypeStruct((batch, dil, sub, D_MODEL), BF16),
                   jax.ShapeDtypeStruct((batch, dil, sub, LANES), F32)],
        compiler_params=_cparams(("parallel", "parallel", "arbitrary")),
        name="attn_dilated_g%d" % group,
    )(qkv, qkv, qkv, qkv, qkv, qkv, qkv, masks)


def _attn_c_kernel(q_ref, k_ref, vt_ref, o_ref, s_ref, smax_ref, m_ref, l_ref, acc_ref, *, tk):
    nk = k_ref.shape[0] // tk
    m_ref[...] = jnp.full(m_ref.shape, MASK_NEG, F32)
    l_ref[...] = jnp.zeros(l_ref.shape, F32)
    acc_ref[...] = jnp.zeros(acc_ref.shape, F32)

    def issue_scores(h, c, slot):
        start = pl.multiple_of(c * tk, tk)
        s = _dot_nt(k_ref[pl.ds(start, tk), :], q_ref[:, h * C_HEAD_DIM:(h + 1) * C_HEAD_DIM])
        s_ref[slot] = s
        smax_ref[slot] = s.max(axis=0, keepdims=True)

    issue_scores(0, 0, 0)

    def chunk(c, carry):
        vt = vt_ref[c]
        for h in range(C_GROUP):
            if h + 1 < C_GROUP:
                issue_scores(h + 1, c, (h + 1) % 2)
            else:
                issue_scores(0, jnp.minimum(c + 1, nk - 1), (h + 1) % 2)
            s = s_ref[h % 2]
            m_prev = m_ref[h]
            m_new = jnp.maximum(m_prev, smax_ref[h % 2])
            alpha = jnp.exp2(m_prev - m_new)
            pr = jnp.exp2(s - m_new)
            l_ref[h] = alpha * l_ref[h] + pr.sum(axis=0, keepdims=True)
            acc_ref[h] = alpha * acc_ref[h] + jnp.dot(vt, pr.astype(BF16), preferred_element_type=F32)
            m_ref[h] = m_new
        return carry

    lax.fori_loop(0, nk, chunk, 0)
    for h in range(C_GROUP):
        o_ref[:, h * C_HEAD_DIM:(h + 1) * C_HEAD_DIM] = (acc_ref[h] / l_ref[h]).T.astype(BF16)


def _attn_c(qk, vt, *, batch, seq):
    tq, tk = C_TQ, C_TK
    nq = seq // tq
    assert seq % tk == 0 and C_GROUP % 2 == 0
    qw = C_GROUP * C_HEAD_DIM
    return pl.pallas_call(
        functools.partial(_attn_c_kernel, tk=tk),
        grid=(batch, C_KV_HEADS, nq),
        in_specs=[
            pl.BlockSpec((tq, qw), lambda b, g, i: (b * nq + i, g)),
            pl.BlockSpec((seq, C_HEAD_DIM), lambda b, g, i: (b, C_Q_HEADS + g)),
            pl.BlockSpec((None, seq // tk, C_HEAD_DIM, tk), lambda b, g, i: (b, 0, g, 0)),
        ],
        out_specs=pl.BlockSpec((tq, qw), lambda b, g, i: (b * nq + i, g)),
        out_shape=jax.ShapeDtypeStruct((batch * seq, C_Q_HEADS * C_HEAD_DIM), BF16),
        scratch_shapes=[pltpu.VMEM((2, tk, tq), F32), pltpu.VMEM((2, 1, tq), F32),
                        pltpu.VMEM((C_GROUP, 1, tq), F32), pltpu.VMEM((C_GROUP, 1, tq), F32),
                        pltpu.VMEM((C_GROUP, C_HEAD_DIM, tq), F32)],
        compiler_params=_cparams(("parallel", "parallel", "arbitrary")),
        name="attn_gqa_flash",
    )(qk, qk, vt)


def _post_kernel(*refs, merge, final, halves, tf, lead):
    refs = list(refs)
    x_ref = refs.pop(0)
    if merge:
        o_refs = [refs.pop(0) for _ in range(3)]
        lse_refs = [refs.pop(0) for _ in range(3)]
    else:
        o_ref_in = refs.pop(0)
    mod_ref, g_ref, wo_ref, w1_ref, w2_ref = [refs.pop(0) for _ in range(5)]
    fg_ref = refs.pop(0) if final else None
    out_ref = refs.pop(0)
    mod = mod_ref[0]
    hm = x_ref.shape[0] // halves

    def token_order(ref, hf, c):
        dil = ref.shape[0]
        rows = hm // dil
        part = ref[:, hf * rows:(hf + 1) * rows, c * LANES:(c + 1) * LANES].astype(F32)
        return part[0] if dil == 1 else jnp.swapaxes(part, 0, 1).reshape(hm, LANES)

    def merged_groups(hf):
        lse = [token_order(r, hf, 0) for r in lse_refs]
        top = jnp.maximum(jnp.maximum(lse[0], lse[1]), lse[2])
        ex = [jnp.exp2(v - top) for v in lse]
        den = ex[0] + ex[1] + ex[2]
        wgt = [e / den for e in ex]
        lane = lax.broadcasted_iota(jnp.int32, (hm, LANES), 1)
        cols = []
        for c in range(D_MODEL // LANES):
            head_of_lane = lane // HEAD_DIM_AB + c * (LANES // HEAD_DIM_AB)
            mixed = None
            for g in range(3):
                term = jnp.take_along_axis(wgt[g], head_of_lane, axis=1) * token_order(o_refs[g], hf, c)
                mixed = term if mixed is None else mixed + term
            cols.append(mixed.astype(BF16))
        return jnp.concatenate(cols, axis=1)

    def pre(hf):
        rows = slice(hf * hm, (hf + 1) * hm)
        o = merged_groups(hf) if merge else o_ref_in[rows, :]
        mix = jnp.dot(o, wo_ref[...], preferred_element_type=F32)
        x1 = x_ref[rows, :] + mod[:, 2 * D_MODEL:3 * D_MODEL] * mix
        h2 = _norm_mod(x1, g_ref[0], mod[:, 4 * D_MODEL:5 * D_MODEL], mod[:, 3 * D_MODEL:4 * D_MODEL])
        return x1, h2.astype(BF16)

    def mlp(h2, acc, chunks):
        for c in chunks:
            a = jnp.dot(h2, w1_ref[:, c * tf:(c + 1) * tf], preferred_element_type=F32)
            a = jnp.square(jnp.maximum(a, 0.0)).astype(BF16)
            d = jnp.dot(a, w2_ref[c * tf:(c + 1) * tf, :], preferred_element_type=F32)
            acc = d if acc is None else acc + d
        return acc

    def finish(hf, x1, acc):
        x2 = x1 + mod[:, 5 * D_MODEL:6 * D_MODEL] * acc
        if final:
            ms = jnp.mean(x2 * x2, axis=-1, keepdims=True)
            x2 = x2 * lax.rsqrt(ms + EPS) * fg_ref[...]
        out_ref[hf * hm:(hf + 1) * hm, :] = x2

    n_chunks = D_FF // tf
    cur = pre(0)
    for hf in range(halves):
        x1, h2 = cur
        acc = mlp(h2, None, range(0, lead))
        if hf + 1 < halves:
            cur = pre(hf + 1)
        finish(hf, x1, mlp(h2, acc, range(lead, n_chunks)))


def _post(x2d, attn, mods3, norm_g3, wo, w1, w2, final_g, *, layer, mod_row0, seq, merge, final):
    t = x2d.shape[0]
    tm = TM_POST_MERGE if merge else TM_POST
    hm = tm // POST_HALVES
    per_seq = seq // tm
    row = lambda i: (i, 0)
    const = lambda i: (0, 0)
    resident = pl.Buffered(1)
    in_specs = [pl.BlockSpec((tm, D_MODEL), row)]
    args = [x2d]
    scratch = []
    if merge:
        outs, lses = attn
        def res_spec(arr):
            dil, width = arr.shape[1], arr.shape[3]
            assert hm % (dil * 16) == 0
            return pl.BlockSpec((None, dil, tm // dil, width), lambda i: (i // per_seq, 0, i % per_seq, 0))
        in_specs += [res_spec(a) for a in outs] + [res_spec(a) for a in lses]
        args += list(outs) + list(lses)
    else:
        in_specs += [pl.BlockSpec((tm, D_MODEL), row)]
        args += [attn]
    in_specs += [
        pl.BlockSpec((1, 1, 6 * D_MODEL), lambda i: (layer * MOD_ROWS + mod_row0 + i // per_seq, 0, 0)),
        pl.BlockSpec((1, 1, D_MODEL), lambda i: (2 * layer + 1, 0, 0)),
        pl.BlockSpec((D_MODEL, D_MODEL), const, pipeline_mode=resident),
        pl.BlockSpec((D_MODEL, D_FF), const, pipeline_mode=resident),
        pl.BlockSpec((D_FF, D_MODEL), const, pipeline_mode=resident),
    ]
    args += [mods3, norm_g3, wo, w1, w2]
    if final:
        in_specs += [pl.BlockSpec((1, D_MODEL), const)]
        args += [final_g.reshape(1, D_MODEL)]
    return pl.pallas_call(
        functools.partial(_post_kernel, merge=merge, final=final, halves=POST_HALVES, tf=TF_POST,
                          lead=POST_LEAD_MERGE if merge else POST_LEAD),
        grid=(t // tm,),
        in_specs=in_specs,
        out_specs=pl.BlockSpec((tm, D_MODEL), row),
        out_shape=jax.ShapeDtypeStruct((t, D_MODEL), F32),
        scratch_shapes=scratch,
        compiler_params=_cparams(("parallel",)),
        name="wo_mlp_merge" if merge else "wo_mlp",
    )(*args)


def _trunk(x, mods3, mod_row0, norm_g3, final_g, wts, a_bias, b_masks, tabs_b, tabs_c):
    batch, seq, _ = x.shape
    assert seq <= tabs_b[0].shape[0]
    x2d = x.reshape(batch * seq, D_MODEL)
    for layer in range(DEPTH):
        kind, j = layer % N_MIXERS, layer // N_MIXERS
        common = dict(layer=layer, mod_row0=mod_row0, seq=seq)
        if kind == 0:
            qkv = _qkv_proj(x2d, mods3, norm_g3, wts["a_qkv"][j], mode="a", **common)
            attn = _attn_a(qkv, a_bias[j], batch=batch, seq=seq)
            wo = wts["a_o"][j]
        elif kind == 1:
            groups = []
            for g, (win, dil) in enumerate(B_PAIRS):
                assert win == 2 * B_SIDE * dil
                qkv = _qkv_proj(x2d, mods3, norm_g3, wts["b_qkv"][j], mode="b", cos=tabs_b[0], sin=tabs_b[1],
                                group=g, dil=dil, **common)
                groups.append(_attn_b_group(qkv, b_masks, group=g))
            attn = ([o for o, _ in groups], [l for _, l in groups])
            wo = wts["b_o"][j]
        else:
            qk, vt = _qkv_proj(x2d, mods3, norm_g3, wts["c_qkv"][j], mode="c", cos=tabs_c[0], sin=tabs_c[1],
                               qg=wts["c_qg"][j], kg=wts["c_kg"][j], **common)
            attn = _attn_c(qk, vt, batch=batch, seq=seq)
            wo = wts["c_o"][j]
        x2d = _post(x2d, attn, mods3, norm_g3, wo, wts["w1"][layer], wts["w2"][layer], final_g,
                    merge=(kind == 1), final=(layer == DEPTH - 1), **common)
    return x2d.reshape(batch, seq, D_MODEL)


def kernel(x_prompt, x_sample, c_prompt, c_sample, w_mod, b_mod, norm_g, final_g, a_w_qkv, a_rpb, a_w_o,
           b_w_qkv, b_w_o, c_w_qkv, c_q_g, c_k_g, c_w_o, mlp_w1, mlp_w2):
    nb_p, nb_s = c_prompt.shape[0], c_sample.shape[0]
    assert nb_p + nb_s <= MOD_ROWS
    c_all = jnp.concatenate([c_prompt, c_sample, jnp.zeros((MOD_ROWS - nb_p - nb_s, D_MODEL), F32)], axis=0)
    mods3 = _modulation(c_all, w_mod, b_mod).reshape(DEPTH * MOD_ROWS, 1, 6 * D_MODEL)
    norm_g3 = norm_g.reshape(DEPTH * 2, 1, D_MODEL)

    scale_ab = HEAD_DIM_AB ** -0.5 * LOG2E
    col_scale = np.ones((3 * D_MODEL,), np.float32)
    col_scale[:D_MODEL] = scale_ab
    a_qkv = (a_w_qkv * col_scale).astype(BF16)
    b_qkv = (b_w_qkv * np.tile(col_scale, len(B_PAIRS))).astype(BF16)
    wts = {
        "a_qkv": a_qkv, "a_o": a_w_o.astype(BF16),
        "b_qkv": b_qkv, "b_o": b_w_o.astype(BF16),
        "c_qkv": c_w_qkv.astype(BF16), "c_o": c_w_o.astype(BF16),
        "c_qg": (c_q_g * (C_HEAD_DIM ** -0.5 * LOG2E)).reshape(-1, 1, C_HEAD_DIM),
        "c_kg": c_k_g.reshape(-1, 1, C_HEAD_DIM),
        "w1": mlp_w1.astype(BF16), "w2": mlp_w2.astype(BF16),
    }
    a_bias = [_attn_a_bias(a_rpb[j]) for j in range(a_rpb.shape[0])]
    b_masks = jnp.asarray(_attn_b_masks())

    tabs_b, tabs_c = _rope_tables(max(x_prompt.shape[1], x_sample.shape[1]))
    shared = (norm_g3, final_g, wts, a_bias, b_masks, tabs_b, tabs_c)
    y_prompt = _trunk(x_prompt, mods3, 0, *shared)
    y_sample = _trunk(x_sample, mods3, nb_p, *shared)
    return (y_prompt, y_sample)
```

```python
import functools

import numpy as np
import jax
import jax.numpy as jnp
from jax import lax
from jax.experimental import pallas as pl
from jax.experimental.pallas import tpu as pltpu

F32 = jnp.float32
BF16 = jnp.bfloat16

D_MODEL = 1024
DEPTH = 4
GRID_W = 64
D_FF = 4 * D_MODEL
EPS = 1e-6
ROPE_THETA = 10000.0
N_MIXERS = 3
HEAD_DIM_AB = 64
HEADS_AB = D_MODEL // HEAD_DIM_AB
A_WIN_ROWS = 8
A_WIN_COLS = 16
B_PAIRS = ((128, 1), (512, 4), (2048, 16))
B_SIDE = 64
C_HEAD_DIM = 128
C_Q_HEADS = 8
C_KV_HEADS = 2
C_GROUP = C_Q_HEADS // C_KV_HEADS
C_QKV_WIDTH = (C_Q_HEADS + 2 * C_KV_HEADS) * C_HEAD_DIM
ROPE_HALF = 32
LANES = 128
MOD_ROWS = 8
MASK_NEG = -1e30
LOG2E = float(np.log2(np.e))
VMEM_LIMIT = 56 * 1024 * 1024

TM_QKV = 1024
TN_QKV = 256
TN_QKV_C = 512
QKV_HALVES = 2
TM_POST = 1024
TM_POST_MERGE = 512
POST_HALVES = 2
TF_POST = 1024
POST_LEAD = 0
POST_LEAD_MERGE = 2
A_TQ = 4 * GRID_W
A_AHEAD = 3
B_AHEAD = 1
B_TQ = 256
B_HALF = 128
C_TQ = 512
C_TK = 1024
C_VT_ROWS = C_HEAD_DIM + 16


def _cparams(sem):
    return pltpu.CompilerParams(dimension_semantics=sem, vmem_limit_bytes=VMEM_LIMIT)


def _mod_kernel(c_ref, w_ref, b_ref, o_ref):
    c = c_ref[...]
    act = (c / (1.0 + jnp.exp(-c))).astype(BF16)
    o_ref[0] = jnp.dot(act, w_ref[0].astype(BF16), preferred_element_type=F32) + b_ref[0]


def _modulation(c_all, w_mod, b_mod):
    tn = 1536
    n = 6 * D_MODEL
    return pl.pallas_call(
        _mod_kernel,
        grid=(DEPTH, n // tn),
        in_specs=[
            pl.BlockSpec((MOD_ROWS, D_MODEL), lambda l, j: (0, 0)),
            pl.BlockSpec((1, D_MODEL, tn), lambda l, j: (l, 0, j)),
            pl.BlockSpec((1, 1, tn), lambda l, j: (l, 0, j)),
        ],
        out_specs=pl.BlockSpec((1, MOD_ROWS, tn), lambda l, j: (l, 0, j)),
        out_shape=jax.ShapeDtypeStruct((DEPTH, MOD_ROWS, n), F32),
        compiler_params=_cparams(("parallel", "parallel")),
        name="adaln_modulation",
    )(c_all, w_mod, b_mod.reshape(DEPTH, 1, n))


def _norm_mod(x, gain, scale, shift):
    ms = jnp.mean(x * x, axis=-1, keepdims=True)
    return (x * lax.rsqrt(ms + EPS) * gain) * (1.0 + scale) + shift


def _rope128(x, cos, sin_signed, low_half):
    up = pltpu.roll(x, LANES - ROPE_HALF, 1)
    down = pltpu.roll(x, ROPE_HALF, 1)
    return x * cos + jnp.where(low_half, up, down) * sin_signed


def _low_half_mask():
    lane = lax.broadcasted_iota(jnp.int32, (1, LANES), 1)
    return (lane % (2 * ROPE_HALF)) < ROPE_HALF


def _rope_tables(seq):
    inv = ROPE_THETA ** (-jnp.arange(ROPE_HALF, dtype=F32) / ROPE_HALF)
    rows = seq // GRID_W

    def trig(pos):
        ang = pos.astype(F32)[:, None] * inv[None, :]
        return jnp.cos(ang), jnp.sin(ang)

    per_row = lambda x: jnp.repeat(x, GRID_W, axis=0)
    per_col = lambda x: jnp.tile(x, (rows, 1))
    c_row, s_row = (per_row(x) for x in trig(jnp.arange(rows)))
    c_col, s_col = (per_col(x) for x in trig(jnp.arange(GRID_W)))
    c_blk, s_blk = (per_row(x) for x in trig(jnp.arange(rows) * GRID_W))
    c_tok = c_blk * c_col - s_blk * s_col
    s_tok = s_blk * c_col + c_blk * s_col

    def table(ca, sa, cb, sb):
        return jnp.concatenate([ca, ca, cb, cb], axis=1), jnp.concatenate([-sa, sa, -sb, sb], axis=1)

    return table(c_tok, s_tok, c_tok, s_tok), table(c_row, s_row, c_col, s_col)


def _qkv_kernel(*refs, mode, dil, halves, tn):
    if mode == "a":
        x_ref, mod_ref, g_ref, w_ref, o_ref = refs
    elif mode == "b":
        x_ref, mod_ref, g_ref, w_ref, cos_ref, sin_ref, o_ref = refs
    else:
        x_ref, mod_ref, g_ref, w_ref, cos_ref, sin_ref, qg_ref, kg_ref, o_ref, vt_ref = refs
    mod = mod_ref[0]
    hm = x_ref.shape[0] // halves
    n = w_ref.shape[1]
    low = _low_half_mask()

    def normed(hf):
        rows = slice(hf * hm, (hf + 1) * hm)
        return _norm_mod(x_ref[rows, :], g_ref[0], mod[:, D_MODEL:2 * D_MODEL], mod[:, 0:D_MODEL]).astype(BF16)

    def emit(hf, j, acc):
        rows = slice(hf * hm, (hf + 1) * hm)
        if mode == "a":
            o_ref[rows, j * tn:(j + 1) * tn] = acc.astype(BF16)
            return
        cos, sin = cos_ref[rows, :], sin_ref[rows, :]
        for c in range(tn // LANES):
            col = j * tn + c * LANES
            val = acc[:, c * LANES:(c + 1) * LANES]
            if mode == "c":
                head = col // C_HEAD_DIM
                if head < C_Q_HEADS + C_KV_HEADS:
                    gain = qg_ref[...] if head < C_Q_HEADS else kg_ref[...]
                    ms = jnp.mean(val * val, axis=-1, keepdims=True)
                    val = _rope128(val * lax.rsqrt(ms + EPS) * gain, cos, sin, low)
                    o_ref[rows, col:col + LANES] = val.astype(BF16)
                else:
                    kv = head - C_Q_HEADS - C_KV_HEADS
                    vt_ref[kv * C_VT_ROWS:kv * C_VT_ROWS + C_HEAD_DIM, rows] = val.T.astype(BF16)
                    vt_ref[kv * C_VT_ROWS + C_HEAD_DIM:(kv + 1) * C_VT_ROWS, rows] = jnp.ones(
                        (C_VT_ROWS - C_HEAD_DIM, hm), BF16)
                continue
            if col < 2 * D_MODEL:
                val = _rope128(val, cos, sin, low)
            sub = hm // dil
            dst = slice(hf * sub, (hf + 1) * sub)
            if dil == 1:
                o_ref[0, 0, dst, col:col + LANES] = val.astype(BF16)
                continue
            by_residue = jnp.swapaxes(val.reshape(sub, dil, LANES), 0, 1)
            o_ref[0, :, dst, col:col + LANES] = by_residue.astype(BF16)

    hs = [normed(hf) for hf in range(halves)]
    for hf in range(halves):
        for j in range(n // tn):
            emit(hf, j, jnp.dot(hs[hf], w_ref[:, j * tn:(j + 1) * tn], preferred_element_type=F32))


def _qkv_proj(x2d, mods3, norm_g3, w, *, layer, mod_row0, seq, mode, cos=None, sin=None, qg=None, kg=None,
              group=0, dil=1):
    t = x2d.shape[0]
    tm = TM_QKV
    if mode == "b":
        n, col0 = 3 * D_MODEL, group
    else:
        n, col0 = w.shape[1], 0
    tn = TN_QKV_C if mode == "c" else TN_QKV
    hm = tm // QKV_HALVES
    per_seq = seq // tm
    in_specs = [
        pl.BlockSpec((tm, D_MODEL), lambda i: (i, 0)),
        pl.BlockSpec((1, 1, 6 * D_MODEL), lambda i: (layer * MOD_ROWS + mod_row0 + i // per_seq, 0, 0)),
        pl.BlockSpec((1, 1, D_MODEL), lambda i: (2 * layer, 0, 0)),
        pl.BlockSpec((D_MODEL, n), lambda i: (0, col0), pipeline_mode=pl.Buffered(1)),
    ]
    args = [x2d, mods3, norm_g3, w]
    if mode in ("b", "c"):
        in_specs += [pl.BlockSpec((tm, LANES), lambda i: (i % per_seq, 0))] * 2
        args += [cos, sin]
    if mode == "c":
        in_specs += [pl.BlockSpec((1, C_HEAD_DIM), lambda i: (0, 0))] * 2
        args += [qg, kg]
    scratch = []
    if mode == "b":
        assert hm % (dil * 16) == 0
        batch = t // seq
        out_spec = pl.BlockSpec((1, dil, tm // dil, n), lambda i: (i // per_seq, 0, i % per_seq, 0))
        out_shape = jax.ShapeDtypeStruct((batch, dil, seq // dil, n), BF16)
    elif mode == "c":
        assert tm == C_TK
        n_qk = (C_Q_HEADS + C_KV_HEADS) * C_HEAD_DIM
        kv_w = C_KV_HEADS * C_VT_ROWS
        out_spec = [pl.BlockSpec((tm, n_qk), lambda i: (i, 0)),
                    pl.BlockSpec((None, None, kv_w, tm), lambda i: (i // per_seq, i % per_seq, 0, 0))]
        out_shape = [jax.ShapeDtypeStruct((t, n_qk), BF16),
                     jax.ShapeDtypeStruct((t // seq, per_seq, kv_w, tm), BF16)]
    else:
        out_spec = pl.BlockSpec((tm, n), lambda i: (i, 0))
        out_shape = jax.ShapeDtypeStruct((t, n), BF16)
    return pl.pallas_call(
        functools.partial(_qkv_kernel, mode=mode, dil=dil, halves=QKV_HALVES, tn=tn),
        grid=(t // tm,),
        in_specs=in_specs,
        out_specs=out_spec,
        out_shape=out_shape,
        scratch_shapes=scratch,
        compiler_params=_cparams(("parallel",)),
        name="norm_qkv_" + mode,
    )(*args)


def _pair_masks():
    lane = lax.broadcasted_iota(jnp.int32, (1, LANES), 1)
    first = lane < HEAD_DIM_AB
    return first, jnp.logical_not(first)


def _dot_nt(a, b):
    return lax.dot_general(a, b, (((1,), (1,)), ((), ())), preferred_element_type=F32)


def _pipelined(units, scores, finish, ahead=1):
    pending = [scores(u) for u in units[:ahead]]
    for i, unit in enumerate(units):
        if i + ahead < len(units):
            pending.append(scores(units[i + ahead]))
        finish(unit, pending.pop(0))


def _attn_a_kernel(q_ref, kbuf, vbuf, bias_ref, o_ref):
    tq = q_ref.shape[0]
    rows_per = tq // GRID_W
    win_keys = A_WIN_ROWS * GRID_W
    first = pl.program_id(1) == 0
    last = pl.program_id(1) == pl.num_programs(1) - 1
    masks = _pair_masks()
    units = [(p, rl) for p in range(HEADS_AB // 2) for rl in range(rows_per)]

    def window_of(rl):
        centred = A_WIN_ROWS - 1 - A_WIN_ROWS // 2
        start = jnp.where(first, 0, jnp.where(last, 3 * rows_per - A_WIN_ROWS,
                                              rl + rows_per - A_WIN_ROWS // 2))
        win = jnp.where(first, A_WIN_ROWS - 1 - rl, jnp.where(last, rows_per - 1 - rl, centred))
        return pl.multiple_of(start * GRID_W, GRID_W), win

    def scores(unit):
        p, rl = unit
        sl = slice(p * LANES, (p + 1) * LANES)
        start, win = window_of(rl)
        qp = q_ref[rl * GRID_W:(rl + 1) * GRID_W, sl]
        stacked = jnp.concatenate([jnp.where(mk, qp, jnp.zeros_like(qp)) for mk in masks], axis=0)
        s = _dot_nt(stacked, kbuf[pl.ds(start, win_keys), sl]) + bias_ref[p, win]
        return s, s.max(axis=-1, keepdims=True)

    def finish(unit, scored):
        s, m = scored
        p, rl = unit
        sl = slice(p * LANES, (p + 1) * LANES)
        start, _ = window_of(rl)
        pr = jnp.exp2(s - m)
        l = pr.sum(axis=-1, keepdims=True)
        o = jnp.dot(pr.astype(BF16), vbuf[pl.ds(start, win_keys), sl], preferred_element_type=F32) / l
        o_ref[rl * GRID_W:(rl + 1) * GRID_W, sl] = jnp.where(masks[0], o[:GRID_W], o[GRID_W:]).astype(BF16)

    _pipelined(units, scores, finish, ahead=A_AHEAD)


def _attn_a_bias(rpb):
    n_dc = 2 * A_WIN_COLS - 1
    c, kc = np.arange(GRID_W)[:, None], np.arange(GRID_W)[None, :]
    onehot_c = ((kc - c + A_WIN_COLS - 1)[..., None] == np.arange(n_dc)).astype(np.float32)
    toe = jnp.einsum("hde,cje->hdcj", rpb * LOG2E, onehot_c, precision=lax.Precision.HIGHEST)
    cs = np.clip(c - A_WIN_COLS // 2, 0, GRID_W - A_WIN_COLS)
    col_ok = (kc >= cs) & (kc < cs + A_WIN_COLS)
    toe = jnp.where(jnp.asarray(col_ok)[None, None], toe, MASK_NEG)
    wins = jnp.stack([jnp.concatenate([toe[:, w + m] for m in range(A_WIN_ROWS)], axis=-1)
                      for w in range(A_WIN_ROWS)], axis=1)
    wins = wins.reshape(HEADS_AB // 2, 2, A_WIN_ROWS, GRID_W, A_WIN_ROWS * GRID_W)
    return jnp.transpose(wins, (0, 2, 1, 3, 4)).reshape(HEADS_AB // 2, A_WIN_ROWS, 2 * GRID_W, A_WIN_ROWS * GRID_W)


def _attn_a(qkv, bias, *, batch, seq):
    tq = A_TQ
    nb = seq // tq
    rows_per = tq // GRID_W
    assert nb >= 3 and A_WIN_ROWS // 2 <= rows_per <= A_WIN_ROWS
    q_spec = pl.BlockSpec((tq, D_MODEL), lambda b, i: (b * nb + i, 0))
    def kv_spec(col):
        def index(b, i):
            row = b * seq + jnp.clip((i - 1) * tq, 0, seq - 3 * tq)
            return pl.multiple_of(row, tq), col * D_MODEL
        return pl.BlockSpec((pl.Element(3 * tq), pl.Element(D_MODEL)), index)
    bias_spec = pl.BlockSpec(bias.shape, lambda b, i: (0, 0, 0, 0), pipeline_mode=pl.Buffered(1))
    return pl.pallas_call(
        _attn_a_kernel,
        grid=(batch, nb),
        in_specs=[q_spec, kv_spec(1), kv_spec(2), bias_spec],
        out_specs=pl.BlockSpec((tq, D_MODEL), lambda b, i: (b * nb + i, 0)),
        out_shape=jax.ShapeDtypeStruct((batch * seq, D_MODEL), BF16),
        compiler_params=_cparams(("parallel", "arbitrary")),
        name="attn_neighbourhood",
    )(qkv, qkv, qkv, bias)


def _attn_b_kernel(q_ref, kp_ref, kc_ref, kn_ref, vp_ref, vc_ref, vn_ref, mask_ref, o_ref, lse_ref):
    tq = q_ref.shape[0]
    masks = _pair_masks()
    lane = lax.broadcasted_iota(jnp.int32, (1, LANES), 1)
    lse_tiles = [jnp.zeros((B_HALF, LANES), F32) for _ in range(tq // B_HALF)]
    units = [(p, hf) for p in range(HEADS_AB // 2) for hf in range(tq // B_HALF)]

    def window(prev_ref, cur_ref, next_ref, hf, sl):
        lo, hi = hf * B_HALF - B_SIDE, (hf + 1) * B_HALF + B_SIDE
        parts = []
        if lo < 0:
            parts.append(prev_ref[tq + lo:, sl])
        parts.append(cur_ref[max(lo, 0):min(hi, tq), sl])
        if hi > tq:
            parts.append(next_ref[:hi - tq, sl])
        return jnp.concatenate(parts, axis=0)

    def scores(unit):
        p, hf = unit
        sl = slice(p * LANES, (p + 1) * LANES)
        qp = q_ref[hf * B_HALF:(hf + 1) * B_HALF, sl]
        stacked = jnp.concatenate([jnp.where(mk, qp, jnp.zeros_like(qp)) for mk in masks], axis=0)
        s = _dot_nt(stacked, window(kp_ref, kc_ref, kn_ref, hf, sl)) + mask_ref[0, hf]
        return s, s.max(axis=-1, keepdims=True)

    def finish(unit, scored):
        s, m = scored
        p, hf = unit
        sl = slice(p * LANES, (p + 1) * LANES)
        pr = jnp.exp2(s - m)
        l = pr.sum(axis=-1, keepdims=True)
        o = jnp.dot(pr.astype(BF16), window(vp_ref, vc_ref, vn_ref, hf, sl), preferred_element_type=F32) / l
        o_ref[hf * B_HALF:(hf + 1) * B_HALF, sl] = jnp.where(masks[0], o[:B_HALF], o[B_HALF:]).astype(BF16)
        lse = m + jnp.log2(l)
        tile = jnp.where(lane == 2 * p, lse[:B_HALF], lse_tiles[hf])
        lse_tiles[hf] = jnp.where(lane == 2 * p + 1, lse[B_HALF:], tile)

    _pipelined(units, scores, finish, ahead=B_AHEAD)
    for hf, tile in enumerate(lse_tiles):
        lse_ref[hf * B_HALF:(hf + 1) * B_HALF, :] = tile


def _attn_b_masks():
    qq = np.arange(B_HALF)[:, None]
    jj = np.arange(B_HALF + 2 * B_SIDE)[None, :]
    band = (jj - qq >= 0) & (jj - qq <= 2 * B_SIDE)
    out = []
    for ty in range(4):
        per_half = []
        for hf in range(B_TQ // B_HALF):
            pos = hf * B_HALF - B_SIDE + jj
            ok = band
            if ty & 1:
                ok = ok & (pos >= 0)
            if ty & 2:
                ok = ok & (pos < B_TQ)
            half = np.where(ok, 0.0, MASK_NEG)
            per_half.append(np.concatenate([half, half], axis=0))
        out.append(np.stack(per_half))
    return np.stack(out).astype(np.float32)


def _attn_b_group(qkv, masks, *, group):
    batch, dil, sub, _ = qkv.shape
    tq = B_TQ
    nb = sub // tq
    assert sub % tq == 0
    q_spec = pl.BlockSpec((None, None, tq, D_MODEL), lambda b, r, i: (b, r, i, 0))
    def kv_spec(which, off):
        return pl.BlockSpec((None, None, tq, D_MODEL),
                            lambda b, r, i: (b, r, jnp.clip(i + off, 0, nb - 1), which))
    mask_spec = pl.BlockSpec((1,) + masks.shape[1:],
                             lambda b, r, i: ((i == 0).astype(jnp.int32) + 2 * (i == nb - 1).astype(jnp.int32),
                                              0, 0, 0))
    return pl.pallas_call(
        _attn_b_kernel,
        grid=(batch, dil, nb),
        in_specs=[q_spec, kv_spec(1, -1), kv_spec(1, 0), kv_spec(1, 1),
                  kv_spec(2, -1), kv_spec(2, 0), kv_spec(2, 1), mask_spec],
        out_specs=[pl.BlockSpec((None, None, tq, D_MODEL), lambda b, r, i: (b, r, i, 0)),
                   pl.BlockSpec((None, None, tq, LANES), lambda b, r, i: (b, r, i, 0))],
        out_shape=[jax.ShapeDtypeStruct((batch, dil, sub, D_MODEL), BF16),
                   jax.ShapeDtypeStruct((batch, dil, sub, LANES), F32)],
        compiler_params=_cparams(("parallel", "parallel", "arbitrary")),
        name="attn_dilated_g%d" % group,
    )(qkv, qkv, qkv, qkv, qkv, qkv, qkv, masks)


def _attn_c_kernel(q_ref, k_ref, vt_ref, o_ref, s_ref, smax_ref, m_ref, acc_ref, *, tk):
    nk = k_ref.shape[0] // tk
    m_ref[...] = jnp.full(m_ref.shape, MASK_NEG, F32)
    acc_ref[...] = jnp.zeros(acc_ref.shape, F32)

    def issue_scores(h, c, slot):
        start = pl.multiple_of(c * tk, tk)
        s = _dot_nt(k_ref[pl.ds(start, tk), :], q_ref[:, h * C_HEAD_DIM:(h + 1) * C_HEAD_DIM])
        s_ref[slot] = s
        smax_ref[slot] = s.max(axis=0, keepdims=True)

    issue_scores(0, 0, 0)

    def chunk(c, carry):
        vt = vt_ref[c]
        for h in range(C_GROUP):
            if h + 1 < C_GROUP:
                issue_scores(h + 1, c, (h + 1) % 2)
            else:
                issue_scores(0, jnp.minimum(c + 1, nk - 1), (h + 1) % 2)
            s = s_ref[h % 2]
            m_prev = m_ref[h]
            m_new = jnp.maximum(m_prev, smax_ref[h % 2])
            alpha = jnp.exp2(m_prev - m_new)
            pr = jnp.exp2(s - m_new)
            acc_ref[h] = alpha * acc_ref[h] + jnp.dot(vt, pr.astype(BF16), preferred_element_type=F32)
            m_ref[h] = m_new
        return carry

    lax.fori_loop(0, nk, chunk, 0)
    for h in range(C_GROUP):
        acc = acc_ref[h]
        out = acc[:C_HEAD_DIM] / acc[C_HEAD_DIM:C_HEAD_DIM + 1]
        o_ref[:, h * C_HEAD_DIM:(h + 1) * C_HEAD_DIM] = out.T.astype(BF16)


def _attn_c(qk, vt, *, batch, seq):
    tq, tk = C_TQ, C_TK
    nq = seq // tq
    assert seq % tk == 0 and C_GROUP % 2 == 0
    qw = C_GROUP * C_HEAD_DIM
    return pl.pallas_call(
        functools.partial(_attn_c_kernel, tk=tk),
        grid=(batch, C_KV_HEADS, nq),
        in_specs=[
            pl.BlockSpec((tq, qw), lambda b, g, i: (b * nq + i, g)),
            pl.BlockSpec((seq, C_HEAD_DIM), lambda b, g, i: (b, C_Q_HEADS + g)),
            pl.BlockSpec((None, seq // tk, C_VT_ROWS, tk), lambda b, g, i: (b, 0, g, 0)),
        ],
        out_specs=pl.BlockSpec((tq, qw), lambda b, g, i: (b * nq + i, g)),
        out_shape=jax.ShapeDtypeStruct((batch * seq, C_Q_HEADS * C_HEAD_DIM), BF16),
        scratch_shapes=[pltpu.VMEM((2, tk, tq), F32), pltpu.VMEM((2, 1, tq), F32),
                        pltpu.VMEM((C_GROUP, 1, tq), F32),
                        pltpu.VMEM((C_GROUP, C_VT_ROWS, tq), F32)],
        compiler_params=_cparams(("parallel", "parallel", "arbitrary")),
        name="attn_gqa_flash",
    )(qk, qk, vt)


def _post_kernel(*refs, merge, final, halves, tf, lead):
    refs = list(refs)
    x_ref = refs.pop(0)
    if merge:
        o_refs = [refs.pop(0) for _ in range(3)]
        lse_refs = [refs.pop(0) for _ in range(3)]
    else:
        o_ref_in = refs.pop(0)
    mod_ref, g_ref, wo_ref, w1_ref, w2_ref = [refs.pop(0) for _ in range(5)]
    fg_ref = refs.pop(0) if final else None
    out_ref = refs.pop(0)
    mod = mod_ref[0]
    hm = x_ref.shape[0] // halves

    def token_order(ref, hf, c):
        dil = ref.shape[0]
        rows = hm // dil
        part = ref[:, hf * rows:(hf + 1) * rows, c * LANES:(c + 1) * LANES].astype(F32)
        return part[0] if dil == 1 else jnp.swapaxes(part, 0, 1).reshape(hm, LANES)

    def merged_groups(hf):
        lse = [token_order(r, hf, 0) for r in lse_refs]
        top = jnp.maximum(jnp.maximum(lse[0], lse[1]), lse[2])
        ex = [jnp.exp2(v - top) for v in lse]
        den = ex[0] + ex[1] + ex[2]
        wgt = [e / den for e in ex]
        lane = lax.broadcasted_iota(jnp.int32, (hm, LANES), 1)
        cols = []
        for c in range(D_MODEL // LANES):
            head_of_lane = lane // HEAD_DIM_AB + c * (LANES // HEAD_DIM_AB)
            mixed = None
            for g in range(3):
                term = jnp.take_along_axis(wgt[g], head_of_lane, axis=1) * token_order(o_refs[g], hf, c)
                mixed = term if mixed is None else mixed + term
            cols.append(mixed.astype(BF16))
        return jnp.concatenate(cols, axis=1)

    def pre(hf):
        rows = slice(hf * hm, (hf + 1) * hm)
        o = merged_groups(hf) if merge else o_ref_in[rows, :]
        mix = jnp.dot(o, wo_ref[...], preferred_element_type=F32)
        x1 = x_ref[rows, :] + mod[:, 2 * D_MODEL:3 * D_MODEL] * mix
        h2 = _norm_mod(x1, g_ref[0], mod[:, 4 * D_MODEL:5 * D_MODEL], mod[:, 3 * D_MODEL:4 * D_MODEL])
        return x1, h2.astype(BF16)

    def mlp(h2, acc, chunks):
        for c in chunks:
            a = jnp.dot(h2, w1_ref[:, c * tf:(c + 1) * tf], preferred_element_type=F32)
            a = jnp.square(jnp.maximum(a, 0.0)).astype(BF16)
            d = jnp.dot(a, w2_ref[c * tf:(c + 1) * tf, :], preferred_element_type=F32)
            acc = d if acc is None else acc + d
        return acc

    def finish(hf, x1, acc):
        x2 = x1 + mod[:, 5 * D_MODEL:6 * D_MODEL] * acc
        if final:
            ms = jnp.mean(x2 * x2, axis=-1, keepdims=True)
            x2 = x2 * lax.rsqrt(ms + EPS) * fg_ref[...]
        out_ref[hf * hm:(hf + 1) * hm, :] = x2

    n_chunks = D_FF // tf
    cur = pre(0)
    for hf in range(halves):
        x1, h2 = cur
        acc = mlp(h2, None, range(0, lead))
        if hf + 1 < halves:
            cur = pre(hf + 1)
        finish(hf, x1, mlp(h2, acc, range(lead, n_chunks)))


def _post(x2d, attn, mods3, norm_g3, wo, w1, w2, final_g, *, layer, mod_row0, seq, merge, final):
    t = x2d.shape[0]
    tm = TM_POST_MERGE if merge else TM_POST
    hm = tm // POST_HALVES
    per_seq = seq // tm
    row = lambda i: (i, 0)
    const = lambda i: (0, 0)
    resident = pl.Buffered(1)
    in_specs = [pl.BlockSpec((tm, D_MODEL), row)]
    args = [x2d]
    scratch = []
    if merge:
        outs, lses = attn
        def res_spec(arr):
            dil, width = arr.shape[1], arr.shape[3]
            assert hm % (dil * 16) == 0
            return pl.BlockSpec((None, dil, tm // dil, width), lambda i: (i // per_seq, 0, i % per_seq, 0))
        in_specs += [res_spec(a) for a in outs] + [res_spec(a) for a in lses]
        args += list(outs) + list(lses)
    else:
        in_specs += [pl.BlockSpec((tm, D_MODEL), row)]
        args += [attn]
    in_specs += [
        pl.BlockSpec((1, 1, 6 * D_MODEL), lambda i: (layer * MOD_ROWS + mod_row0 + i // per_seq, 0, 0)),
        pl.BlockSpec((1, 1, D_MODEL), lambda i: (2 * layer + 1, 0, 0)),
        pl.BlockSpec((D_MODEL, D_MODEL), const, pipeline_mode=resident),
        pl.BlockSpec((D_MODEL, D_FF), const, pipeline_mode=resident),
        pl.BlockSpec((D_FF, D_MODEL), const, pipeline_mode=resident),
    ]
    args += [mods3, norm_g3, wo, w1, w2]
    if final:
        in_specs += [pl.BlockSpec((1, D_MODEL), const)]
        args += [final_g.reshape(1, D_MODEL)]
    return pl.pallas_call(
        functools.partial(_post_kernel, merge=merge, final=final, halves=POST_HALVES, tf=TF_POST,
                          lead=POST_LEAD_MERGE if merge else POST_LEAD),
        grid=(t // tm,),
        in_specs=in_specs,
        out_specs=pl.BlockSpec((tm, D_MODEL), row),
        out_shape=jax.ShapeDtypeStruct((t, D_MODEL), F32),
        scratch_shapes=scratch,
        compiler_params=_cparams(("parallel",)),
        name="wo_mlp_merge" if merge else "wo_mlp",
    )(*args)


def _trunk(x, mods3, mod_row0, norm_g3, final_g, wts, a_bias, b_masks, tabs_b, tabs_c):
    batch, seq, _ = x.shape
    assert seq <= tabs_b[0].shape[0]
    x2d = x.reshape(batch * seq, D_MODEL)
    for layer in range(DEPTH):
        kind, j = layer % N_MIXERS, layer // N_MIXERS
        common = dict(layer=layer, mod_row0=mod_row0, seq=seq)
        if kind == 0:
            qkv = _qkv_proj(x2d, mods3, norm_g3, wts["a_qkv"][j], mode="a", **common)
            attn = _attn_a(qkv, a_bias[j], batch=batch, seq=seq)
            wo = wts["a_o"][j]
        elif kind == 1:
            groups = []
            for g, (win, dil) in enumerate(B_PAIRS):
                assert win == 2 * B_SIDE * dil
                qkv = _qkv_proj(x2d, mods3, norm_g3, wts["b_qkv"][j], mode="b", cos=tabs_b[0], sin=tabs_b[1],
                                group=g, dil=dil, **common)
                groups.append(_attn_b_group(qkv, b_masks, group=g))
            attn = ([o for o, _ in groups], [l for _, l in groups])
            wo = wts["b_o"][j]
        else:
            qk, vt = _qkv_proj(x2d, mods3, norm_g3, wts["c_qkv"][j], mode="c", cos=tabs_c[0], sin=tabs_c[1],
                               qg=wts["c_qg"][j], kg=wts["c_kg"][j], **common)
            attn = _attn_c(qk, vt, batch=batch, seq=seq)
            wo = wts["c_o"][j]
        x2d = _post(x2d, attn, mods3, norm_g3, wo, wts["w1"][layer], wts["w2"][layer], final_g,
                    merge=(kind == 1), final=(layer == DEPTH - 1), **common)
    return x2d.reshape(batch, seq, D_MODEL)


def kernel(x_prompt, x_sample, c_prompt, c_sample, w_mod, b_mod, norm_g, final_g, a_w_qkv, a_rpb, a_w_o,
           b_w_qkv, b_w_o, c_w_qkv, c_q_g, c_k_g, c_w_o, mlp_w1, mlp_w2):
    nb_p, nb_s = c_prompt.shape[0], c_sample.shape[0]
    assert nb_p + nb_s <= MOD_ROWS
    c_all = jnp.concatenate([c_prompt, c_sample, jnp.zeros((MOD_ROWS - nb_p - nb_s, D_MODEL), F32)], axis=0)
    mods3 = _modulation(c_all, w_mod, b_mod).reshape(DEPTH * MOD_ROWS, 1, 6 * D_MODEL)
    norm_g3 = norm_g.reshape(DEPTH * 2, 1, D_MODEL)

    scale_ab = HEAD_DIM_AB ** -0.5 * LOG2E
    col_scale = np.ones((3 * D_MODEL,), np.float32)
    col_scale[:D_MODEL] = scale_ab
    a_qkv = (a_w_qkv * col_scale).astype(BF16)
    b_qkv = (b_w_qkv * np.tile(col_scale, len(B_PAIRS))).astype(BF16)
    wts = {
        "a_qkv": a_qkv, "a_o": a_w_o.astype(BF16),
        "b_qkv": b_qkv, "b_o": b_w_o.astype(BF16),
        "c_qkv": c_w_qkv.astype(BF16), "c_o": c_w_o.astype(BF16),
        "c_qg": (c_q_g * (C_HEAD_DIM ** -0.5 * LOG2E)).reshape(-1, 1, C_HEAD_DIM),
        "c_kg": c_k_g.reshape(-1, 1, C_HEAD_DIM),
        "w1": mlp_w1.astype(BF16), "w2": mlp_w2.astype(BF16),
    }
    a_bias = [_attn_a_bias(a_rpb[j]) for j in range(a_rpb.shape[0])]
    b_masks = jnp.asarray(_attn_b_masks())

    tabs_b, tabs_c = _rope_tables(max(x_prompt.shape[1], x_sample.shape[1]))
    shared = (norm_g3, final_g, wts, a_bias, b_masks, tabs_b, tabs_c)
    y_prompt = _trunk(x_prompt, mods3, 0, *shared)
    y_sample = _trunk(x_sample, mods3, nb_p, *shared)
    return (y_prompt, y_sample)
```

```python
import functools

import numpy as np
import jax
import jax.numpy as jnp
from jax import lax
from jax.experimental import pallas as pl
from jax.experimental.pallas import tpu as pltpu

F32 = jnp.float32
BF16 = jnp.bfloat16

D_MODEL = 1024
DEPTH = 4
GRID_W = 64
D_FF = 4 * D_MODEL
EPS = 1e-6
ROPE_THETA = 10000.0
N_MIXERS = 3
HEAD_DIM_AB = 64
HEADS_AB = D_MODEL // HEAD_DIM_AB
A_WIN_ROWS = 8
A_WIN_COLS = 16
B_PAIRS = ((128, 1), (512, 4), (2048, 16))
B_SIDE = 64
C_HEAD_DIM = 128
C_Q_HEADS = 8
C_KV_HEADS = 2
C_GROUP = C_Q_HEADS // C_KV_HEADS
C_QKV_WIDTH = (C_Q_HEADS + 2 * C_KV_HEADS) * C_HEAD_DIM
ROPE_HALF = 32
LANES = 128
BF16_SUBLANES = 16
MOD_ROWS = 8
MASK_NEG = -1e30
LOG2E = float(np.log2(np.e))
VMEM_LIMIT = 56 * 1024 * 1024

TN_MOD = 1536
TM_QKV = 1024
TN_QKV = 256
TN_QKV_C = 512
QKV_HALVES = 2
TM_POST = 1024
TM_POST_MERGE = 512
POST_HALVES = 2
TF_POST = 1024
POST_LEAD = 0
POST_LEAD_MERGE = 2
A_TQ = 4 * GRID_W
A_AHEAD = 3
B_AHEAD = 1
B_TQ = 256
B_HALF = 128
C_TQ = 512
C_TK = 1024
C_VT_ROWS = C_HEAD_DIM + BF16_SUBLANES


def _cparams(sem):
    return pltpu.CompilerParams(dimension_semantics=sem, vmem_limit_bytes=VMEM_LIMIT)


def _mod_kernel(c_ref, w_ref, b_ref, o_ref):
    c = c_ref[...]
    act = (c / (1.0 + jnp.exp(-c))).astype(BF16)
    o_ref[0] = jnp.dot(act, w_ref[0].astype(BF16), preferred_element_type=F32) + b_ref[0]


def _modulation(c_all, w_mod, b_mod):
    tn = TN_MOD
    n = 6 * D_MODEL
    return pl.pallas_call(
        _mod_kernel,
        grid=(DEPTH, n // tn),
        in_specs=[
            pl.BlockSpec((MOD_ROWS, D_MODEL), lambda l, j: (0, 0)),
            pl.BlockSpec((1, D_MODEL, tn), lambda l, j: (l, 0, j)),
            pl.BlockSpec((1, 1, tn), lambda l, j: (l, 0, j)),
        ],
        out_specs=pl.BlockSpec((1, MOD_ROWS, tn), lambda l, j: (l, 0, j)),
        out_shape=jax.ShapeDtypeStruct((DEPTH, MOD_ROWS, n), F32),
        compiler_params=_cparams(("parallel", "parallel")),
        name="adaln_modulation",
    )(c_all, w_mod, b_mod.reshape(DEPTH, 1, n))


def _norm_mod(x, gain, scale, shift):
    ms = jnp.mean(x * x, axis=-1, keepdims=True)
    return (x * lax.rsqrt(ms + EPS) * gain) * (1.0 + scale) + shift


def _rope128(x, cos, sin_signed, low_half):
    up = pltpu.roll(x, LANES - ROPE_HALF, 1)
    down = pltpu.roll(x, ROPE_HALF, 1)
    return x * cos + jnp.where(low_half, up, down) * sin_signed


def _low_half_mask():
    lane = lax.broadcasted_iota(jnp.int32, (1, LANES), 1)
    return (lane % (2 * ROPE_HALF)) < ROPE_HALF


def _rope_tables(seq):
    inv = ROPE_THETA ** (-jnp.arange(ROPE_HALF, dtype=F32) / ROPE_HALF)
    rows = seq // GRID_W

    def trig(pos):
        ang = pos.astype(F32)[:, None] * inv[None, :]
        return jnp.cos(ang), jnp.sin(ang)

    per_row = lambda x: jnp.repeat(x, GRID_W, axis=0)
    per_col = lambda x: jnp.tile(x, (rows, 1))
    c_row, s_row = (per_row(x) for x in trig(jnp.arange(rows)))
    c_col, s_col = (per_col(x) for x in trig(jnp.arange(GRID_W)))
    c_blk, s_blk = (per_row(x) for x in trig(jnp.arange(rows) * GRID_W))
    c_tok = c_blk * c_col - s_blk * s_col
    s_tok = s_blk * c_col + c_blk * s_col

    def table(ca, sa, cb, sb):
        return jnp.concatenate([ca, ca, cb, cb], axis=1), jnp.concatenate([-sa, sa, -sb, sb], axis=1)

    return table(c_tok, s_tok, c_tok, s_tok), table(c_row, s_row, c_col, s_col)


def _qkv_kernel(*refs, mode, dil, halves, tn):
    if mode == "a":
        x_ref, mod_ref, g_ref, w_ref, o_ref = refs
    elif mode == "b":
        x_ref, mod_ref, g_ref, w_ref, cos_ref, sin_ref, o_ref = refs
    else:
        x_ref, mod_ref, g_ref, w_ref, cos_ref, sin_ref, qg_ref, kg_ref, o_ref, vt_ref = refs
    mod = mod_ref[0]
    hm = x_ref.shape[0] // halves
    n = w_ref.shape[1]
    low = _low_half_mask()

    def normed(hf):
        rows = slice(hf * hm, (hf + 1) * hm)
        return _norm_mod(x_ref[rows, :], g_ref[0], mod[:, D_MODEL:2 * D_MODEL], mod[:, 0:D_MODEL]).astype(BF16)

    def emit(hf, j, acc):
        rows = slice(hf * hm, (hf + 1) * hm)
        if mode == "a":
            o_ref[rows, j * tn:(j + 1) * tn] = acc.astype(BF16)
            return
        cos, sin = cos_ref[rows, :], sin_ref[rows, :]
        for c in range(tn // LANES):
            col = j * tn + c * LANES
            val = acc[:, c * LANES:(c + 1) * LANES]
            if mode == "c":
                head = col // C_HEAD_DIM
                if head < C_Q_HEADS + C_KV_HEADS:
                    gain = qg_ref[...] if head < C_Q_HEADS else kg_ref[...]
                    ms = jnp.mean(val * val, axis=-1, keepdims=True)
                    val = _rope128(val * lax.rsqrt(ms + EPS) * gain, cos, sin, low)
                    o_ref[rows, col:col + LANES] = val.astype(BF16)
                else:
                    kv = head - C_Q_HEADS - C_KV_HEADS
                    vt_ref[kv * C_VT_ROWS:kv * C_VT_ROWS + C_HEAD_DIM, rows] = val.T.astype(BF16)
                    vt_ref[kv * C_VT_ROWS + C_HEAD_DIM:(kv + 1) * C_VT_ROWS, rows] = jnp.ones(
                        (C_VT_ROWS - C_HEAD_DIM, hm), BF16)
                continue
            if col < 2 * D_MODEL:
                val = _rope128(val, cos, sin, low)
            sub = hm // dil
            dst = slice(hf * sub, (hf + 1) * sub)
            if dil == 1:
                o_ref[0, 0, dst, col:col + LANES] = val.astype(BF16)
                continue
            by_residue = jnp.swapaxes(val.reshape(sub, dil, LANES), 0, 1)
            o_ref[0, :, dst, col:col + LANES] = by_residue.astype(BF16)

    hs = [normed(hf) for hf in range(halves)]
    for hf in range(halves):
        for j in range(n // tn):
            emit(hf, j, jnp.dot(hs[hf], w_ref[:, j * tn:(j + 1) * tn], preferred_element_type=F32))


def _qkv_proj(x2d, mods3, norm_g3, w, *, layer, mod_row0, seq, mode, cos=None, sin=None, qg=None, kg=None,
              group=0, dil=1):
    t = x2d.shape[0]
    tm = TM_QKV
    if mode == "b":
        n, col0 = 3 * D_MODEL, group
    else:
        n, col0 = w.shape[1], 0
    tn = TN_QKV_C if mode == "c" else TN_QKV
    hm = tm // QKV_HALVES
    per_seq = seq // tm
    in_specs = [
        pl.BlockSpec((tm, D_MODEL), lambda i: (i, 0)),
        pl.BlockSpec((1, 1, 6 * D_MODEL), lambda i: (layer * MOD_ROWS + mod_row0 + i // per_seq, 0, 0)),
        pl.BlockSpec((1, 1, D_MODEL), lambda i: (2 * layer, 0, 0)),
        pl.BlockSpec((D_MODEL, n), lambda i: (0, col0), pipeline_mode=pl.Buffered(1)),
    ]
    args = [x2d, mods3, norm_g3, w]
    if mode in ("b", "c"):
        in_specs += [pl.BlockSpec((tm, LANES), lambda i: (i % per_seq, 0))] * 2
        args += [cos, sin]
    if mode == "c":
        in_specs += [pl.BlockSpec((1, C_HEAD_DIM), lambda i: (0, 0))] * 2
        args += [qg, kg]
    scratch = []
    if mode == "b":
        assert hm % (dil * BF16_SUBLANES) == 0
        batch = t // seq
        out_spec = pl.BlockSpec((1, dil, tm // dil, n), lambda i: (i // per_seq, 0, i % per_seq, 0))
        out_shape = jax.ShapeDtypeStruct((batch, dil, seq // dil, n), BF16)
    elif mode == "c":
        assert tm == C_TK
        n_qk = (C_Q_HEADS + C_KV_HEADS) * C_HEAD_DIM
        kv_w = C_KV_HEADS * C_VT_ROWS
        out_spec = [pl.BlockSpec((tm, n_qk), lambda i: (i, 0)),
                    pl.BlockSpec((None, None, kv_w, tm), lambda i: (i // per_seq, i % per_seq, 0, 0))]
        out_shape = [jax.ShapeDtypeStruct((t, n_qk), BF16),
                     jax.ShapeDtypeStruct((t // seq, per_seq, kv_w, tm), BF16)]
    else:
        out_spec = pl.BlockSpec((tm, n), lambda i: (i, 0))
        out_shape = jax.ShapeDtypeStruct((t, n), BF16)
    return pl.pallas_call(
        functools.partial(_qkv_kernel, mode=mode, dil=dil, halves=QKV_HALVES, tn=tn),
        grid=(t // tm,),
        in_specs=in_specs,
        out_specs=out_spec,
        out_shape=out_shape,
        scratch_shapes=scratch,
        compiler_params=_cparams(("parallel",)),
        name="norm_qkv_" + mode,
    )(*args)


def _pair_masks():
    lane = lax.broadcasted_iota(jnp.int32, (1, LANES), 1)
    first = lane < HEAD_DIM_AB
    return first, jnp.logical_not(first)


def _dot_nt(a, b):
    return lax.dot_general(a, b, (((1,), (1,)), ((), ())), preferred_element_type=F32)


def _pipelined(units, scores, finish, ahead=1):
    pending = [scores(u) for u in units[:ahead]]
    for i, unit in enumerate(units):
        if i + ahead < len(units):
            pending.append(scores(units[i + ahead]))
        finish(unit, pending.pop(0))


def _attn_a_kernel(q_ref, kbuf, vbuf, bias_ref, o_ref):
    tq = q_ref.shape[0]
    rows_per = tq // GRID_W
    win_keys = A_WIN_ROWS * GRID_W
    first = pl.program_id(1) == 0
    last = pl.program_id(1) == pl.num_programs(1) - 1
    masks = _pair_masks()
    units = [(p, rl) for p in range(HEADS_AB // 2) for rl in range(rows_per)]

    def window_of(rl):
        centred = A_WIN_ROWS - 1 - A_WIN_ROWS // 2
        start = jnp.where(first, 0, jnp.where(last, 3 * rows_per - A_WIN_ROWS,
                                              rl + rows_per - A_WIN_ROWS // 2))
        win = jnp.where(first, A_WIN_ROWS - 1 - rl, jnp.where(last, rows_per - 1 - rl, centred))
        return pl.multiple_of(start * GRID_W, GRID_W), win

    def scores(unit):
        p, rl = unit
        sl = slice(p * LANES, (p + 1) * LANES)
        start, win = window_of(rl)
        qp = q_ref[rl * GRID_W:(rl + 1) * GRID_W, sl]
        stacked = jnp.concatenate([jnp.where(mk, qp, jnp.zeros_like(qp)) for mk in masks], axis=0)
        s = _dot_nt(stacked, kbuf[pl.ds(start, win_keys), sl]) + bias_ref[p, win]
        return s, s.max(axis=-1, keepdims=True)

    def finish(unit, scored):
        s, m = scored
        p, rl = unit
        sl = slice(p * LANES, (p + 1) * LANES)
        start, _ = window_of(rl)
        pr = jnp.exp2(s - m)
        l = pr.sum(axis=-1, keepdims=True)
        o = jnp.dot(pr.astype(BF16), vbuf[pl.ds(start, win_keys), sl], preferred_element_type=F32) / l
        o_ref[rl * GRID_W:(rl + 1) * GRID_W, sl] = jnp.where(masks[0], o[:GRID_W], o[GRID_W:]).astype(BF16)

    _pipelined(units, scores, finish, ahead=A_AHEAD)


def _attn_a_bias(rpb):
    n_dc = 2 * A_WIN_COLS - 1
    c, kc = np.arange(GRID_W)[:, None], np.arange(GRID_W)[None, :]
    onehot_c = ((kc - c + A_WIN_COLS - 1)[..., None] == np.arange(n_dc)).astype(np.float32)
    toe = jnp.einsum("hde,cje->hdcj", rpb * LOG2E, onehot_c, precision=lax.Precision.HIGHEST)
    cs = np.clip(c - A_WIN_COLS // 2, 0, GRID_W - A_WIN_COLS)
    col_ok = (kc >= cs) & (kc < cs + A_WIN_COLS)
    toe = jnp.where(jnp.asarray(col_ok)[None, None], toe, MASK_NEG)
    wins = jnp.stack([jnp.concatenate([toe[:, w + m] for m in range(A_WIN_ROWS)], axis=-1)
                      for w in range(A_WIN_ROWS)], axis=1)
    wins = wins.reshape(HEADS_AB // 2, 2, A_WIN_ROWS, GRID_W, A_WIN_ROWS * GRID_W)
    return jnp.transpose(wins, (0, 2, 1, 3, 4)).reshape(HEADS_AB // 2, A_WIN_ROWS, 2 * GRID_W, A_WIN_ROWS * GRID_W)


def _attn_a(qkv, bias, *, batch, seq):
    tq = A_TQ
    nb = seq // tq
    rows_per = tq // GRID_W
    assert nb >= 3 and A_WIN_ROWS // 2 <= rows_per <= A_WIN_ROWS
    q_spec = pl.BlockSpec((tq, D_MODEL), lambda b, i: (b * nb + i, 0))
    def kv_spec(col):
        def index(b, i):
            row = b * seq + jnp.clip((i - 1) * tq, 0, seq - 3 * tq)
            return pl.multiple_of(row, tq), col * D_MODEL
        return pl.BlockSpec((pl.Element(3 * tq), pl.Element(D_MODEL)), index)
    bias_spec = pl.BlockSpec(bias.shape, lambda b, i: (0, 0, 0, 0), pipeline_mode=pl.Buffered(1))
    return pl.pallas_call(
        _attn_a_kernel,
        grid=(batch, nb),
        in_specs=[q_spec, kv_spec(1), kv_spec(2), bias_spec],
        out_specs=pl.BlockSpec((tq, D_MODEL), lambda b, i: (b * nb + i, 0)),
        out_shape=jax.ShapeDtypeStruct((batch * seq, D_MODEL), BF16),
        compiler_params=_cparams(("parallel", "arbitrary")),
        name="attn_neighbourhood",
    )(qkv, qkv, qkv, bias)


def _attn_b_kernel(q_ref, kp_ref, kc_ref, kn_ref, vp_ref, vc_ref, vn_ref, mask_ref, o_ref, lse_ref):
    tq = q_ref.shape[0]
    masks = _pair_masks()
    lane = lax.broadcasted_iota(jnp.int32, (1, LANES), 1)
    lse_tiles = [jnp.zeros((B_HALF, LANES), F32) for _ in range(tq // B_HALF)]
    units = [(p, hf) for p in range(HEADS_AB // 2) for hf in range(tq // B_HALF)]

    def window(prev_ref, cur_ref, next_ref, hf, sl):
        lo, hi = hf * B_HALF - B_SIDE, (hf + 1) * B_HALF + B_SIDE
        parts = []
        if lo < 0:
            parts.append(prev_ref[tq + lo:, sl])
        parts.append(cur_ref[max(lo, 0):min(hi, tq), sl])
        if hi > tq:
            parts.append(next_ref[:hi - tq, sl])
        return jnp.concatenate(parts, axis=0)

    def scores(unit):
        p, hf = unit
        sl = slice(p * LANES, (p + 1) * LANES)
        qp = q_ref[hf * B_HALF:(hf + 1) * B_HALF, sl]
        stacked = jnp.concatenate([jnp.where(mk, qp, jnp.zeros_like(qp)) for mk in masks], axis=0)
        s = _dot_nt(stacked, window(kp_ref, kc_ref, kn_ref, hf, sl)) + mask_ref[0, hf]
        return s, s.max(axis=-1, keepdims=True)

    def finish(unit, scored):
        s, m = scored
        p, hf = unit
        sl = slice(p * LANES, (p + 1) * LANES)
        pr = jnp.exp2(s - m)
        l = pr.sum(axis=-1, keepdims=True)
        o = jnp.dot(pr.astype(BF16), window(vp_ref, vc_ref, vn_ref, hf, sl), preferred_element_type=F32) / l
        o_ref[hf * B_HALF:(hf + 1) * B_HALF, sl] = jnp.where(masks[0], o[:B_HALF], o[B_HALF:]).astype(BF16)
        lse = m + jnp.log2(l)
        tile = jnp.where(lane == 2 * p, lse[:B_HALF], lse_tiles[hf])
        lse_tiles[hf] = jnp.where(lane == 2 * p + 1, lse[B_HALF:], tile)

    _pipelined(units, scores, finish, ahead=B_AHEAD)
    for hf, tile in enumerate(lse_tiles):
        lse_ref[hf * B_HALF:(hf + 1) * B_HALF, :] = tile


def _attn_b_masks():
    qq = np.arange(B_HALF)[:, None]
    jj = np.arange(B_HALF + 2 * B_SIDE)[None, :]
    band = (jj - qq >= 0) & (jj - qq <= 2 * B_SIDE)
    out = []
    for ty in range(4):
        per_half = []
        for hf in range(B_TQ // B_HALF):
            pos = hf * B_HALF - B_SIDE + jj
            ok = band
            if ty & 1:
                ok = ok & (pos >= 0)
            if ty & 2:
                ok = ok & (pos < B_TQ)
            half = np.where(ok, 0.0, MASK_NEG)
            per_half.append(np.concatenate([half, half], axis=0))
        out.append(np.stack(per_half))
    return np.stack(out).astype(np.float32)


def _attn_b_group(qkv, masks, *, group):
    batch, dil, sub, _ = qkv.shape
    tq = B_TQ
    nb = sub // tq
    assert sub % tq == 0
    q_spec = pl.BlockSpec((None, None, tq, D_MODEL), lambda b, r, i: (b, r, i, 0))
    def kv_spec(which, off):
        return pl.BlockSpec((None, None, tq, D_MODEL),
                            lambda b, r, i: (b, r, jnp.clip(i + off, 0, nb - 1), which))
    mask_spec = pl.BlockSpec((1,) + masks.shape[1:],
                             lambda b, r, i: ((i == 0).astype(jnp.int32) + 2 * (i == nb - 1).astype(jnp.int32),
                                              0, 0, 0))
    return pl.pallas_call(
        _attn_b_kernel,
        grid=(batch, dil, nb),
        in_specs=[q_spec, kv_spec(1, -1), kv_spec(1, 0), kv_spec(1, 1),
                  kv_spec(2, -1), kv_spec(2, 0), kv_spec(2, 1), mask_spec],
        out_specs=[pl.BlockSpec((None, None, tq, D_MODEL), lambda b, r, i: (b, r, i, 0)),
                   pl.BlockSpec((None, None, tq, LANES), lambda b, r, i: (b, r, i, 0))],
        out_shape=[jax.ShapeDtypeStruct((batch, dil, sub, D_MODEL), BF16),
                   jax.ShapeDtypeStruct((batch, dil, sub, LANES), F32)],
        compiler_params=_cparams(("parallel", "parallel", "arbitrary")),
        name="attn_dilated_g%d" % group,
    )(qkv, qkv, qkv, qkv, qkv, qkv, qkv, masks)


def _attn_c_kernel(q_ref, k_ref, vt_ref, o_ref, s_ref, smax_ref, m_ref, acc_ref, *, tk):
    nk = k_ref.shape[0] // tk
    m_ref[...] = jnp.full(m_ref.shape, MASK_NEG, F32)
    acc_ref[...] = jnp.zeros(acc_ref.shape, F32)

    def issue_scores(h, c, slot):
        start = pl.multiple_of(c * tk, tk)
        s = _dot_nt(k_ref[pl.ds(start, tk), :], q_ref[:, h * C_HEAD_DIM:(h + 1) * C_HEAD_DIM])
        s_ref[slot] = s
        smax_ref[slot] = s.max(axis=0, keepdims=True)

    issue_scores(0, 0, 0)

    def chunk(c, carry):
        vt = vt_ref[c]
        for h in range(C_GROUP):
            if h + 1 < C_GROUP:
                issue_scores(h + 1, c, (h + 1) % 2)
            else:
                issue_scores(0, jnp.minimum(c + 1, nk - 1), (h + 1) % 2)
            s = s_ref[h % 2]
            m_prev = m_ref[h]
            m_new = jnp.maximum(m_prev, smax_ref[h % 2])
            alpha = jnp.exp2(m_prev - m_new)
            pr = jnp.exp2(s - m_new)
            acc_ref[h] = alpha * acc_ref[h] + jnp.dot(vt, pr.astype(BF16), preferred_element_type=F32)
            m_ref[h] = m_new
        return carry

    lax.fori_loop(0, nk, chunk, 0)
    for h in range(C_GROUP):
        acc = acc_ref[h]
        out = acc[:C_HEAD_DIM] / acc[C_HEAD_DIM:C_HEAD_DIM + 1]
        o_ref[:, h * C_HEAD_DIM:(h + 1) * C_HEAD_DIM] = out.T.astype(BF16)


def _attn_c(qk, vt, *, batch, seq):
    tq, tk = C_TQ, C_TK
    nq = seq // tq
    assert seq % tk == 0 and C_GROUP % 2 == 0
    qw = C_GROUP * C_HEAD_DIM
    return pl.pallas_call(
        functools.partial(_attn_c_kernel, tk=tk),
        grid=(batch, C_KV_HEADS, nq),
        in_specs=[
            pl.BlockSpec((tq, qw), lambda b, g, i: (b * nq + i, g)),
            pl.BlockSpec((seq, C_HEAD_DIM), lambda b, g, i: (b, C_Q_HEADS + g)),
            pl.BlockSpec((None, seq // tk, C_VT_ROWS, tk), lambda b, g, i: (b, 0, g, 0)),
        ],
        out_specs=pl.BlockSpec((tq, qw), lambda b, g, i: (b * nq + i, g)),
        out_shape=jax.ShapeDtypeStruct((batch * seq, C_Q_HEADS * C_HEAD_DIM), BF16),
        scratch_shapes=[pltpu.VMEM((2, tk, tq), F32), pltpu.VMEM((2, 1, tq), F32),
                        pltpu.VMEM((C_GROUP, 1, tq), F32),
                        pltpu.VMEM((C_GROUP, C_VT_ROWS, tq), F32)],
        compiler_params=_cparams(("parallel", "parallel", "arbitrary")),
        name="attn_gqa_flash",
    )(qk, qk, vt)


def _post_kernel(*refs, merge, final, halves, tf, lead):
    refs = list(refs)
    x_ref = refs.pop(0)
    if merge:
        o_refs = [refs.pop(0) for _ in range(3)]
        lse_refs = [refs.pop(0) for _ in range(3)]
    else:
        o_ref_in = refs.pop(0)
    mod_ref, g_ref, wo_ref, w1_ref, w2_ref = [refs.pop(0) for _ in range(5)]
    fg_ref = refs.pop(0) if final else None
    out_ref = refs.pop(0)
    mod = mod_ref[0]
    hm = x_ref.shape[0] // halves

    def token_order(ref, hf, c):
        dil = ref.shape[0]
        rows = hm // dil
        part = ref[:, hf * rows:(hf + 1) * rows, c * LANES:(c + 1) * LANES].astype(F32)
        return part[0] if dil == 1 else jnp.swapaxes(part, 0, 1).reshape(hm, LANES)

    def merged_groups(hf):
        lse = [token_order(r, hf, 0) for r in lse_refs]
        top = jnp.maximum(jnp.maximum(lse[0], lse[1]), lse[2])
        ex = [jnp.exp2(v - top) for v in lse]
        den = ex[0] + ex[1] + ex[2]
        wgt = [e / den for e in ex]
        lane = lax.broadcasted_iota(jnp.int32, (hm, LANES), 1)
        cols = []
        for c in range(D_MODEL // LANES):
            head_of_lane = lane // HEAD_DIM_AB + c * (LANES // HEAD_DIM_AB)
            mixed = None
            for g in range(3):
                term = jnp.take_along_axis(wgt[g], head_of_lane, axis=1) * token_order(o_refs[g], hf, c)
                mixed = term if mixed is None else mixed + term
            cols.append(mixed.astype(BF16))
        return jnp.concatenate(cols, axis=1)

    def pre(hf):
        rows = slice(hf * hm, (hf + 1) * hm)
        o = merged_groups(hf) if merge else o_ref_in[rows, :]
        mix = jnp.dot(o, wo_ref[...], preferred_element_type=F32)
        x1 = x_ref[rows, :] + mod[:, 2 * D_MODEL:3 * D_MODEL] * mix
        h2 = _norm_mod(x1, g_ref[0], mod[:, 4 * D_MODEL:5 * D_MODEL], mod[:, 3 * D_MODEL:4 * D_MODEL])
        return x1, h2.astype(BF16)

    def mlp(h2, acc, chunks):
        for c in chunks:
            a = jnp.dot(h2, w1_ref[:, c * tf:(c + 1) * tf], preferred_element_type=F32)
            a = jnp.square(jnp.maximum(a, 0.0)).astype(BF16)
            d = jnp.dot(a, w2_ref[c * tf:(c + 1) * tf, :], preferred_element_type=F32)
            acc = d if acc is None else acc + d
        return acc

    def finish(hf, x1, acc):
        x2 = x1 + mod[:, 5 * D_MODEL:6 * D_MODEL] * acc
        if final:
            ms = jnp.mean(x2 * x2, axis=-1, keepdims=True)
            x2 = x2 * lax.rsqrt(ms + EPS) * fg_ref[...]
        out_ref[hf * hm:(hf + 1) * hm, :] = x2

    n_chunks = D_FF // tf
    cur = pre(0)
    for hf in range(halves):
        x1, h2 = cur
        acc = mlp(h2, None, range(0, lead))
        if hf + 1 < halves:
            cur = pre(hf + 1)
        finish(hf, x1, mlp(h2, acc, range(lead, n_chunks)))


def _post(x2d, attn, mods3, norm_g3, wo, w1, w2, final_g, *, layer, mod_row0, seq, merge, final):
    t = x2d.shape[0]
    tm = TM_POST_MERGE if merge else TM_POST
    hm = tm // POST_HALVES
    per_seq = seq // tm
    row = lambda i: (i, 0)
    const = lambda i: (0, 0)
    resident = pl.Buffered(1)
    in_specs = [pl.BlockSpec((tm, D_MODEL), row)]
    args = [x2d]
    scratch = []
    if merge:
        outs, lses = attn
        def res_spec(arr):
            dil, width = arr.shape[1], arr.shape[3]
            assert hm % (dil * BF16_SUBLANES) == 0
            return pl.BlockSpec((None, dil, tm // dil, width), lambda i: (i // per_seq, 0, i % per_seq, 0))
        in_specs += [res_spec(a) for a in outs] + [res_spec(a) for a in lses]
        args += list(outs) + list(lses)
    else:
        in_specs += [pl.BlockSpec((tm, D_MODEL), row)]
        args += [attn]
    in_specs += [
        pl.BlockSpec((1, 1, 6 * D_MODEL), lambda i: (layer * MOD_ROWS + mod_row0 + i // per_seq, 0, 0)),
        pl.BlockSpec((1, 1, D_MODEL), lambda i: (2 * layer + 1, 0, 0)),
        pl.BlockSpec((D_MODEL, D_MODEL), const, pipeline_mode=resident),
        pl.BlockSpec((D_MODEL, D_FF), const, pipeline_mode=resident),
        pl.BlockSpec((D_FF, D_MODEL), const, pipeline_mode=resident),
    ]
    args += [mods3, norm_g3, wo, w1, w2]
    if final:
        in_specs += [pl.BlockSpec((1, D_MODEL), const)]
        args += [final_g.reshape(1, D_MODEL)]
    return pl.pallas_call(
        functools.partial(_post_kernel, merge=merge, final=final, halves=POST_HALVES, tf=TF_POST,
                          lead=POST_LEAD_MERGE if merge else POST_LEAD),
        grid=(t // tm,),
        in_specs=in_specs,
        out_specs=pl.BlockSpec((tm, D_MODEL), row),
        out_shape=jax.ShapeDtypeStruct((t, D_MODEL), F32),
        scratch_shapes=scratch,
        compiler_params=_cparams(("parallel",)),
        name="wo_mlp_merge" if merge else "wo_mlp",
    )(*args)


def _trunk(x, mods3, mod_row0, norm_g3, final_g, wts, a_bias, b_masks, tabs_b, tabs_c):
    batch, seq, _ = x.shape
    assert seq <= tabs_b[0].shape[0]
    x2d = x.reshape(batch * seq, D_MODEL)
    for layer in range(DEPTH):
        kind, j = layer % N_MIXERS, layer // N_MIXERS
        common = dict(layer=layer, mod_row0=mod_row0, seq=seq)
        if kind == 0:
            qkv = _qkv_proj(x2d, mods3, norm_g3, wts["a_qkv"][j], mode="a", **common)
            attn = _attn_a(qkv, a_bias[j], batch=batch, seq=seq)
            wo = wts["a_o"][j]
        elif kind == 1:
            groups = []
            for g, (win, dil) in enumerate(B_PAIRS):
                assert win == 2 * B_SIDE * dil
                qkv = _qkv_proj(x2d, mods3, norm_g3, wts["b_qkv"][j], mode="b", cos=tabs_b[0], sin=tabs_b[1],
                                group=g, dil=dil, **common)
                groups.append(_attn_b_group(qkv, b_masks, group=g))
            attn = ([o for o, _ in groups], [l for _, l in groups])
            wo = wts["b_o"][j]
        else:
            qk, vt = _qkv_proj(x2d, mods3, norm_g3, wts["c_qkv"][j], mode="c", cos=tabs_c[0], sin=tabs_c[1],
                               qg=wts["c_qg"][j], kg=wts["c_kg"][j], **common)
            attn = _attn_c(qk, vt, batch=batch, seq=seq)
            wo = wts["c_o"][j]
        x2d = _post(x2d, attn, mods3, norm_g3, wo, wts["w1"][layer], wts["w2"][layer], final_g,
                    merge=(kind == 1), final=(layer == DEPTH - 1), **common)
    return x2d.reshape(batch, seq, D_MODEL)


def kernel(x_prompt, x_sample, c_prompt, c_sample, w_mod, b_mod, norm_g, final_g, a_w_qkv, a_rpb, a_w_o,
           b_w_qkv, b_w_o, c_w_qkv, c_q_g, c_k_g, c_w_o, mlp_w1, mlp_w2):
    nb_p, nb_s = c_prompt.shape[0], c_sample.shape[0]
    assert nb_p + nb_s <= MOD_ROWS
    c_all = jnp.concatenate([c_prompt, c_sample, jnp.zeros((MOD_ROWS - nb_p - nb_s, D_MODEL), F32)], axis=0)
    mods3 = _modulation(c_all, w_mod, b_mod).reshape(DEPTH * MOD_ROWS, 1, 6 * D_MODEL)
    norm_g3 = norm_g.reshape(DEPTH * 2, 1, D_MODEL)

    scale_ab = HEAD_DIM_AB ** -0.5 * LOG2E
    col_scale = np.ones((3 * D_MODEL,), np.float32)
    col_scale[:D_MODEL] = scale_ab
    a_qkv = (a_w_qkv * col_scale).astype(BF16)
    b_qkv = (b_w_qkv * np.tile(col_scale, len(B_PAIRS))).astype(BF16)
    wts = {
        "a_qkv": a_qkv, "a_o": a_w_o.astype(BF16),
        "b_qkv": b_qkv, "b_o": b_w_o.astype(BF16),
        "c_qkv": c_w_qkv.astype(BF16), "c_o": c_w_o.astype(BF16),
        "c_qg": (c_q_g * (C_HEAD_DIM ** -0.5 * LOG2E)).reshape(-1, 1, C_HEAD_DIM),
        "c_kg": c_k_g.reshape(-1, 1, C_HEAD_DIM),
        "w1": mlp_w1.astype(BF16), "w2": mlp_w2.astype(BF16),
    }
    a_bias = [_attn_a_bias(a_rpb[j]) for j in range(a_rpb.shape[0])]
    b_masks = jnp.asarray(_attn_b_masks())

    tabs_b, tabs_c = _rope_tables(max(x_prompt.shape[1], x_sample.shape[1]))
    shared = (norm_g3, final_g, wts, a_bias, b_masks, tabs_b, tabs_c)
    y_prompt = _trunk(x_prompt, mods3, 0, *shared)
    y_sample = _trunk(x_sample, mods3, nb_p, *shared)
    return (y_prompt, y_sample)
```

```python
import functools

import numpy as np
import jax
import jax.numpy as jnp
from jax import lax
from jax.experimental import pallas as pl
from jax.experimental.pallas import tpu as pltpu

F32 = jnp.float32
BF16 = jnp.bfloat16

D_MODEL = 1024
DEPTH = 4
GRID_W = 64
D_FF = 4 * D_MODEL
EPS = 1e-6
ROPE_THETA = 10000.0
N_MIXERS = 3
HEAD_DIM_AB = 64
HEADS_AB = D_MODEL // HEAD_DIM_AB
A_WIN_ROWS = 8
A_WIN_COLS = 16
B_PAIRS = ((128, 1), (512, 4), (2048, 16))
B_SIDE = 64
C_HEAD_DIM = 128
C_Q_HEADS = 8
C_KV_HEADS = 2
C_GROUP = C_Q_HEADS // C_KV_HEADS
C_QKV_WIDTH = (C_Q_HEADS + 2 * C_KV_HEADS) * C_HEAD_DIM
ROPE_HALF = 32
LANES = 128
BF16_SUBLANES = 16
MOD_ROWS = 8
MASK_NEG = -1e30
LOG2E = float(np.log2(np.e))
VMEM_LIMIT = 56 * 1024 * 1024

TN_MOD = 1536
TM_QKV = 1024
TN_QKV = 256
TN_QKV_C = 512
QKV_HALVES = 2
TM_POST = 1024
TM_POST_MERGE = 512
POST_HALVES = 2
TF_POST = 1024
POST_LEAD = 0
POST_LEAD_MERGE = 2
A_TQ = 8 * GRID_W
A_AHEAD = 3
B_AHEAD = 1
B_TQ = 256
B_HALF = 128
C_TQ = 512
C_TK = 1024
C_VT_ROWS = C_HEAD_DIM + BF16_SUBLANES


def _cparams(sem):
    return pltpu.CompilerParams(dimension_semantics=sem, vmem_limit_bytes=VMEM_LIMIT)


def _mod_kernel(c_ref, w_ref, b_ref, o_ref):
    c = c_ref[...]
    act = (c / (1.0 + jnp.exp(-c))).astype(BF16)
    o_ref[0] = jnp.dot(act, w_ref[0].astype(BF16), preferred_element_type=F32) + b_ref[0]


def _modulation(c_all, w_mod, b_mod):
    tn = TN_MOD
    n = 6 * D_MODEL
    return pl.pallas_call(
        _mod_kernel,
        grid=(DEPTH, n // tn),
        in_specs=[
            pl.BlockSpec((MOD_ROWS, D_MODEL), lambda l, j: (0, 0)),
            pl.BlockSpec((1, D_MODEL, tn), lambda l, j: (l, 0, j)),
            pl.BlockSpec((1, 1, tn), lambda l, j: (l, 0, j)),
        ],
        out_specs=pl.BlockSpec((1, MOD_ROWS, tn), lambda l, j: (l, 0, j)),
        out_shape=jax.ShapeDtypeStruct((DEPTH, MOD_ROWS, n), F32),
        compiler_params=_cparams(("parallel", "parallel")),
        name="adaln_modulation",
    )(c_all, w_mod, b_mod.reshape(DEPTH, 1, n))


def _norm_mod(x, gain, scale, shift):
    ms = jnp.mean(x * x, axis=-1, keepdims=True)
    return (x * lax.rsqrt(ms + EPS) * gain) * (1.0 + scale) + shift


def _rope128(x, cos, sin_signed, low_half):
    up = pltpu.roll(x, LANES - ROPE_HALF, 1)
    down = pltpu.roll(x, ROPE_HALF, 1)
    return x * cos + jnp.where(low_half, up, down) * sin_signed


def _low_half_mask():
    lane = lax.broadcasted_iota(jnp.int32, (1, LANES), 1)
    return (lane % (2 * ROPE_HALF)) < ROPE_HALF


def _rope_tables(seq):
    inv = ROPE_THETA ** (-jnp.arange(ROPE_HALF, dtype=F32) / ROPE_HALF)
    rows = seq // GRID_W

    def trig(pos):
        ang = pos.astype(F32)[:, None] * inv[None, :]
        return jnp.cos(ang), jnp.sin(ang)

    per_row = lambda x: jnp.repeat(x, GRID_W, axis=0)
    per_col = lambda x: jnp.tile(x, (rows, 1))
    c_row, s_row = (per_row(x) for x in trig(jnp.arange(rows)))
    c_col, s_col = (per_col(x) for x in trig(jnp.arange(GRID_W)))
    c_blk, s_blk = (per_row(x) for x in trig(jnp.arange(rows) * GRID_W))
    c_tok = c_blk * c_col - s_blk * s_col
    s_tok = s_blk * c_col + c_blk * s_col

    def table(ca, sa, cb, sb):
        return jnp.concatenate([ca, ca, cb, cb], axis=1), jnp.concatenate([-sa, sa, -sb, sb], axis=1)

    return table(c_tok, s_tok, c_tok, s_tok), table(c_row, s_row, c_col, s_col)


def _qkv_kernel(*refs, mode, dil, halves, tn):
    if mode == "a":
        x_ref, mod_ref, g_ref, w_ref, o_ref = refs
    elif mode == "b":
        x_ref, mod_ref, g_ref, w_ref, cos_ref, sin_ref, o_ref = refs
    else:
        x_ref, mod_ref, g_ref, w_ref, cos_ref, sin_ref, qg_ref, kg_ref, o_ref, vt_ref = refs
    mod = mod_ref[0]
    hm = x_ref.shape[0] // halves
    n = w_ref.shape[1]
    low = _low_half_mask()

    def normed(hf):
        rows = slice(hf * hm, (hf + 1) * hm)
        return _norm_mod(x_ref[rows, :], g_ref[0], mod[:, D_MODEL:2 * D_MODEL], mod[:, 0:D_MODEL]).astype(BF16)

    def emit(hf, j, acc):
        rows = slice(hf * hm, (hf + 1) * hm)
        if mode == "a":
            o_ref[rows, j * tn:(j + 1) * tn] = acc.astype(BF16)
            return
        cos, sin = cos_ref[rows, :], sin_ref[rows, :]
        for c in range(tn // LANES):
            col = j * tn + c * LANES
            val = acc[:, c * LANES:(c + 1) * LANES]
            if mode == "c":
                head = col // C_HEAD_DIM
                if head < C_Q_HEADS + C_KV_HEADS:
                    gain = qg_ref[...] if head < C_Q_HEADS else kg_ref[...]
                    ms = jnp.mean(val * val, axis=-1, keepdims=True)
                    val = _rope128(val * lax.rsqrt(ms + EPS) * gain, cos, sin, low)
                    o_ref[rows, col:col + LANES] = val.astype(BF16)
                else:
                    kv = head - C_Q_HEADS - C_KV_HEADS
                    vt_ref[kv * C_VT_ROWS:kv * C_VT_ROWS + C_HEAD_DIM, rows] = val.T.astype(BF16)
                    vt_ref[kv * C_VT_ROWS + C_HEAD_DIM:(kv + 1) * C_VT_ROWS, rows] = jnp.ones(
                        (C_VT_ROWS - C_HEAD_DIM, hm), BF16)
                continue
            if col < 2 * D_MODEL:
                val = _rope128(val, cos, sin, low)
            sub = hm // dil
            dst = slice(hf * sub, (hf + 1) * sub)
            if dil == 1:
                o_ref[0, 0, dst, col:col + LANES] = val.astype(BF16)
                continue
            by_residue = jnp.swapaxes(val.reshape(sub, dil, LANES), 0, 1)
            o_ref[0, :, dst, col:col + LANES] = by_residue.astype(BF16)

    hs = [normed(hf) for hf in range(halves)]
    for hf in range(halves):
        for j in range(n // tn):
            emit(hf, j, jnp.dot(hs[hf], w_ref[:, j * tn:(j + 1) * tn], preferred_element_type=F32))


def _qkv_proj(x2d, mods3, norm_g3, w, *, layer, mod_row0, seq, mode, cos=None, sin=None, qg=None, kg=None,
              group=0, dil=1):
    t = x2d.shape[0]
    tm = TM_QKV
    if mode == "b":
        n, col0 = 3 * D_MODEL, group
    else:
        n, col0 = w.shape[1], 0
    tn = TN_QKV_C if mode == "c" else TN_QKV
    hm = tm // QKV_HALVES
    per_seq = seq // tm
    in_specs = [
        pl.BlockSpec((tm, D_MODEL), lambda i: (i, 0)),
        pl.BlockSpec((1, 1, 6 * D_MODEL), lambda i: (layer * MOD_ROWS + mod_row0 + i // per_seq, 0, 0)),
        pl.BlockSpec((1, 1, D_MODEL), lambda i: (2 * layer, 0, 0)),
        pl.BlockSpec((D_MODEL, n), lambda i: (0, col0), pipeline_mode=pl.Buffered(1)),
    ]
    args = [x2d, mods3, norm_g3, w]
    if mode in ("b", "c"):
        in_specs += [pl.BlockSpec((tm, LANES), lambda i: (i % per_seq, 0))] * 2
        args += [cos, sin]
    if mode == "c":
        in_specs += [pl.BlockSpec((1, C_HEAD_DIM), lambda i: (0, 0))] * 2
        args += [qg, kg]
    scratch = []
    if mode == "b":
        assert hm % (dil * BF16_SUBLANES) == 0
        batch = t // seq
        out_spec = pl.BlockSpec((1, dil, tm // dil, n), lambda i: (i // per_seq, 0, i % per_seq, 0))
        out_shape = jax.ShapeDtypeStruct((batch, dil, seq // dil, n), BF16)
    elif mode == "c":
        assert tm == C_TK
        n_qk = (C_Q_HEADS + C_KV_HEADS) * C_HEAD_DIM
        kv_w = C_KV_HEADS * C_VT_ROWS
        out_spec = [pl.BlockSpec((tm, n_qk), lambda i: (i, 0)),
                    pl.BlockSpec((None, None, kv_w, tm), lambda i: (i // per_seq, i % per_seq, 0, 0))]
        out_shape = [jax.ShapeDtypeStruct((t, n_qk), BF16),
                     jax.ShapeDtypeStruct((t // seq, per_seq, kv_w, tm), BF16)]
    else:
        out_spec = pl.BlockSpec((tm, n), lambda i: (i, 0))
        out_shape = jax.ShapeDtypeStruct((t, n), BF16)
    return pl.pallas_call(
        functools.partial(_qkv_kernel, mode=mode, dil=dil, halves=QKV_HALVES, tn=tn),
        grid=(t // tm,),
        in_specs=in_specs,
        out_specs=out_spec,
        out_shape=out_shape,
        scratch_shapes=scratch,
        compiler_params=_cparams(("parallel",)),
        name="norm_qkv_" + mode,
    )(*args)


def _pair_masks():
    lane = lax.broadcasted_iota(jnp.int32, (1, LANES), 1)
    first = lane < HEAD_DIM_AB
    return first, jnp.logical_not(first)


def _dot_nt(a, b):
    return lax.dot_general(a, b, (((1,), (1,)), ((), ())), preferred_element_type=F32)


def _pipelined(units, scores, finish, ahead=1):
    pending = [scores(u) for u in units[:ahead]]
    for i, unit in enumerate(units):
        if i + ahead < len(units):
            pending.append(scores(units[i + ahead]))
        finish(unit, pending.pop(0))


def _attn_a_kernel(q_ref, kbuf, vbuf, bias_ref, o_ref):
    tq = q_ref.shape[0]
    rows_per = tq // GRID_W
    half = A_WIN_ROWS // 2
    win_keys = A_WIN_ROWS * GRID_W
    first = pl.program_id(1) == 0
    last = pl.program_id(1) == pl.num_programs(1) - 1
    masks = _pair_masks()
    units = [(p, rl) for p in range(HEADS_AB // 2) for rl in range(rows_per)]

    def window_of(rl):
        at_start, at_end = max(rl - half, 0), min(rl + half, rows_per)
        start = jnp.where(first, at_start, jnp.where(last, at_end, rl))
        win = jnp.where(first, at_start - rl + A_WIN_ROWS - 1,
                        jnp.where(last, at_end - rl - 1, A_WIN_ROWS - 1 - half))
        return pl.multiple_of(start * GRID_W, GRID_W), win

    def scores(unit):
        p, rl = unit
        sl = slice(p * LANES, (p + 1) * LANES)
        start, win = window_of(rl)
        qp = q_ref[rl * GRID_W:(rl + 1) * GRID_W, sl]
        stacked = jnp.concatenate([jnp.where(mk, qp, jnp.zeros_like(qp)) for mk in masks], axis=0)
        s = _dot_nt(stacked, kbuf[pl.ds(start, win_keys), sl]) + bias_ref[p, win]
        return s, s.max(axis=-1, keepdims=True)

    def finish(unit, scored):
        s, m = scored
        p, rl = unit
        sl = slice(p * LANES, (p + 1) * LANES)
        start, _ = window_of(rl)
        pr = jnp.exp2(s - m)
        l = pr.sum(axis=-1, keepdims=True)
        o = jnp.dot(pr.astype(BF16), vbuf[pl.ds(start, win_keys), sl], preferred_element_type=F32) / l
        o_ref[rl * GRID_W:(rl + 1) * GRID_W, sl] = jnp.where(masks[0], o[:GRID_W], o[GRID_W:]).astype(BF16)

    _pipelined(units, scores, finish, ahead=A_AHEAD)


def _attn_a_bias(rpb):
    n_dc = 2 * A_WIN_COLS - 1
    c, kc = np.arange(GRID_W)[:, None], np.arange(GRID_W)[None, :]
    onehot_c = ((kc - c + A_WIN_COLS - 1)[..., None] == np.arange(n_dc)).astype(np.float32)
    toe = jnp.einsum("hde,cje->hdcj", rpb * LOG2E, onehot_c, precision=lax.Precision.HIGHEST)
    cs = np.clip(c - A_WIN_COLS // 2, 0, GRID_W - A_WIN_COLS)
    col_ok = (kc >= cs) & (kc < cs + A_WIN_COLS)
    toe = jnp.where(jnp.asarray(col_ok)[None, None], toe, MASK_NEG)
    wins = jnp.stack([jnp.concatenate([toe[:, w + m] for m in range(A_WIN_ROWS)], axis=-1)
                      for w in range(A_WIN_ROWS)], axis=1)
    wins = wins.reshape(HEADS_AB // 2, 2, A_WIN_ROWS, GRID_W, A_WIN_ROWS * GRID_W)
    return jnp.transpose(wins, (0, 2, 1, 3, 4)).reshape(HEADS_AB // 2, A_WIN_ROWS, 2 * GRID_W, A_WIN_ROWS * GRID_W)


def _attn_a(qkv, bias, *, batch, seq):
    tq = A_TQ
    nb = seq // tq
    rows_per = tq // GRID_W
    side = (A_WIN_ROWS // 2) * GRID_W
    buf = tq + 2 * side
    assert nb >= 2 and rows_per >= A_WIN_ROWS // 2 and tq % side == 0 and seq >= buf
    q_spec = pl.BlockSpec((tq, D_MODEL), lambda b, i: (b * nb + i, 0))
    def kv_spec(col):
        def index(b, i):
            row = b * seq + jnp.clip(i * tq - side, 0, seq - buf)
            return pl.multiple_of(row, side), col * D_MODEL
        return pl.BlockSpec((pl.Element(buf), pl.Element(D_MODEL)), index)
    bias_spec = pl.BlockSpec(bias.shape, lambda b, i: (0, 0, 0, 0), pipeline_mode=pl.Buffered(1))
    return pl.pallas_call(
        _attn_a_kernel,
        grid=(batch, nb),
        in_specs=[q_spec, kv_spec(1), kv_spec(2), bias_spec],
        out_specs=pl.BlockSpec((tq, D_MODEL), lambda b, i: (b * nb + i, 0)),
        out_shape=jax.ShapeDtypeStruct((batch * seq, D_MODEL), BF16),
        compiler_params=_cparams(("parallel", "arbitrary")),
        name="attn_neighbourhood",
    )(qkv, qkv, qkv, bias)


def _attn_b_kernel(q_ref, kp_ref, kc_ref, kn_ref, vp_ref, vc_ref, vn_ref, mask_ref, o_ref, lse_ref):
    tq = q_ref.shape[0]
    masks = _pair_masks()
    lane = lax.broadcasted_iota(jnp.int32, (1, LANES), 1)
    lse_tiles = [jnp.zeros((B_HALF, LANES), F32) for _ in range(tq // B_HALF)]
    units = [(p, hf) for p in range(HEADS_AB // 2) for hf in range(tq // B_HALF)]

    def window(prev_ref, cur_ref, next_ref, hf, sl):
        lo, hi = hf * B_HALF - B_SIDE, (hf + 1) * B_HALF + B_SIDE
        parts = []
        if lo < 0:
            parts.append(prev_ref[tq + lo:, sl])
        parts.append(cur_ref[max(lo, 0):min(hi, tq), sl])
        if hi > tq:
            parts.append(next_ref[:hi - tq, sl])
        return jnp.concatenate(parts, axis=0)

    def scores(unit):
        p, hf = unit
        sl = slice(p * LANES, (p + 1) * LANES)
        qp = q_ref[hf * B_HALF:(hf + 1) * B_HALF, sl]
        stacked = jnp.concatenate([jnp.where(mk, qp, jnp.zeros_like(qp)) for mk in masks], axis=0)
        s = _dot_nt(stacked, window(kp_ref, kc_ref, kn_ref, hf, sl)) + mask_ref[0, hf]
        return s, s.max(axis=-1, keepdims=True)

    def finish(unit, scored):
        s, m = scored
        p, hf = unit
        sl = slice(p * LANES, (p + 1) * LANES)
        pr = jnp.exp2(s - m)
        l = pr.sum(axis=-1, keepdims=True)
        o = jnp.dot(pr.astype(BF16), window(vp_ref, vc_ref, vn_ref, hf, sl), preferred_element_type=F32) / l
        o_ref[hf * B_HALF:(hf + 1) * B_HALF, sl] = jnp.where(masks[0], o[:B_HALF], o[B_HALF:]).astype(BF16)
        lse = m + jnp.log2(l)
        tile = jnp.where(lane == 2 * p, lse[:B_HALF], lse_tiles[hf])
        lse_tiles[hf] = jnp.where(lane == 2 * p + 1, lse[B_HALF:], tile)

    _pipelined(units, scores, finish, ahead=B_AHEAD)
    for hf, tile in enumerate(lse_tiles):
        lse_ref[hf * B_HALF:(hf + 1) * B_HALF, :] = tile


def _attn_b_masks():
    qq = np.arange(B_HALF)[:, None]
    jj = np.arange(B_HALF + 2 * B_SIDE)[None, :]
    band = (jj - qq >= 0) & (jj - qq <= 2 * B_SIDE)
    out = []
    for ty in range(4):
        per_half = []
        for hf in range(B_TQ // B_HALF):
            pos = hf * B_HALF - B_SIDE + jj
            ok = band
            if ty & 1:
                ok = ok & (pos >= 0)
            if ty & 2:
                ok = ok & (pos < B_TQ)
            half = np.where(ok, 0.0, MASK_NEG)
            per_half.append(np.concatenate([half, half], axis=0))
        out.append(np.stack(per_half))
    return np.stack(out).astype(np.float32)


def _attn_b_group(qkv, masks, *, group):
    batch, dil, sub, _ = qkv.shape
    tq = B_TQ
    nb = sub // tq
    assert sub % tq == 0
    q_spec = pl.BlockSpec((None, None, tq, D_MODEL), lambda b, r, i: (b, r, i, 0))
    def kv_spec(which, off):
        return pl.BlockSpec((None, None, tq, D_MODEL),
                            lambda b, r, i: (b, r, jnp.clip(i + off, 0, nb - 1), which))
    mask_spec = pl.BlockSpec((1,) + masks.shape[1:],
                             lambda b, r, i: ((i == 0).astype(jnp.int32) + 2 * (i == nb - 1).astype(jnp.int32),
                                              0, 0, 0))
    return pl.pallas_call(
        _attn_b_kernel,
        grid=(batch, dil, nb),
        in_specs=[q_spec, kv_spec(1, -1), kv_spec(1, 0), kv_spec(1, 1),
                  kv_spec(2, -1), kv_spec(2, 0), kv_spec(2, 1), mask_spec],
        out_specs=[pl.BlockSpec((None, None, tq, D_MODEL), lambda b, r, i: (b, r, i, 0)),
                   pl.BlockSpec((None, None, tq, LANES), lambda b, r, i: (b, r, i, 0))],
        out_shape=[jax.ShapeDtypeStruct((batch, dil, sub, D_MODEL), BF16),
                   jax.ShapeDtypeStruct((batch, dil, sub, LANES), F32)],
        compiler_params=_cparams(("parallel", "parallel", "arbitrary")),
        name="attn_dilated_g%d" % group,
    )(qkv, qkv, qkv, qkv, qkv, qkv, qkv, masks)


def _attn_c_kernel(q_ref, k_ref, vt_ref, o_ref, s_ref, smax_ref, m_ref, acc_ref, *, tk):
    nk = k_ref.shape[0] // tk
    m_ref[...] = jnp.full(m_ref.shape, MASK_NEG, F32)
    acc_ref[...] = jnp.zeros(acc_ref.shape, F32)

    def issue_scores(h, c, slot):
        start = pl.multiple_of(c * tk, tk)
        s = _dot_nt(k_ref[pl.ds(start, tk), :], q_ref[:, h * C_HEAD_DIM:(h + 1) * C_HEAD_DIM])
        s_ref[slot] = s
        smax_ref[slot] = s.max(axis=0, keepdims=True)

    issue_scores(0, 0, 0)

    def chunk(c, carry):
        vt = vt_ref[c]
        for h in range(C_GROUP):
            if h + 1 < C_GROUP:
                issue_scores(h + 1, c, (h + 1) % 2)
            else:
                issue_scores(0, jnp.minimum(c + 1, nk - 1), (h + 1) % 2)
            s = s_ref[h % 2]
            m_prev = m_ref[h]
            m_new = jnp.maximum(m_prev, smax_ref[h % 2])
            alpha = jnp.exp2(m_prev - m_new)
            pr = jnp.exp2(s - m_new)
            acc_ref[h] = alpha * acc_ref[h] + jnp.dot(vt, pr.astype(BF16), preferred_element_type=F32)
            m_ref[h] = m_new
        return carry

    lax.fori_loop(0, nk, chunk, 0)
    for h in range(C_GROUP):
        acc = acc_ref[h]
        out = acc[:C_HEAD_DIM] / acc[C_HEAD_DIM:C_HEAD_DIM + 1]
        o_ref[:, h * C_HEAD_DIM:(h + 1) * C_HEAD_DIM] = out.T.astype(BF16)


def _attn_c(qk, vt, *, batch, seq):
    tq, tk = C_TQ, C_TK
    nq = seq // tq
    assert seq % tk == 0 and C_GROUP % 2 == 0
    qw = C_GROUP * C_HEAD_DIM
    return pl.pallas_call(
        functools.partial(_attn_c_kernel, tk=tk),
        grid=(batch, C_KV_HEADS, nq),
        in_specs=[
            pl.BlockSpec((tq, qw), lambda b, g, i: (b * nq + i, g)),
            pl.BlockSpec((seq, C_HEAD_DIM), lambda b, g, i: (b, C_Q_HEADS + g)),
            pl.BlockSpec((None, seq // tk, C_VT_ROWS, tk), lambda b, g, i: (b, 0, g, 0)),
        ],
        out_specs=pl.BlockSpec((tq, qw), lambda b, g, i: (b * nq + i, g)),
        out_shape=jax.ShapeDtypeStruct((batch * seq, C_Q_HEADS * C_HEAD_DIM), BF16),
        scratch_shapes=[pltpu.VMEM((2, tk, tq), F32), pltpu.VMEM((2, 1, tq), F32),
                        pltpu.VMEM((C_GROUP, 1, tq), F32),
                        pltpu.VMEM((C_GROUP, C_VT_ROWS, tq), F32)],
        compiler_params=_cparams(("parallel", "parallel", "arbitrary")),
        name="attn_gqa_flash",
    )(qk, qk, vt)


def _post_kernel(*refs, merge, final, halves, tf, lead):
    refs = list(refs)
    x_ref = refs.pop(0)
    if merge:
        o_refs = [refs.pop(0) for _ in range(3)]
        lse_refs = [refs.pop(0) for _ in range(3)]
    else:
        o_ref_in = refs.pop(0)
    mod_ref, g_ref, wo_ref, w1_ref, w2_ref = [refs.pop(0) for _ in range(5)]
    fg_ref = refs.pop(0) if final else None
    out_ref = refs.pop(0)
    mod = mod_ref[0]
    hm = x_ref.shape[0] // halves

    def token_order(ref, hf, c):
        dil = ref.shape[0]
        rows = hm // dil
        part = ref[:, hf * rows:(hf + 1) * rows, c * LANES:(c + 1) * LANES].astype(F32)
        return part[0] if dil == 1 else jnp.swapaxes(part, 0, 1).reshape(hm, LANES)

    def merged_groups(hf):
        lse = [token_order(r, hf, 0) for r in lse_refs]
        top = jnp.maximum(jnp.maximum(lse[0], lse[1]), lse[2])
        ex = [jnp.exp2(v - top) for v in lse]
        den = ex[0] + ex[1] + ex[2]
        wgt = [e / den for e in ex]
        lane = lax.broadcasted_iota(jnp.int32, (hm, LANES), 1)
        cols = []
        for c in range(D_MODEL // LANES):
            head_of_lane = lane // HEAD_DIM_AB + c * (LANES // HEAD_DIM_AB)
            mixed = None
            for g in range(3):
                term = jnp.take_along_axis(wgt[g], head_of_lane, axis=1) * token_order(o_refs[g], hf, c)
                mixed = term if mixed is None else mixed + term
            cols.append(mixed.astype(BF16))
        return jnp.concatenate(cols, axis=1)

    def pre(hf):
        rows = slice(hf * hm, (hf + 1) * hm)
        o = merged_groups(hf) if merge else o_ref_in[rows, :]
        mix = jnp.dot(o, wo_ref[...], preferred_element_type=F32)
        x1 = x_ref[rows, :] + mod[:, 2 * D_MODEL:3 * D_MODEL] * mix
        h2 = _norm_mod(x1, g_ref[0], mod[:, 4 * D_MODEL:5 * D_MODEL], mod[:, 3 * D_MODEL:4 * D_MODEL])
        return x1, h2.astype(BF16)

    def mlp(h2, acc, chunks):
        for c in chunks:
            a = jnp.dot(h2, w1_ref[:, c * tf:(c + 1) * tf], preferred_element_type=F32)
            a = jnp.square(jnp.maximum(a, 0.0)).astype(BF16)
            d = jnp.dot(a, w2_ref[c * tf:(c + 1) * tf, :], preferred_element_type=F32)
            acc = d if acc is None else acc + d
        return acc

    def finish(hf, x1, acc):
        x2 = x1 + mod[:, 5 * D_MODEL:6 * D_MODEL] * acc
        if final:
            ms = jnp.mean(x2 * x2, axis=-1, keepdims=True)
            x2 = x2 * lax.rsqrt(ms + EPS) * fg_ref[...]
        out_ref[hf * hm:(hf + 1) * hm, :] = x2

    n_chunks = D_FF // tf
    cur = pre(0)
    for hf in range(halves):
        x1, h2 = cur
        acc = mlp(h2, None, range(0, lead))
        if hf + 1 < halves:
            cur = pre(hf + 1)
        finish(hf, x1, mlp(h2, acc, range(lead, n_chunks)))


def _post(x2d, attn, mods3, norm_g3, wo, w1, w2, final_g, *, layer, mod_row0, seq, merge, final):
    t = x2d.shape[0]
    tm = TM_POST_MERGE if merge else TM_POST
    hm = tm // POST_HALVES
    per_seq = seq // tm
    row = lambda i: (i, 0)
    const = lambda i: (0, 0)
    resident = pl.Buffered(1)
    in_specs = [pl.BlockSpec((tm, D_MODEL), row)]
    args = [x2d]
    scratch = []
    if merge:
        outs, lses = attn
        def res_spec(arr):
            dil, width = arr.shape[1], arr.shape[3]
            assert hm % (dil * BF16_SUBLANES) == 0
            return pl.BlockSpec((None, dil, tm // dil, width), lambda i: (i // per_seq, 0, i % per_seq, 0))
        in_specs += [res_spec(a) for a in outs] + [res_spec(a) for a in lses]
        args += list(outs) + list(lses)
    else:
        in_specs += [pl.BlockSpec((tm, D_MODEL), row)]
        args += [attn]
    in_specs += [
        pl.BlockSpec((1, 1, 6 * D_MODEL), lambda i: (layer * MOD_ROWS + mod_row0 + i // per_seq, 0, 0)),
        pl.BlockSpec((1, 1, D_MODEL), lambda i: (2 * layer + 1, 0, 0)),
        pl.BlockSpec((D_MODEL, D_MODEL), const, pipeline_mode=resident),
        pl.BlockSpec((D_MODEL, D_FF), const, pipeline_mode=resident),
        pl.BlockSpec((D_FF, D_MODEL), const, pipeline_mode=resident),
    ]
    args += [mods3, norm_g3, wo, w1, w2]
    if final:
        in_specs += [pl.BlockSpec((1, D_MODEL), const)]
        args += [final_g.reshape(1, D_MODEL)]
    return pl.pallas_call(
        functools.partial(_post_kernel, merge=merge, final=final, halves=POST_HALVES, tf=TF_POST,
                          lead=POST_LEAD_MERGE if merge else POST_LEAD),
        grid=(t // tm,),
        in_specs=in_specs,
        out_specs=pl.BlockSpec((tm, D_MODEL), row),
        out_shape=jax.ShapeDtypeStruct((t, D_MODEL), F32),
        scratch_shapes=scratch,
        compiler_params=_cparams(("parallel",)),
        name="wo_mlp_merge" if merge else "wo_mlp",
    )(*args)


def _trunk(x, mods3, mod_row0, norm_g3, final_g, wts, a_bias, b_masks, tabs_b, tabs_c):
    batch, seq, _ = x.shape
    assert seq <= tabs_b[0].shape[0]
    x2d = x.reshape(batch * seq, D_MODEL)
    for layer in range(DEPTH):
        kind, j = layer % N_MIXERS, layer // N_MIXERS
        common = dict(layer=layer, mod_row0=mod_row0, seq=seq)
        if kind == 0:
            qkv = _qkv_proj(x2d, mods3, norm_g3, wts["a_qkv"][j], mode="a", **common)
            attn = _attn_a(qkv, a_bias[j], batch=batch, seq=seq)
            wo = wts["a_o"][j]
        elif kind == 1:
            groups = []
            for g, (win, dil) in enumerate(B_PAIRS):
                assert win == 2 * B_SIDE * dil
                qkv = _qkv_proj(x2d, mods3, norm_g3, wts["b_qkv"][j], mode="b", cos=tabs_b[0], sin=tabs_b[1],
                                group=g, dil=dil, **common)
                groups.append(_attn_b_group(qkv, b_masks, group=g))
            attn = ([o for o, _ in groups], [l for _, l in groups])
            wo = wts["b_o"][j]
        else:
            qk, vt = _qkv_proj(x2d, mods3, norm_g3, wts["c_qkv"][j], mode="c", cos=tabs_c[0], sin=tabs_c[1],
                               qg=wts["c_qg"][j], kg=wts["c_kg"][j], **common)
            attn = _attn_c(qk, vt, batch=batch, seq=seq)
            wo = wts["c_o"][j]
        x2d = _post(x2d, attn, mods3, norm_g3, wo, wts["w1"][layer], wts["w2"][layer], final_g,
                    merge=(kind == 1), final=(layer == DEPTH - 1), **common)
    return x2d.reshape(batch, seq, D_MODEL)


def kernel(x_prompt, x_sample, c_prompt, c_sample, w_mod, b_mod, norm_g, final_g, a_w_qkv, a_rpb, a_w_o,
           b_w_qkv, b_w_o, c_w_qkv, c_q_g, c_k_g, c_w_o, mlp_w1, mlp_w2):
    nb_p, nb_s = c_prompt.shape[0], c_sample.shape[0]
    assert nb_p + nb_s <= MOD_ROWS
    c_all = jnp.concatenate([c_prompt, c_sample, jnp.zeros((MOD_ROWS - nb_p - nb_s, D_MODEL), F32)], axis=0)
    mods3 = _modulation(c_all, w_mod, b_mod).reshape(DEPTH * MOD_ROWS, 1, 6 * D_MODEL)
    norm_g3 = norm_g.reshape(DEPTH * 2, 1, D_MODEL)

    scale_ab = HEAD_DIM_AB ** -0.5 * LOG2E
    col_scale = np.ones((3 * D_MODEL,), np.float32)
    col_scale[:D_MODEL] = scale_ab
    a_qkv = (a_w_qkv * col_scale).astype(BF16)
    b_qkv = (b_w_qkv * np.tile(col_scale, len(B_PAIRS))).astype(BF16)
    wts = {
        "a_qkv": a_qkv, "a_o": a_w_o.astype(BF16),
        "b_qkv": b_qkv, "b_o": b_w_o.astype(BF16),
        "c_qkv": c_w_qkv.astype(BF16), "c_o": c_w_o.astype(BF16),
        "c_qg": (c_q_g * (C_HEAD_DIM ** -0.5 * LOG2E)).reshape(-1, 1, C_HEAD_DIM),
        "c_kg": c_k_g.reshape(-1, 1, C_HEAD_DIM),
        "w1": mlp_w1.astype(BF16), "w2": mlp_w2.astype(BF16),
    }
    a_bias = [_attn_a_bias(a_rpb[j]) for j in range(a_rpb.shape[0])]
    b_masks = jnp.asarray(_attn_b_masks())

    tabs_b, tabs_c = _rope_tables(max(x_prompt.shape[1], x_sample.shape[1]))
    shared = (norm_g3, final_g, wts, a_bias, b_masks, tabs_b, tabs_c)
    y_prompt = _trunk(x_prompt, mods3, 0, *shared)
    y_sample = _trunk(x_sample, mods3, nb_p, *shared)
    return (y_prompt, y_sample)
```

```python
import functools

import numpy as np
import jax
import jax.numpy as jnp
from jax import lax
from jax.experimental import pallas as pl
from jax.experimental.pallas import tpu as pltpu

F32 = jnp.float32
BF16 = jnp.bfloat16

D_MODEL = 1024
DEPTH = 4
GRID_W = 64
D_FF = 4 * D_MODEL
EPS = 1e-6
ROPE_THETA = 10000.0
N_MIXERS = 3
HEAD_DIM_AB = 64
HEADS_AB = D_MODEL // HEAD_DIM_AB
A_WIN_ROWS = 8
A_WIN_COLS = 16
B_PAIRS = ((128, 1), (512, 4), (2048, 16))
B_SIDE = 64
C_HEAD_DIM = 128
C_Q_HEADS = 8
C_KV_HEADS = 2
C_GROUP = C_Q_HEADS // C_KV_HEADS
C_QKV_WIDTH = (C_Q_HEADS + 2 * C_KV_HEADS) * C_HEAD_DIM
ROPE_HALF = 32
LANES = 128
BF16_SUBLANES = 16
MOD_ROWS = 8
MASK_NEG = -1e30
LOG2E = float(np.log2(np.e))
VMEM_LIMIT = 60 * 1024 * 1024

TN_MOD = 1536
TM_QKV = 1024
TN_QKV = 256
TN_QKV_C = 512
QKV_HALVES = 2
TM_POST = 1024
TM_POST_MERGE = 1024
POST_HALVES_MERGE = 4
POST_HALVES = 2
TF_POST = 1024
POST_LEAD = 0
POST_LEAD_MERGE = 2
A_TQ = 8 * GRID_W
A_AHEAD = 3
B_AHEAD = 1
B_TQ = 256
B_HALF = 128
C_TQ = 512
C_TK = 1024
C_VT_ROWS = C_HEAD_DIM + BF16_SUBLANES


def _cparams(sem):
    return pltpu.CompilerParams(dimension_semantics=sem, vmem_limit_bytes=VMEM_LIMIT)


def _mod_kernel(c_ref, w_ref, b_ref, o_ref):
    c = c_ref[...]
    act = (c / (1.0 + jnp.exp(-c))).astype(BF16)
    o_ref[0] = jnp.dot(act, w_ref[0].astype(BF16), preferred_element_type=F32) + b_ref[0]


def _modulation(c_all, w_mod, b_mod):
    tn = TN_MOD
    n = 6 * D_MODEL
    return pl.pallas_call(
        _mod_kernel,
        grid=(DEPTH, n // tn),
        in_specs=[
            pl.BlockSpec((MOD_ROWS, D_MODEL), lambda l, j: (0, 0)),
            pl.BlockSpec((1, D_MODEL, tn), lambda l, j: (l, 0, j)),
            pl.BlockSpec((1, 1, tn), lambda l, j: (l, 0, j)),
        ],
        out_specs=pl.BlockSpec((1, MOD_ROWS, tn), lambda l, j: (l, 0, j)),
        out_shape=jax.ShapeDtypeStruct((DEPTH, MOD_ROWS, n), F32),
        compiler_params=_cparams(("parallel", "parallel")),
        name="adaln_modulation",
    )(c_all, w_mod, b_mod.reshape(DEPTH, 1, n))


def _norm_mod(x, gain, scale, shift):
    ms = jnp.mean(x * x, axis=-1, keepdims=True)
    return (x * lax.rsqrt(ms + EPS) * gain) * (1.0 + scale) + shift


def _rope128(x, cos, sin_signed, low_half):
    up = pltpu.roll(x, LANES - ROPE_HALF, 1)
    down = pltpu.roll(x, ROPE_HALF, 1)
    return x * cos + jnp.where(low_half, up, down) * sin_signed


def _low_half_mask():
    lane = lax.broadcasted_iota(jnp.int32, (1, LANES), 1)
    return (lane % (2 * ROPE_HALF)) < ROPE_HALF


def _rope_tables(seq):
    inv = ROPE_THETA ** (-jnp.arange(ROPE_HALF, dtype=F32) / ROPE_HALF)
    rows = seq // GRID_W

    def trig(pos):
        ang = pos.astype(F32)[:, None] * inv[None, :]
        return jnp.cos(ang), jnp.sin(ang)

    per_row = lambda x: jnp.repeat(x, GRID_W, axis=0)
    per_col = lambda x: jnp.tile(x, (rows, 1))
    c_row, s_row = (per_row(x) for x in trig(jnp.arange(rows)))
    c_col, s_col = (per_col(x) for x in trig(jnp.arange(GRID_W)))
    c_blk, s_blk = (per_row(x) for x in trig(jnp.arange(rows) * GRID_W))
    c_tok = c_blk * c_col - s_blk * s_col
    s_tok = s_blk * c_col + c_blk * s_col

    def table(ca, sa, cb, sb):
        return jnp.concatenate([ca, ca, cb, cb], axis=1), jnp.concatenate([-sa, sa, -sb, sb], axis=1)

    return table(c_tok, s_tok, c_tok, s_tok), table(c_row, s_row, c_col, s_col)


def _qkv_kernel(*refs, mode, dil, halves, tn):
    if mode == "a":
        x_ref, mod_ref, g_ref, w_ref, o_ref = refs
    elif mode == "b":
        x_ref, mod_ref, g_ref, w_ref, cos_ref, sin_ref, o_ref = refs
    else:
        x_ref, mod_ref, g_ref, w_ref, cos_ref, sin_ref, qg_ref, kg_ref, o_ref, vt_ref = refs
    mod = mod_ref[0]
    hm = x_ref.shape[0] // halves
    n = w_ref.shape[1]
    low = _low_half_mask()

    def normed(hf):
        rows = slice(hf * hm, (hf + 1) * hm)
        return _norm_mod(x_ref[rows, :], g_ref[0], mod[:, D_MODEL:2 * D_MODEL], mod[:, 0:D_MODEL]).astype(BF16)

    def emit(hf, j, acc):
        rows = slice(hf * hm, (hf + 1) * hm)
        if mode == "a":
            o_ref[rows, j * tn:(j + 1) * tn] = acc.astype(BF16)
            return
        cos, sin = cos_ref[rows, :], sin_ref[rows, :]
        for c in range(tn // LANES):
            col = j * tn + c * LANES
            val = acc[:, c * LANES:(c + 1) * LANES]
            if mode == "c":
                head = col // C_HEAD_DIM
                if head < C_Q_HEADS + C_KV_HEADS:
                    gain = qg_ref[...] if head < C_Q_HEADS else kg_ref[...]
                    ms = jnp.mean(val * val, axis=-1, keepdims=True)
                    val = _rope128(val * lax.rsqrt(ms + EPS) * gain, cos, sin, low)
                    o_ref[rows, col:col + LANES] = val.astype(BF16)
                else:
                    kv = head - C_Q_HEADS - C_KV_HEADS
                    vt_ref[kv * C_VT_ROWS:kv * C_VT_ROWS + C_HEAD_DIM, rows] = val.T.astype(BF16)
                    vt_ref[kv * C_VT_ROWS + C_HEAD_DIM:(kv + 1) * C_VT_ROWS, rows] = jnp.ones(
                        (C_VT_ROWS - C_HEAD_DIM, hm), BF16)
                continue
            if col < 2 * D_MODEL:
                val = _rope128(val, cos, sin, low)
            sub = hm // dil
            dst = slice(hf * sub, (hf + 1) * sub)
            if dil == 1:
                o_ref[0, 0, dst, col:col + LANES] = val.astype(BF16)
                continue
            by_residue = jnp.swapaxes(val.reshape(sub, dil, LANES), 0, 1)
            o_ref[0, :, dst, col:col + LANES] = by_residue.astype(BF16)

    hs = [normed(hf) for hf in range(halves)]
    for hf in range(halves):
        for j in range(n // tn):
            emit(hf, j, jnp.dot(hs[hf], w_ref[:, j * tn:(j + 1) * tn], preferred_element_type=F32))


def _qkv_proj(x2d, mods3, norm_g3, w, *, layer, mod_row0, seq, mode, cos=None, sin=None, qg=None, kg=None,
              group=0, dil=1):
    t = x2d.shape[0]
    tm = TM_QKV
    if mode == "b":
        n, col0 = 3 * D_MODEL, group
    else:
        n, col0 = w.shape[1], 0
    tn = TN_QKV_C if mode == "c" else TN_QKV
    hm = tm // QKV_HALVES
    per_seq = seq // tm
    in_specs = [
        pl.BlockSpec((tm, D_MODEL), lambda i: (i, 0)),
        pl.BlockSpec((1, 1, 6 * D_MODEL), lambda i: (layer * MOD_ROWS + mod_row0 + i // per_seq, 0, 0)),
        pl.BlockSpec((1, 1, D_MODEL), lambda i: (2 * layer, 0, 0)),
        pl.BlockSpec((D_MODEL, n), lambda i: (0, col0), pipeline_mode=pl.Buffered(1)),
    ]
    args = [x2d, mods3, norm_g3, w]
    if mode in ("b", "c"):
        in_specs += [pl.BlockSpec((tm, LANES), lambda i: (i % per_seq, 0))] * 2
        args += [cos, sin]
    if mode == "c":
        in_specs += [pl.BlockSpec((1, C_HEAD_DIM), lambda i: (0, 0))] * 2
        args += [qg, kg]
    scratch = []
    if mode == "b":
        assert hm % (dil * BF16_SUBLANES) == 0
        batch = t // seq
        out_spec = pl.BlockSpec((1, dil, tm // dil, n), lambda i: (i // per_seq, 0, i % per_seq, 0))
        out_shape = jax.ShapeDtypeStruct((batch, dil, seq // dil, n), BF16)
    elif mode == "c":
        assert tm == C_TK
        n_qk = (C_Q_HEADS + C_KV_HEADS) * C_HEAD_DIM
        kv_w = C_KV_HEADS * C_VT_ROWS
        out_spec = [pl.BlockSpec((tm, n_qk), lambda i: (i, 0)),
                    pl.BlockSpec((None, None, kv_w, tm), lambda i: (i // per_seq, i % per_seq, 0, 0))]
        out_shape = [jax.ShapeDtypeStruct((t, n_qk), BF16),
                     jax.ShapeDtypeStruct((t // seq, per_seq, kv_w, tm), BF16)]
    else:
        out_spec = pl.BlockSpec((tm, n), lambda i: (i, 0))
        out_shape = jax.ShapeDtypeStruct((t, n), BF16)
    return pl.pallas_call(
        functools.partial(_qkv_kernel, mode=mode, dil=dil, halves=QKV_HALVES, tn=tn),
        grid=(t // tm,),
        in_specs=in_specs,
        out_specs=out_spec,
        out_shape=out_shape,
        scratch_shapes=scratch,
        compiler_params=_cparams(("parallel",)),
        name="norm_qkv_" + mode,
    )(*args)


def _pair_masks():
    lane = lax.broadcasted_iota(jnp.int32, (1, LANES), 1)
    first = lane < HEAD_DIM_AB
    return first, jnp.logical_not(first)


def _dot_nt(a, b):
    return lax.dot_general(a, b, (((1,), (1,)), ((), ())), preferred_element_type=F32)


def _pipelined(units, scores, finish, ahead=1):
    pending = [scores(u) for u in units[:ahead]]
    for i, unit in enumerate(units):
        if i + ahead < len(units):
            pending.append(scores(units[i + ahead]))
        finish(unit, pending.pop(0))


def _attn_a_kernel(q_ref, kbuf, vbuf, bias_ref, o_ref):
    tq = q_ref.shape[0]
    rows_per = tq // GRID_W
    half = A_WIN_ROWS // 2
    win_keys = A_WIN_ROWS * GRID_W
    first = pl.program_id(1) == 0
    last = pl.program_id(1) == pl.num_programs(1) - 1
    masks = _pair_masks()
    units = [(p, rl) for p in range(HEADS_AB // 2) for rl in range(rows_per)]

    def window_of(rl):
        at_start, at_end = max(rl - half, 0), min(rl + half, rows_per)
        start = jnp.where(first, at_start, jnp.where(last, at_end, rl))
        win = jnp.where(first, at_start - rl + A_WIN_ROWS - 1,
                        jnp.where(last, at_end - rl - 1, A_WIN_ROWS - 1 - half))
        return pl.multiple_of(start * GRID_W, GRID_W), win

    def scores(unit):
        p, rl = unit
        sl = slice(p * LANES, (p + 1) * LANES)
        start, win = window_of(rl)
        qp = q_ref[rl * GRID_W:(rl + 1) * GRID_W, sl]
        stacked = jnp.concatenate([jnp.where(mk, qp, jnp.zeros_like(qp)) for mk in masks], axis=0)
        s = _dot_nt(stacked, kbuf[pl.ds(start, win_keys), sl]) + bias_ref[p, win]
        return s, s.max(axis=-1, keepdims=True)

    def finish(unit, scored):
        s, m = scored
        p, rl = unit
        sl = slice(p * LANES, (p + 1) * LANES)
        start, _ = window_of(rl)
        pr = jnp.exp2(s - m)
        l = pr.sum(axis=-1, keepdims=True)
        o = jnp.dot(pr.astype(BF16), vbuf[pl.ds(start, win_keys), sl], preferred_element_type=F32) / l
        o_ref[rl * GRID_W:(rl + 1) * GRID_W, sl] = jnp.where(masks[0], o[:GRID_W], o[GRID_W:]).astype(BF16)

    _pipelined(units, scores, finish, ahead=A_AHEAD)


def _attn_a_bias(rpb):
    n_dc = 2 * A_WIN_COLS - 1
    c, kc = np.arange(GRID_W)[:, None], np.arange(GRID_W)[None, :]
    onehot_c = ((kc - c + A_WIN_COLS - 1)[..., None] == np.arange(n_dc)).astype(np.float32)
    toe = jnp.einsum("hde,cje->hdcj", rpb * LOG2E, onehot_c, precision=lax.Precision.HIGHEST)
    cs = np.clip(c - A_WIN_COLS // 2, 0, GRID_W - A_WIN_COLS)
    col_ok = (kc >= cs) & (kc < cs + A_WIN_COLS)
    toe = jnp.where(jnp.asarray(col_ok)[None, None], toe, MASK_NEG)
    wins = jnp.stack([jnp.concatenate([toe[:, w + m] for m in range(A_WIN_ROWS)], axis=-1)
                      for w in range(A_WIN_ROWS)], axis=1)
    wins = wins.reshape(HEADS_AB // 2, 2, A_WIN_ROWS, GRID_W, A_WIN_ROWS * GRID_W)
    return jnp.transpose(wins, (0, 2, 1, 3, 4)).reshape(HEADS_AB // 2, A_WIN_ROWS, 2 * GRID_W, A_WIN_ROWS * GRID_W)


def _attn_a(qkv, bias, *, batch, seq):
    tq = A_TQ
    nb = seq // tq
    rows_per = tq // GRID_W
    side = (A_WIN_ROWS // 2) * GRID_W
    buf = tq + 2 * side
    assert nb >= 2 and rows_per >= A_WIN_ROWS // 2 and tq % side == 0 and seq >= buf
    q_spec = pl.BlockSpec((tq, D_MODEL), lambda b, i: (b * nb + i, 0))
    def kv_spec(col):
        def index(b, i):
            row = b * seq + jnp.clip(i * tq - side, 0, seq - buf)
            return pl.multiple_of(row, side), col * D_MODEL
        return pl.BlockSpec((pl.Element(buf), pl.Element(D_MODEL)), index)
    bias_spec = pl.BlockSpec(bias.shape, lambda b, i: (0, 0, 0, 0), pipeline_mode=pl.Buffered(1))
    return pl.pallas_call(
        _attn_a_kernel,
        grid=(batch, nb),
        in_specs=[q_spec, kv_spec(1), kv_spec(2), bias_spec],
        out_specs=pl.BlockSpec((tq, D_MODEL), lambda b, i: (b * nb + i, 0)),
        out_shape=jax.ShapeDtypeStruct((batch * seq, D_MODEL), BF16),
        compiler_params=_cparams(("parallel", "arbitrary")),
        name="attn_neighbourhood",
    )(qkv, qkv, qkv, bias)


def _attn_b_kernel(q_ref, kp_ref, kc_ref, kn_ref, vp_ref, vc_ref, vn_ref, mask_ref, o_ref, lse_ref):
    tq = q_ref.shape[0]
    masks = _pair_masks()
    lane = lax.broadcasted_iota(jnp.int32, (1, LANES), 1)
    lse_tiles = [jnp.zeros((B_HALF, LANES), F32) for _ in range(tq // B_HALF)]
    units = [(p, hf) for p in range(HEADS_AB // 2) for hf in range(tq // B_HALF)]

    def window(prev_ref, cur_ref, next_ref, hf, sl):
        lo, hi = hf * B_HALF - B_SIDE, (hf + 1) * B_HALF + B_SIDE
        parts = []
        if lo < 0:
            parts.append(prev_ref[tq + lo:, sl])
        parts.append(cur_ref[max(lo, 0):min(hi, tq), sl])
        if hi > tq:
            parts.append(next_ref[:hi - tq, sl])
        return jnp.concatenate(parts, axis=0)

    def scores(unit):
        p, hf = unit
        sl = slice(p * LANES, (p + 1) * LANES)
        qp = q_ref[hf * B_HALF:(hf + 1) * B_HALF, sl]
        stacked = jnp.concatenate([jnp.where(mk, qp, jnp.zeros_like(qp)) for mk in masks], axis=0)
        s = _dot_nt(stacked, window(kp_ref, kc_ref, kn_ref, hf, sl)) + mask_ref[0, hf]
        return s, s.max(axis=-1, keepdims=True)

    def finish(unit, scored):
        s, m = scored
        p, hf = unit
        sl = slice(p * LANES, (p + 1) * LANES)
        pr = jnp.exp2(s - m)
        l = pr.sum(axis=-1, keepdims=True)
        o = jnp.dot(pr.astype(BF16), window(vp_ref, vc_ref, vn_ref, hf, sl), preferred_element_type=F32) / l
        o_ref[hf * B_HALF:(hf + 1) * B_HALF, sl] = jnp.where(masks[0], o[:B_HALF], o[B_HALF:]).astype(BF16)
        lse = m + jnp.log2(l)
        tile = jnp.where(lane == 2 * p, lse[:B_HALF], lse_tiles[hf])
        lse_tiles[hf] = jnp.where(lane == 2 * p + 1, lse[B_HALF:], tile)

    _pipelined(units, scores, finish, ahead=B_AHEAD)
    for hf, tile in enumerate(lse_tiles):
        lse_ref[hf * B_HALF:(hf + 1) * B_HALF, :] = tile


def _attn_b_masks():
    qq = np.arange(B_HALF)[:, None]
    jj = np.arange(B_HALF + 2 * B_SIDE)[None, :]
    band = (jj - qq >= 0) & (jj - qq <= 2 * B_SIDE)
    out = []
    for ty in range(4):
        per_half = []
        for hf in range(B_TQ // B_HALF):
            pos = hf * B_HALF - B_SIDE + jj
            ok = band
            if ty & 1:
                ok = ok & (pos >= 0)
            if ty & 2:
                ok = ok & (pos < B_TQ)
            half = np.where(ok, 0.0, MASK_NEG)
            per_half.append(np.concatenate([half, half], axis=0))
        out.append(np.stack(per_half))
    return np.stack(out).astype(np.float32)


def _attn_b_group(qkv, masks, *, group):
    batch, dil, sub, _ = qkv.shape
    tq = B_TQ
    nb = sub // tq
    assert sub % tq == 0
    q_spec = pl.BlockSpec((None, None, tq, D_MODEL), lambda b, r, i: (b, r, i, 0))
    def kv_spec(which, off):
        return pl.BlockSpec((None, None, tq, D_MODEL),
                            lambda b, r, i: (b, r, jnp.clip(i + off, 0, nb - 1), which))
    mask_spec = pl.BlockSpec((1,) + masks.shape[1:],
                             lambda b, r, i: ((i == 0).astype(jnp.int32) + 2 * (i == nb - 1).astype(jnp.int32),
                                              0, 0, 0))
    return pl.pallas_call(
        _attn_b_kernel,
        grid=(batch, dil, nb),
        in_specs=[q_spec, kv_spec(1, -1), kv_spec(1, 0), kv_spec(1, 1),
                  kv_spec(2, -1), kv_spec(2, 0), kv_spec(2, 1), mask_spec],
        out_specs=[pl.BlockSpec((None, None, tq, D_MODEL), lambda b, r, i: (b, r, i, 0)),
                   pl.BlockSpec((None, None, tq, LANES), lambda b, r, i: (b, r, i, 0))],
        out_shape=[jax.ShapeDtypeStruct((batch, dil, sub, D_MODEL), BF16),
                   jax.ShapeDtypeStruct((batch, dil, sub, LANES), F32)],
        compiler_params=_cparams(("parallel", "parallel", "arbitrary")),
        name="attn_dilated_g%d" % group,
    )(qkv, qkv, qkv, qkv, qkv, qkv, qkv, masks)


def _attn_c_kernel(q_ref, k_ref, vt_ref, o_ref, s_ref, smax_ref, m_ref, acc_ref, *, tk):
    nk = k_ref.shape[0] // tk
    m_ref[...] = jnp.full(m_ref.shape, MASK_NEG, F32)
    acc_ref[...] = jnp.zeros(acc_ref.shape, F32)

    def issue_scores(h, c, slot):
        start = pl.multiple_of(c * tk, tk)
        s = _dot_nt(k_ref[pl.ds(start, tk), :], q_ref[:, h * C_HEAD_DIM:(h + 1) * C_HEAD_DIM])
        s_ref[slot] = s
        smax_ref[slot] = s.max(axis=0, keepdims=True)

    issue_scores(0, 0, 0)

    def chunk(c, carry):
        vt = vt_ref[c]
        for h in range(C_GROUP):
            if h + 1 < C_GROUP:
                issue_scores(h + 1, c, (h + 1) % 2)
            else:
                issue_scores(0, jnp.minimum(c + 1, nk - 1), (h + 1) % 2)
            s = s_ref[h % 2]
            m_prev = m_ref[h]
            m_new = jnp.maximum(m_prev, smax_ref[h % 2])
            alpha = jnp.exp2(m_prev - m_new)
            pr = jnp.exp2(s - m_new)
            acc_ref[h] = alpha * acc_ref[h] + jnp.dot(vt, pr.astype(BF16), preferred_element_type=F32)
            m_ref[h] = m_new
        return carry

    lax.fori_loop(0, nk, chunk, 0)
    for h in range(C_GROUP):
        acc = acc_ref[h]
        out = acc[:C_HEAD_DIM] / acc[C_HEAD_DIM:C_HEAD_DIM + 1]
        o_ref[:, h * C_HEAD_DIM:(h + 1) * C_HEAD_DIM] = out.T.astype(BF16)


def _attn_c(qk, vt, *, batch, seq):
    tq, tk = C_TQ, C_TK
    nq = seq // tq
    assert seq % tk == 0 and C_GROUP % 2 == 0
    qw = C_GROUP * C_HEAD_DIM
    return pl.pallas_call(
        functools.partial(_attn_c_kernel, tk=tk),
        grid=(batch, C_KV_HEADS, nq),
        in_specs=[
            pl.BlockSpec((tq, qw), lambda b, g, i: (b * nq + i, g)),
            pl.BlockSpec((seq, C_HEAD_DIM), lambda b, g, i: (b, C_Q_HEADS + g)),
            pl.BlockSpec((None, seq // tk, C_VT_ROWS, tk), lambda b, g, i: (b, 0, g, 0)),
        ],
        out_specs=pl.BlockSpec((tq, qw), lambda b, g, i: (b * nq + i, g)),
        out_shape=jax.ShapeDtypeStruct((batch * seq, C_Q_HEADS * C_HEAD_DIM), BF16),
        scratch_shapes=[pltpu.VMEM((2, tk, tq), F32), pltpu.VMEM((2, 1, tq), F32),
                        pltpu.VMEM((C_GROUP, 1, tq), F32),
                        pltpu.VMEM((C_GROUP, C_VT_ROWS, tq), F32)],
        compiler_params=_cparams(("parallel", "parallel", "arbitrary")),
        name="attn_gqa_flash",
    )(qk, qk, vt)


def _post_kernel(*refs, merge, final, halves, tf, lead):
    refs = list(refs)
    x_ref = refs.pop(0)
    if merge:
        o_refs = [refs.pop(0) for _ in range(3)]
        lse_refs = [refs.pop(0) for _ in range(3)]
    else:
        o_ref_in = refs.pop(0)
    mod_ref, g_ref, wo_ref, w1_ref, w2_ref = [refs.pop(0) for _ in range(5)]
    fg_ref = refs.pop(0) if final else None
    out_ref = refs.pop(0)
    mod = mod_ref[0]
    hm = x_ref.shape[0] // halves

    def token_order(ref, hf, c):
        dil = ref.shape[0]
        rows = hm // dil
        part = ref[:, hf * rows:(hf + 1) * rows, c * LANES:(c + 1) * LANES].astype(F32)
        return part[0] if dil == 1 else jnp.swapaxes(part, 0, 1).reshape(hm, LANES)

    def merged_groups(hf):
        lse = [token_order(r, hf, 0) for r in lse_refs]
        top = jnp.maximum(jnp.maximum(lse[0], lse[1]), lse[2])
        ex = [jnp.exp2(v - top) for v in lse]
        den = ex[0] + ex[1] + ex[2]
        wgt = [e / den for e in ex]
        lane = lax.broadcasted_iota(jnp.int32, (hm, LANES), 1)
        cols = []
        for c in range(D_MODEL // LANES):
            head_of_lane = lane // HEAD_DIM_AB + c * (LANES // HEAD_DIM_AB)
            mixed = None
            for g in range(3):
                term = jnp.take_along_axis(wgt[g], head_of_lane, axis=1) * token_order(o_refs[g], hf, c)
                mixed = term if mixed is None else mixed + term
            cols.append(mixed.astype(BF16))
        return jnp.concatenate(cols, axis=1)

    def pre(hf):
        rows = slice(hf * hm, (hf + 1) * hm)
        o = merged_groups(hf) if merge else o_ref_in[rows, :]
        mix = jnp.dot(o, wo_ref[...], preferred_element_type=F32)
        x1 = x_ref[rows, :] + mod[:, 2 * D_MODEL:3 * D_MODEL] * mix
        h2 = _norm_mod(x1, g_ref[0], mod[:, 4 * D_MODEL:5 * D_MODEL], mod[:, 3 * D_MODEL:4 * D_MODEL])
        return x1, h2.astype(BF16)

    def mlp(h2, acc, chunks):
        for c in chunks:
            a = jnp.dot(h2, w1_ref[:, c * tf:(c + 1) * tf], preferred_element_type=F32)
            a = jnp.square(jnp.maximum(a, 0.0)).astype(BF16)
            d = jnp.dot(a, w2_ref[c * tf:(c + 1) * tf, :], preferred_element_type=F32)
            acc = d if acc is None else acc + d
        return acc

    def finish(hf, x1, acc):
        x2 = x1 + mod[:, 5 * D_MODEL:6 * D_MODEL] * acc
        if final:
            ms = jnp.mean(x2 * x2, axis=-1, keepdims=True)
            x2 = x2 * lax.rsqrt(ms + EPS) * fg_ref[...]
        out_ref[hf * hm:(hf + 1) * hm, :] = x2

    n_chunks = D_FF // tf
    cur = pre(0)
    for hf in range(halves):
        x1, h2 = cur
        acc = mlp(h2, None, range(0, lead))
        if hf + 1 < halves:
            cur = pre(hf + 1)
        finish(hf, x1, mlp(h2, acc, range(lead, n_chunks)))


def _post(x2d, attn, mods3, norm_g3, wo, w1, w2, final_g, *, layer, mod_row0, seq, merge, final):
    t = x2d.shape[0]
    tm = TM_POST_MERGE if merge else TM_POST
    halves = POST_HALVES_MERGE if merge else POST_HALVES
    hm = tm // halves
    per_seq = seq // tm
    row = lambda i: (i, 0)
    const = lambda i: (0, 0)
    resident = pl.Buffered(1)
    in_specs = [pl.BlockSpec((tm, D_MODEL), row)]
    args = [x2d]
    scratch = []
    if merge:
        outs, lses = attn
        def res_spec(arr):
            dil, width = arr.shape[1], arr.shape[3]
            assert hm % (dil * BF16_SUBLANES) == 0
            return pl.BlockSpec((None, dil, tm // dil, width), lambda i: (i // per_seq, 0, i % per_seq, 0))
        in_specs += [res_spec(a) for a in outs] + [res_spec(a) for a in lses]
        args += list(outs) + list(lses)
    else:
        in_specs += [pl.BlockSpec((tm, D_MODEL), row)]
        args += [attn]
    in_specs += [
        pl.BlockSpec((1, 1, 6 * D_MODEL), lambda i: (layer * MOD_ROWS + mod_row0 + i // per_seq, 0, 0)),
        pl.BlockSpec((1, 1, D_MODEL), lambda i: (2 * layer + 1, 0, 0)),
        pl.BlockSpec((D_MODEL, D_MODEL), const, pipeline_mode=resident),
        pl.BlockSpec((D_MODEL, D_FF), const, pipeline_mode=resident),
        pl.BlockSpec((D_FF, D_MODEL), const, pipeline_mode=resident),
    ]
    args += [mods3, norm_g3, wo, w1, w2]
    if final:
        in_specs += [pl.BlockSpec((1, D_MODEL), const)]
        args += [final_g.reshape(1, D_MODEL)]
    return pl.pallas_call(
        functools.partial(_post_kernel, merge=merge, final=final, halves=halves, tf=TF_POST,
                          lead=POST_LEAD_MERGE if merge else POST_LEAD),
        grid=(t // tm,),
        in_specs=in_specs,
        out_specs=pl.BlockSpec((tm, D_MODEL), row),
        out_shape=jax.ShapeDtypeStruct((t, D_MODEL), F32),
        scratch_shapes=scratch,
        compiler_params=_cparams(("parallel",)),
        name="wo_mlp_merge" if merge else "wo_mlp",
    )(*args)


def _trunk(x, mods3, mod_row0, norm_g3, final_g, wts, a_bias, b_masks, tabs_b, tabs_c):
    batch, seq, _ = x.shape
    assert seq <= tabs_b[0].shape[0]
    x2d = x.reshape(batch * seq, D_MODEL)
    for layer in range(DEPTH):
        kind, j = layer % N_MIXERS, layer // N_MIXERS
        common = dict(layer=layer, mod_row0=mod_row0, seq=seq)
        if kind == 0:
            qkv = _qkv_proj(x2d, mods3, norm_g3, wts["a_qkv"][j], mode="a", **common)
            attn = _attn_a(qkv, a_bias[j], batch=batch, seq=seq)
            wo = wts["a_o"][j]
        elif kind == 1:
            groups = []
            for g, (win, dil) in enumerate(B_PAIRS):
                assert win == 2 * B_SIDE * dil
                qkv = _qkv_proj(x2d, mods3, norm_g3, wts["b_qkv"][j], mode="b", cos=tabs_b[0], sin=tabs_b[1],
                                group=g, dil=dil, **common)
                groups.append(_attn_b_group(qkv, b_masks, group=g))
            attn = ([o for o, _ in groups], [l for _, l in groups])
            wo = wts["b_o"][j]
        else:
            qk, vt = _qkv_proj(x2d, mods3, norm_g3, wts["c_qkv"][j], mode="c", cos=tabs_c[0], sin=tabs_c[1],
                               qg=wts["c_qg"][j], kg=wts["c_kg"][j], **common)
            attn = _attn_c(qk, vt, batch=batch, seq=seq)
            wo = wts["c_o"][j]
        x2d = _post(x2d, attn, mods3, norm_g3, wo, wts["w1"][layer], wts["w2"][layer], final_g,
                    merge=(kind == 1), final=(layer == DEPTH - 1), **common)
    return x2d.reshape(batch, seq, D_MODEL)


def kernel(x_prompt, x_sample, c_prompt, c_sample, w_mod, b_mod, norm_g, final_g, a_w_qkv, a_rpb, a_w_o,
           b_w_qkv, b_w_o, c_w_qkv, c_q_g, c_k_g, c_w_o, mlp_w1, mlp_w2):
    nb_p, nb_s = c_prompt.shape[0], c_sample.shape[0]
    assert nb_p + nb_s <= MOD_ROWS
    c_all = jnp.concatenate([c_prompt, c_sample, jnp.zeros((MOD_ROWS - nb_p - nb_s, D_MODEL), F32)], axis=0)
    mods3 = _modulation(c_all, w_mod, b_mod).reshape(DEPTH * MOD_ROWS, 1, 6 * D_MODEL)
    norm_g3 = norm_g.reshape(DEPTH * 2, 1, D_MODEL)

    scale_ab = HEAD_DIM_AB ** -0.5 * LOG2E
    col_scale = np.ones((3 * D_MODEL,), np.float32)
    col_scale[:D_MODEL] = scale_ab
    a_qkv = (a_w_qkv * col_scale).astype(BF16)
    b_qkv = (b_w_qkv * np.tile(col_scale, len(B_PAIRS))).astype(BF16)
    wts = {
        "a_qkv": a_qkv, "a_o": a_w_o.astype(BF16),
        "b_qkv": b_qkv, "b_o": b_w_o.astype(BF16),
        "c_qkv": c_w_qkv.astype(BF16), "c_o": c_w_o.astype(BF16),
        "c_qg": (c_q_g * (C_HEAD_DIM ** -0.5 * LOG2E)).reshape(-1, 1, C_HEAD_DIM),
        "c_kg": c_k_g.reshape(-1, 1, C_HEAD_DIM),
        "w1": mlp_w1.astype(BF16), "w2": mlp_w2.astype(BF16),
    }
    a_bias = [_attn_a_bias(a_rpb[j]) for j in range(a_rpb.shape[0])]
    b_masks = jnp.asarray(_attn_b_masks())

    tabs_b, tabs_c = _rope_tables(max(x_prompt.shape[1], x_sample.shape[1]))
    shared = (norm_g3, final_g, wts, a_bias, b_masks, tabs_b, tabs_c)
    y_prompt = _trunk(x_prompt, mods3, 0, *shared)
    y_sample = _trunk(x_sample, mods3, nb_p, *shared)
    return (y_prompt, y_sample)
```

```python
import functools

import numpy as np
import jax
import jax.numpy as jnp
from jax import lax
from jax.experimental import pallas as pl
from jax.experimental.pallas import tpu as pltpu

F32 = jnp.float32
BF16 = jnp.bfloat16

D_MODEL = 1024
DEPTH = 4
GRID_W = 64
D_FF = 4 * D_MODEL
EPS = 1e-6
ROPE_THETA = 10000.0
N_MIXERS = 3
HEAD_DIM_AB = 64
HEADS_AB = D_MODEL // HEAD_DIM_AB
A_WIN_ROWS = 8
A_WIN_COLS = 16
B_PAIRS = ((128, 1), (512, 4), (2048, 16))
B_SIDE = 64
C_HEAD_DIM = 128
C_Q_HEADS = 8
C_KV_HEADS = 2
C_GROUP = C_Q_HEADS // C_KV_HEADS
C_QKV_WIDTH = (C_Q_HEADS + 2 * C_KV_HEADS) * C_HEAD_DIM
ROPE_HALF = 32
LANES = 128
BF16_SUBLANES = 16
MOD_ROWS = 8
MASK_NEG = -1e30
LOG2E = float(np.log2(np.e))
VMEM_LIMIT = 60 * 1024 * 1024

TN_MOD = 1536
TM_QKV = 1024
TN_QKV = 256
TN_QKV_C = 512
QKV_HALVES = 2
TM_POST = 1024
TM_POST_MERGE = 1024
POST_HALVES_MERGE = 4
POST_HALVES = 2
TF_POST = 1024
POST_LEAD = 0
POST_LEAD_MERGE = 2
A_TQ = 8 * GRID_W
A_AHEAD = 3
B_AHEAD = 1
B_TQ = 256
B_HALF = 128
C_TQ = 512
C_TK = 1024
C_UNROLL = 4
C_VT_ROWS = C_HEAD_DIM + BF16_SUBLANES


def _cparams(sem):
    return pltpu.CompilerParams(dimension_semantics=sem, vmem_limit_bytes=VMEM_LIMIT)


def _mod_kernel(c_ref, w_ref, b_ref, o_ref):
    c = c_ref[...]
    act = (c / (1.0 + jnp.exp(-c))).astype(BF16)
    o_ref[0] = jnp.dot(act, w_ref[0].astype(BF16), preferred_element_type=F32) + b_ref[0]


def _modulation(c_all, w_mod, b_mod):
    tn = TN_MOD
    n = 6 * D_MODEL
    return pl.pallas_call(
        _mod_kernel,
        grid=(DEPTH, n // tn),
        in_specs=[
            pl.BlockSpec((MOD_ROWS, D_MODEL), lambda l, j: (0, 0)),
            pl.BlockSpec((1, D_MODEL, tn), lambda l, j: (l, 0, j)),
            pl.BlockSpec((1, 1, tn), lambda l, j: (l, 0, j)),
        ],
        out_specs=pl.BlockSpec((1, MOD_ROWS, tn), lambda l, j: (l, 0, j)),
        out_shape=jax.ShapeDtypeStruct((DEPTH, MOD_ROWS, n), F32),
        compiler_params=_cparams(("parallel", "parallel")),
        name="adaln_modulation",
    )(c_all, w_mod, b_mod.reshape(DEPTH, 1, n))


def _norm_mod(x, gain, scale, shift):
    ms = jnp.mean(x * x, axis=-1, keepdims=True)
    return (x * lax.rsqrt(ms + EPS) * gain) * (1.0 + scale) + shift


def _rope128(x, cos, sin_signed, low_half):
    up = pltpu.roll(x, LANES - ROPE_HALF, 1)
    down = pltpu.roll(x, ROPE_HALF, 1)
    return x * cos + jnp.where(low_half, up, down) * sin_signed


def _low_half_mask():
    lane = lax.broadcasted_iota(jnp.int32, (1, LANES), 1)
    return (lane % (2 * ROPE_HALF)) < ROPE_HALF


def _rope_tables(seq):
    inv = ROPE_THETA ** (-jnp.arange(ROPE_HALF, dtype=F32) / ROPE_HALF)
    rows = seq // GRID_W

    def trig(pos):
        ang = pos.astype(F32)[:, None] * inv[None, :]
        return jnp.cos(ang), jnp.sin(ang)

    per_row = lambda x: jnp.repeat(x, GRID_W, axis=0)
    per_col = lambda x: jnp.tile(x, (rows, 1))
    c_row, s_row = (per_row(x) for x in trig(jnp.arange(rows)))
    c_col, s_col = (per_col(x) for x in trig(jnp.arange(GRID_W)))
    c_blk, s_blk = (per_row(x) for x in trig(jnp.arange(rows) * GRID_W))
    c_tok = c_blk * c_col - s_blk * s_col
    s_tok = s_blk * c_col + c_blk * s_col

    def table(ca, sa, cb, sb):
        return jnp.concatenate([ca, ca, cb, cb], axis=1), jnp.concatenate([-sa, sa, -sb, sb], axis=1)

    return table(c_tok, s_tok, c_tok, s_tok), table(c_row, s_row, c_col, s_col)


def _qkv_kernel(*refs, mode, dil, halves, tn):
    if mode == "a":
        x_ref, mod_ref, g_ref, w_ref, o_ref = refs
    elif mode == "b":
        x_ref, mod_ref, g_ref, w_ref, cos_ref, sin_ref, o_ref = refs
    else:
        x_ref, mod_ref, g_ref, w_ref, cos_ref, sin_ref, qg_ref, kg_ref, o_ref, vt_ref = refs
    mod = mod_ref[0]
    hm = x_ref.shape[0] // halves
    n = w_ref.shape[1]
    low = _low_half_mask()

    def normed(hf):
        rows = slice(hf * hm, (hf + 1) * hm)
        return _norm_mod(x_ref[rows, :], g_ref[0], mod[:, D_MODEL:2 * D_MODEL], mod[:, 0:D_MODEL]).astype(BF16)

    def emit(hf, j, acc):
        rows = slice(hf * hm, (hf + 1) * hm)
        if mode == "a":
            o_ref[rows, j * tn:(j + 1) * tn] = acc.astype(BF16)
            return
        cos, sin = cos_ref[rows, :], sin_ref[rows, :]
        for c in range(tn // LANES):
            col = j * tn + c * LANES
            val = acc[:, c * LANES:(c + 1) * LANES]
            if mode == "c":
                head = col // C_HEAD_DIM
                if head < C_Q_HEADS + C_KV_HEADS:
                    gain = qg_ref[...] if head < C_Q_HEADS else kg_ref[...]
                    ms = jnp.mean(val * val, axis=-1, keepdims=True)
                    val = _rope128(val * lax.rsqrt(ms + EPS) * gain, cos, sin, low)
                    o_ref[rows, col:col + LANES] = val.astype(BF16)
                else:
                    kv = head - C_Q_HEADS - C_KV_HEADS
                    vt_ref[kv * C_VT_ROWS:kv * C_VT_ROWS + C_HEAD_DIM, rows] = val.T.astype(BF16)
                    vt_ref[kv * C_VT_ROWS + C_HEAD_DIM:(kv + 1) * C_VT_ROWS, rows] = jnp.ones(
                        (C_VT_ROWS - C_HEAD_DIM, hm), BF16)
                continue
            if col < 2 * D_MODEL:
                val = _rope128(val, cos, sin, low)
            sub = hm // dil
            dst = slice(hf * sub, (hf + 1) * sub)
            if dil == 1:
                o_ref[0, 0, dst, col:col + LANES] = val.astype(BF16)
                continue
            by_residue = jnp.swapaxes(val.reshape(sub, dil, LANES), 0, 1)
            o_ref[0, :, dst, col:col + LANES] = by_residue.astype(BF16)

    hs = [normed(hf) for hf in range(halves)]
    for hf in range(halves):
        for j in range(n // tn):
            emit(hf, j, jnp.dot(hs[hf], w_ref[:, j * tn:(j + 1) * tn], preferred_element_type=F32))


def _qkv_proj(x2d, mods3, norm_g3, w, *, layer, mod_row0, seq, mode, cos=None, sin=None, qg=None, kg=None,
              group=0, dil=1):
    t = x2d.shape[0]
    tm = TM_QKV
    if mode == "b":
        n, col0 = 3 * D_MODEL, group
    else:
        n, col0 = w.shape[1], 0
    tn = TN_QKV_C if mode == "c" else TN_QKV
    hm = tm // QKV_HALVES
    per_seq = seq // tm
    in_specs = [
        pl.BlockSpec((tm, D_MODEL), lambda i: (i, 0)),
        pl.BlockSpec((1, 1, 6 * D_MODEL), lambda i: (layer * MOD_ROWS + mod_row0 + i // per_seq, 0, 0)),
        pl.BlockSpec((1, 1, D_MODEL), lambda i: (2 * layer, 0, 0)),
        pl.BlockSpec((D_MODEL, n), lambda i: (0, col0), pipeline_mode=pl.Buffered(1)),
    ]
    args = [x2d, mods3, norm_g3, w]
    if mode in ("b", "c"):
        in_specs += [pl.BlockSpec((tm, LANES), lambda i: (i % per_seq, 0))] * 2
        args += [cos, sin]
    if mode == "c":
        in_specs += [pl.BlockSpec((1, C_HEAD_DIM), lambda i: (0, 0))] * 2
        args += [qg, kg]
    scratch = []
    if mode == "b":
        assert hm % (dil * BF16_SUBLANES) == 0
        batch = t // seq
        out_spec = pl.BlockSpec((1, dil, tm // dil, n), lambda i: (i // per_seq, 0, i % per_seq, 0))
        out_shape = jax.ShapeDtypeStruct((batch, dil, seq // dil, n), BF16)
    elif mode == "c":
        assert tm == C_TK
        n_qk = (C_Q_HEADS + C_KV_HEADS) * C_HEAD_DIM
        kv_w = C_KV_HEADS * C_VT_ROWS
        out_spec = [pl.BlockSpec((tm, n_qk), lambda i: (i, 0)),
                    pl.BlockSpec((None, None, kv_w, tm), lambda i: (i // per_seq, i % per_seq, 0, 0))]
        out_shape = [jax.ShapeDtypeStruct((t, n_qk), BF16),
                     jax.ShapeDtypeStruct((t // seq, per_seq, kv_w, tm), BF16)]
    else:
        out_spec = pl.BlockSpec((tm, n), lambda i: (i, 0))
        out_shape = jax.ShapeDtypeStruct((t, n), BF16)
    return pl.pallas_call(
        functools.partial(_qkv_kernel, mode=mode, dil=dil, halves=QKV_HALVES, tn=tn),
        grid=(t // tm,),
        in_specs=in_specs,
        out_specs=out_spec,
        out_shape=out_shape,
        scratch_shapes=scratch,
        compiler_params=_cparams(("parallel",)),
        name="norm_qkv_" + mode,
    )(*args)


def _pair_masks():
    lane = lax.broadcasted_iota(jnp.int32, (1, LANES), 1)
    first = lane < HEAD_DIM_AB
    return first, jnp.logical_not(first)


def _dot_nt(a, b):
    return lax.dot_general(a, b, (((1,), (1,)), ((), ())), preferred_element_type=F32)


def _pipelined(units, scores, finish, ahead=1):
    pending = [scores(u) for u in units[:ahead]]
    for i, unit in enumerate(units):
        if i + ahead < len(units):
            pending.append(scores(units[i + ahead]))
        finish(unit, pending.pop(0))


def _attn_a_kernel(q_ref, kbuf, vbuf, bias_ref, o_ref):
    tq = q_ref.shape[0]
    rows_per = tq // GRID_W
    half = A_WIN_ROWS // 2
    win_keys = A_WIN_ROWS * GRID_W
    first = pl.program_id(1) == 0
    last = pl.program_id(1) == pl.num_programs(1) - 1
    masks = _pair_masks()
    units = [(p, rl) for p in range(HEADS_AB // 2) for rl in range(rows_per)]

    def window_of(rl):
        at_start, at_end = max(rl - half, 0), min(rl + half, rows_per)
        start = jnp.where(first, at_start, jnp.where(last, at_end, rl))
        win = jnp.where(first, at_start - rl + A_WIN_ROWS - 1,
                        jnp.where(last, at_end - rl - 1, A_WIN_ROWS - 1 - half))
        return pl.multiple_of(start * GRID_W, GRID_W), win

    def scores(unit):
        p, rl = unit
        sl = slice(p * LANES, (p + 1) * LANES)
        start, win = window_of(rl)
        qp = q_ref[rl * GRID_W:(rl + 1) * GRID_W, sl]
        stacked = jnp.concatenate([jnp.where(mk, qp, jnp.zeros_like(qp)) for mk in masks], axis=0)
        s = _dot_nt(stacked, kbuf[pl.ds(start, win_keys), sl]) + bias_ref[p, win]
        return s, s.max(axis=-1, keepdims=True)

    def finish(unit, scored):
        s, m = scored
        p, rl = unit
        sl = slice(p * LANES, (p + 1) * LANES)
        start, _ = window_of(rl)
        pr = jnp.exp2(s - m)
        l = pr.sum(axis=-1, keepdims=True)
        o = jnp.dot(pr.astype(BF16), vbuf[pl.ds(start, win_keys), sl], preferred_element_type=F32) / l
        o_ref[rl * GRID_W:(rl + 1) * GRID_W, sl] = jnp.where(masks[0], o[:GRID_W], o[GRID_W:]).astype(BF16)

    _pipelined(units, scores, finish, ahead=A_AHEAD)


def _attn_a_bias(rpb):
    n_dc = 2 * A_WIN_COLS - 1
    c, kc = np.arange(GRID_W)[:, None], np.arange(GRID_W)[None, :]
    onehot_c = ((kc - c + A_WIN_COLS - 1)[..., None] == np.arange(n_dc)).astype(np.float32)
    toe = jnp.einsum("hde,cje->hdcj", rpb * LOG2E, onehot_c, precision=lax.Precision.HIGHEST)
    cs = np.clip(c - A_WIN_COLS // 2, 0, GRID_W - A_WIN_COLS)
    col_ok = (kc >= cs) & (kc < cs + A_WIN_COLS)
    toe = jnp.where(jnp.asarray(col_ok)[None, None], toe, MASK_NEG)
    wins = jnp.stack([jnp.concatenate([toe[:, w + m] for m in range(A_WIN_ROWS)], axis=-1)
                      for w in range(A_WIN_ROWS)], axis=1)
    wins = wins.reshape(HEADS_AB // 2, 2, A_WIN_ROWS, GRID_W, A_WIN_ROWS * GRID_W)
    return jnp.transpose(wins, (0, 2, 1, 3, 4)).reshape(HEADS_AB // 2, A_WIN_ROWS, 2 * GRID_W, A_WIN_ROWS * GRID_W)


def _attn_a(qkv, bias, *, batch, seq):
    tq = A_TQ
    nb = seq // tq
    rows_per = tq // GRID_W
    side = (A_WIN_ROWS // 2) * GRID_W
    buf = tq + 2 * side
    assert nb >= 2 and rows_per >= A_WIN_ROWS // 2 and tq % side == 0 and seq >= buf
    q_spec = pl.BlockSpec((tq, D_MODEL), lambda b, i: (b * nb + i, 0))
    def kv_spec(col):
        def index(b, i):
            row = b * seq + jnp.clip(i * tq - side, 0, seq - buf)
            return pl.multiple_of(row, side), col * D_MODEL
        return pl.BlockSpec((pl.Element(buf), pl.Element(D_MODEL)), index)
    bias_spec = pl.BlockSpec(bias.shape, lambda b, i: (0, 0, 0, 0), pipeline_mode=pl.Buffered(1))
    return pl.pallas_call(
        _attn_a_kernel,
        grid=(batch, nb),
        in_specs=[q_spec, kv_spec(1), kv_spec(2), bias_spec],
        out_specs=pl.BlockSpec((tq, D_MODEL), lambda b, i: (b * nb + i, 0)),
        out_shape=jax.ShapeDtypeStruct((batch * seq, D_MODEL), BF16),
        compiler_params=_cparams(("parallel", "arbitrary")),
        name="attn_neighbourhood",
    )(qkv, qkv, qkv, bias)


def _attn_b_kernel(q_ref, kp_ref, kc_ref, kn_ref, vp_ref, vc_ref, vn_ref, mask_ref, o_ref, lse_ref):
    tq = q_ref.shape[0]
    masks = _pair_masks()
    lane = lax.broadcasted_iota(jnp.int32, (1, LANES), 1)
    lse_tiles = [jnp.zeros((B_HALF, LANES), F32) for _ in range(tq // B_HALF)]
    units = [(p, hf) for p in range(HEADS_AB // 2) for hf in range(tq // B_HALF)]

    def window(prev_ref, cur_ref, next_ref, hf, sl):
        lo, hi = hf * B_HALF - B_SIDE, (hf + 1) * B_HALF + B_SIDE
        parts = []
        if lo < 0:
            parts.append(prev_ref[tq + lo:, sl])
        parts.append(cur_ref[max(lo, 0):min(hi, tq), sl])
        if hi > tq:
            parts.append(next_ref[:hi - tq, sl])
        return jnp.concatenate(parts, axis=0)

    def scores(unit):
        p, hf = unit
        sl = slice(p * LANES, (p + 1) * LANES)
        qp = q_ref[hf * B_HALF:(hf + 1) * B_HALF, sl]
        stacked = jnp.concatenate([jnp.where(mk, qp, jnp.zeros_like(qp)) for mk in masks], axis=0)
        s = _dot_nt(stacked, window(kp_ref, kc_ref, kn_ref, hf, sl)) + mask_ref[0, hf]
        return s, s.max(axis=-1, keepdims=True)

    def finish(unit, scored):
        s, m = scored
        p, hf = unit
        sl = slice(p * LANES, (p + 1) * LANES)
        pr = jnp.exp2(s - m)
        l = pr.sum(axis=-1, keepdims=True)
        o = jnp.dot(pr.astype(BF16), window(vp_ref, vc_ref, vn_ref, hf, sl), preferred_element_type=F32) / l
        o_ref[hf * B_HALF:(hf + 1) * B_HALF, sl] = jnp.where(masks[0], o[:B_HALF], o[B_HALF:]).astype(BF16)
        lse = m + jnp.log2(l)
        tile = jnp.where(lane == 2 * p, lse[:B_HALF], lse_tiles[hf])
        lse_tiles[hf] = jnp.where(lane == 2 * p + 1, lse[B_HALF:], tile)

    _pipelined(units, scores, finish, ahead=B_AHEAD)
    for hf, tile in enumerate(lse_tiles):
        lse_ref[hf * B_HALF:(hf + 1) * B_HALF, :] = tile


def _attn_b_masks():
    qq = np.arange(B_HALF)[:, None]
    jj = np.arange(B_HALF + 2 * B_SIDE)[None, :]
    band = (jj - qq >= 0) & (jj - qq <= 2 * B_SIDE)
    out = []
    for ty in range(4):
        per_half = []
        for hf in range(B_TQ // B_HALF):
            pos = hf * B_HALF - B_SIDE + jj
            ok = band
            if ty & 1:
                ok = ok & (pos >= 0)
            if ty & 2:
                ok = ok & (pos < B_TQ)
            half = np.where(ok, 0.0, MASK_NEG)
            per_half.append(np.concatenate([half, half], axis=0))
        out.append(np.stack(per_half))
    return np.stack(out).astype(np.float32)


def _attn_b_group(qkv, masks, *, group):
    batch, dil, sub, _ = qkv.shape
    tq = B_TQ
    nb = sub // tq
    assert sub % tq == 0
    q_spec = pl.BlockSpec((None, None, tq, D_MODEL), lambda b, r, i: (b, r, i, 0))
    def kv_spec(which, off):
        return pl.BlockSpec((None, None, tq, D_MODEL),
                            lambda b, r, i: (b, r, jnp.clip(i + off, 0, nb - 1), which))
    mask_spec = pl.BlockSpec((1,) + masks.shape[1:],
                             lambda b, r, i: ((i == 0).astype(jnp.int32) + 2 * (i == nb - 1).astype(jnp.int32),
                                              0, 0, 0))
    return pl.pallas_call(
        _attn_b_kernel,
        grid=(batch, dil, nb),
        in_specs=[q_spec, kv_spec(1, -1), kv_spec(1, 0), kv_spec(1, 1),
                  kv_spec(2, -1), kv_spec(2, 0), kv_spec(2, 1), mask_spec],
        out_specs=[pl.BlockSpec((None, None, tq, D_MODEL), lambda b, r, i: (b, r, i, 0)),
                   pl.BlockSpec((None, None, tq, LANES), lambda b, r, i: (b, r, i, 0))],
        out_shape=[jax.ShapeDtypeStruct((batch, dil, sub, D_MODEL), BF16),
                   jax.ShapeDtypeStruct((batch, dil, sub, LANES), F32)],
        compiler_params=_cparams(("parallel", "parallel", "arbitrary")),
        name="attn_dilated_g%d" % group,
    )(qkv, qkv, qkv, qkv, qkv, qkv, qkv, masks)


def _attn_c_kernel(q_ref, k_ref, vt_ref, o_ref, s_ref, smax_ref, m_ref, acc_ref, *, tk):
    nk = k_ref.shape[0] // tk
    m_ref[...] = jnp.full(m_ref.shape, MASK_NEG, F32)
    acc_ref[...] = jnp.zeros(acc_ref.shape, F32)

    def issue_scores(h, c, slot):
        start = pl.multiple_of(c * tk, tk)
        s = _dot_nt(k_ref[pl.ds(start, tk), :], q_ref[:, h * C_HEAD_DIM:(h + 1) * C_HEAD_DIM])
        s_ref[slot] = s
        smax_ref[slot] = s.max(axis=0, keepdims=True)

    issue_scores(0, 0, 0)

    def chunk(c, carry):
        vt = vt_ref[c]
        for h in range(C_GROUP):
            if h + 1 < C_GROUP:
                issue_scores(h + 1, c, (h + 1) % 2)
            else:
                issue_scores(0, jnp.minimum(c + 1, nk - 1), (h + 1) % 2)
            s = s_ref[h % 2]
            m_prev = m_ref[h]
            m_new = jnp.maximum(m_prev, smax_ref[h % 2])
            alpha = jnp.exp2(m_prev - m_new)
            pr = jnp.exp2(s - m_new)
            acc_ref[h] = alpha * acc_ref[h] + jnp.dot(vt, pr.astype(BF16), preferred_element_type=F32)
            m_ref[h] = m_new
        return carry

    def unrolled_chunks(cu, carry):
        for u in range(C_UNROLL):
            chunk(C_UNROLL * cu + u, carry)
        return carry

    lax.fori_loop(0, nk // C_UNROLL, unrolled_chunks, 0)
    for h in range(C_GROUP):
        acc = acc_ref[h]
        out = acc[:C_HEAD_DIM] / acc[C_HEAD_DIM:C_HEAD_DIM + 1]
        o_ref[:, h * C_HEAD_DIM:(h + 1) * C_HEAD_DIM] = out.T.astype(BF16)


def _attn_c(qk, vt, *, batch, seq):
    tq, tk = C_TQ, C_TK
    nq = seq // tq
    assert seq % (tk * C_UNROLL) == 0 and C_GROUP % 2 == 0
    qw = C_GROUP * C_HEAD_DIM
    return pl.pallas_call(
        functools.partial(_attn_c_kernel, tk=tk),
        grid=(batch, C_KV_HEADS, nq),
        in_specs=[
            pl.BlockSpec((tq, qw), lambda b, g, i: (b * nq + i, g)),
            pl.BlockSpec((seq, C_HEAD_DIM), lambda b, g, i: (b, C_Q_HEADS + g)),
            pl.BlockSpec((None, seq // tk, C_VT_ROWS, tk), lambda b, g, i: (b, 0, g, 0)),
        ],
        out_specs=pl.BlockSpec((tq, qw), lambda b, g, i: (b * nq + i, g)),
        out_shape=jax.ShapeDtypeStruct((batch * seq, C_Q_HEADS * C_HEAD_DIM), BF16),
        scratch_shapes=[pltpu.VMEM((2, tk, tq), F32), pltpu.VMEM((2, 1, tq), F32),
                        pltpu.VMEM((C_GROUP, 1, tq), F32),
                        pltpu.VMEM((C_GROUP, C_VT_ROWS, tq), F32)],
        compiler_params=_cparams(("parallel", "parallel", "arbitrary")),
        name="attn_gqa_flash",
    )(qk, qk, vt)


def _post_kernel(*refs, merge, final, halves, tf, lead):
    refs = list(refs)
    x_ref = refs.pop(0)
    if merge:
        o_refs = [refs.pop(0) for _ in range(3)]
        lse_refs = [refs.pop(0) for _ in range(3)]
    else:
        o_ref_in = refs.pop(0)
    mod_ref, g_ref, wo_ref, w1_ref, w2_ref = [refs.pop(0) for _ in range(5)]
    fg_ref = refs.pop(0) if final else None
    out_ref = refs.pop(0)
    mod = mod_ref[0]
    hm = x_ref.shape[0] // halves

    def token_order(ref, hf, c):
        dil = ref.shape[0]
        rows = hm // dil
        part = ref[:, hf * rows:(hf + 1) * rows, c * LANES:(c + 1) * LANES].astype(F32)
        return part[0] if dil == 1 else jnp.swapaxes(part, 0, 1).reshape(hm, LANES)

    def merged_groups(hf):
        lse = [token_order(r, hf, 0) for r in lse_refs]
        top = jnp.maximum(jnp.maximum(lse[0], lse[1]), lse[2])
        ex = [jnp.exp2(v - top) for v in lse]
        den = ex[0] + ex[1] + ex[2]
        wgt = [e / den for e in ex]
        lane = lax.broadcasted_iota(jnp.int32, (hm, LANES), 1)
        cols = []
        for c in range(D_MODEL // LANES):
            head_of_lane = lane // HEAD_DIM_AB + c * (LANES // HEAD_DIM_AB)
            mixed = None
            for g in range(3):
                term = jnp.take_along_axis(wgt[g], head_of_lane, axis=1) * token_order(o_refs[g], hf, c)
                mixed = term if mixed is None else mixed + term
            cols.append(mixed.astype(BF16))
        return jnp.concatenate(cols, axis=1)

    def pre(hf):
        rows = slice(hf * hm, (hf + 1) * hm)
        o = merged_groups(hf) if merge else o_ref_in[rows, :]
        mix = jnp.dot(o, wo_ref[...], preferred_element_type=F32)
        x1 = x_ref[rows, :] + mod[:, 2 * D_MODEL:3 * D_MODEL] * mix
        h2 = _norm_mod(x1, g_ref[0], mod[:, 4 * D_MODEL:5 * D_MODEL], mod[:, 3 * D_MODEL:4 * D_MODEL])
        return x1, h2.astype(BF16)

    def mlp(h2, acc, chunks):
        for c in chunks:
            a = jnp.dot(h2, w1_ref[:, c * tf:(c + 1) * tf], preferred_element_type=F32)
            a = jnp.square(jnp.maximum(a, 0.0)).astype(BF16)
            d = jnp.dot(a, w2_ref[c * tf:(c + 1) * tf, :], preferred_element_type=F32)
            acc = d if acc is None else acc + d
        return acc

    def finish(hf, x1, acc):
        x2 = x1 + mod[:, 5 * D_MODEL:6 * D_MODEL] * acc
        if final:
            ms = jnp.mean(x2 * x2, axis=-1, keepdims=True)
            x2 = x2 * lax.rsqrt(ms + EPS) * fg_ref[...]
        out_ref[hf * hm:(hf + 1) * hm, :] = x2

    n_chunks = D_FF // tf
    cur = pre(0)
    for hf in range(halves):
        x1, h2 = cur
        acc = mlp(h2, None, range(0, lead))
        if hf + 1 < halves:
            cur = pre(hf + 1)
        finish(hf, x1, mlp(h2, acc, range(lead, n_chunks)))


def _post(x2d, attn, mods3, norm_g3, wo, w1, w2, final_g, *, layer, mod_row0, seq, merge, final):
    t = x2d.shape[0]
    tm = TM_POST_MERGE if merge else TM_POST
    halves = POST_HALVES_MERGE if merge else POST_HALVES
    hm = tm // halves
    per_seq = seq // tm
    row = lambda i: (i, 0)
    const = lambda i: (0, 0)
    resident = pl.Buffered(1)
    in_specs = [pl.BlockSpec((tm, D_MODEL), row)]
    args = [x2d]
    scratch = []
    if merge:
        outs, lses = attn
        def res_spec(arr):
            dil, width = arr.shape[1], arr.shape[3]
            assert hm % (dil * BF16_SUBLANES) == 0
            return pl.BlockSpec((None, dil, tm // dil, width), lambda i: (i // per_seq, 0, i % per_seq, 0))
        in_specs += [res_spec(a) for a in outs] + [res_spec(a) for a in lses]
        args += list(outs) + list(lses)
    else:
        in_specs += [pl.BlockSpec((tm, D_MODEL), row)]
        args += [attn]
    in_specs += [
        pl.BlockSpec((1, 1, 6 * D_MODEL), lambda i: (layer * MOD_ROWS + mod_row0 + i // per_seq, 0, 0)),
        pl.BlockSpec((1, 1, D_MODEL), lambda i: (2 * layer + 1, 0, 0)),
        pl.BlockSpec((D_MODEL, D_MODEL), const, pipeline_mode=resident),
        pl.BlockSpec((D_MODEL, D_FF), const, pipeline_mode=resident),
        pl.BlockSpec((D_FF, D_MODEL), const, pipeline_mode=resident),
    ]
    args += [mods3, norm_g3, wo, w1, w2]
    if final:
        in_specs += [pl.BlockSpec((1, D_MODEL), const)]
        args += [final_g.reshape(1, D_MODEL)]
    return pl.pallas_call(
        functools.partial(_post_kernel, merge=merge, final=final, halves=halves, tf=TF_POST,
                          lead=POST_LEAD_MERGE if merge else POST_LEAD),
        grid=(t // tm,),
        in_specs=in_specs,
        out_specs=pl.BlockSpec((tm, D_MODEL), row),
        out_shape=jax.ShapeDtypeStruct((t, D_MODEL), F32),
        scratch_shapes=scratch,
        compiler_params=_cparams(("parallel",)),
        name="wo_mlp_merge" if merge else "wo_mlp",
    )(*args)


def _trunk(x, mods3, mod_row0, norm_g3, final_g, wts, a_bias, b_masks, tabs_b, tabs_c):
    batch, seq, _ = x.shape
    assert seq <= tabs_b[0].shape[0]
    x2d = x.reshape(batch * seq, D_MODEL)
    for layer in range(DEPTH):
        kind, j = layer % N_MIXERS, layer // N_MIXERS
        common = dict(layer=layer, mod_row0=mod_row0, seq=seq)
        if kind == 0:
            qkv = _qkv_proj(x2d, mods3, norm_g3, wts["a_qkv"][j], mode="a", **common)
            attn = _attn_a(qkv, a_bias[j], batch=batch, seq=seq)
            wo = wts["a_o"][j]
        elif kind == 1:
            groups = []
            for g, (win, dil) in enumerate(B_PAIRS):
                assert win == 2 * B_SIDE * dil
                qkv = _qkv_proj(x2d, mods3, norm_g3, wts["b_qkv"][j], mode="b", cos=tabs_b[0], sin=tabs_b[1],
                                group=g, dil=dil, **common)
                groups.append(_attn_b_group(qkv, b_masks, group=g))
            attn = ([o for o, _ in groups], [l for _, l in groups])
            wo = wts["b_o"][j]
        else:
            qk, vt = _qkv_proj(x2d, mods3, norm_g3, wts["c_qkv"][j], mode="c", cos=tabs_c[0], sin=tabs_c[1],
                               qg=wts["c_qg"][j], kg=wts["c_kg"][j], **common)
            attn = _attn_c(qk, vt, batch=batch, seq=seq)
            wo = wts["c_o"][j]
        x2d = _post(x2d, attn, mods3, norm_g3, wo, wts["w1"][layer], wts["w2"][layer], final_g,
                    merge=(kind == 1), final=(layer == DEPTH - 1), **common)
    return x2d.reshape(batch, seq, D_MODEL)


def kernel(x_prompt, x_sample, c_prompt, c_sample, w_mod, b_mod, norm_g, final_g, a_w_qkv, a_rpb, a_w_o,
           b_w_qkv, b_w_o, c_w_qkv, c_q_g, c_k_g, c_w_o, mlp_w1, mlp_w2):
    nb_p, nb_s = c_prompt.shape[0], c_sample.shape[0]
    assert nb_p + nb_s <= MOD_ROWS
    c_all = jnp.concatenate([c_prompt, c_sample, jnp.zeros((MOD_ROWS - nb_p - nb_s, D_MODEL), F32)], axis=0)
    mods3 = _modulation(c_all, w_mod, b_mod).reshape(DEPTH * MOD_ROWS, 1, 6 * D_MODEL)
    norm_g3 = norm_g.reshape(DEPTH * 2, 1, D_MODEL)

    scale_ab = HEAD_DIM_AB ** -0.5 * LOG2E
    col_scale = np.ones((3 * D_MODEL,), np.float32)
    col_scale[:D_MODEL] = scale_ab
    a_qkv = (a_w_qkv * col_scale).astype(BF16)
    b_qkv = (b_w_qkv * np.tile(col_scale, len(B_PAIRS))).astype(BF16)
    wts = {
        "a_qkv": a_qkv, "a_o": a_w_o.astype(BF16),
        "b_qkv": b_qkv, "b_o": b_w_o.astype(BF16),
        "c_qkv": c_w_qkv.astype(BF16), "c_o": c_w_o.astype(BF16),
        "c_qg": (c_q_g * (C_HEAD_DIM ** -0.5 * LOG2E)).reshape(-1, 1, C_HEAD_DIM),
        "c_kg": c_k_g.reshape(-1, 1, C_HEAD_DIM),
        "w1": mlp_w1.astype(BF16), "w2": mlp_w2.astype(BF16),
    }
    a_bias = [_attn_a_bias(a_rpb[j]) for j in range(a_rpb.shape[0])]
    b_masks = jnp.asarray(_attn_b_masks())

    tabs_b, tabs_c = _rope_tables(max(x_prompt.shape[1], x_sample.shape[1]))
    shared = (norm_g3, final_g, wts, a_bias, b_masks, tabs_b, tabs_c)
    y_prompt = _trunk(x_prompt, mods3, 0, *shared)
    y_sample = _trunk(x_sample, mods3, nb_p, *shared)
    return (y_prompt, y_sample)
```

```python
import functools

import numpy as np
import jax
import jax.numpy as jnp
from jax import lax
from jax.experimental import pallas as pl
from jax.experimental.pallas import tpu as pltpu

F32 = jnp.float32
BF16 = jnp.bfloat16

D_MODEL = 1024
DEPTH = 4
GRID_W = 64
D_FF = 4 * D_MODEL
EPS = 1e-6
ROPE_THETA = 10000.0
N_MIXERS = 3
HEAD_DIM_AB = 64
HEADS_AB = D_MODEL // HEAD_DIM_AB
A_WIN_ROWS = 8
A_WIN_COLS = 16
B_PAIRS = ((128, 1), (512, 4), (2048, 16))
B_SIDE = 64
C_HEAD_DIM = 128
C_Q_HEADS = 8
C_KV_HEADS = 2
C_GROUP = C_Q_HEADS // C_KV_HEADS
C_QKV_WIDTH = (C_Q_HEADS + 2 * C_KV_HEADS) * C_HEAD_DIM
ROPE_HALF = 32
LANES = 128
BF16_SUBLANES = 16
MOD_ROWS = 8
MASK_NEG = -1e30
LOG2E = float(np.log2(np.e))
VMEM_LIMIT = 60 * 1024 * 1024

TN_MOD = 1536
TM_QKV = 1024
TN_QKV = 256
TN_QKV_C = 512
QKV_HALVES = 2
TM_POST = 1024
TM_POST_MERGE = 1024
POST_HALVES_MERGE = 4
POST_HALVES = 2
TF_POST = 1024
POST_LEAD = 0
POST_LEAD_MERGE = 2
A_TQ = 8 * GRID_W
A_AHEAD = 3
B_AHEAD = 1
B_TQ = 256
B_HALF = 128
C_TQ = 512
C_TK = 1024
C_UNROLL = 4
C_VT_ROWS = C_HEAD_DIM + BF16_SUBLANES


def _cparams(sem):
    return pltpu.CompilerParams(dimension_semantics=sem, vmem_limit_bytes=VMEM_LIMIT)


def _mod_kernel(c_ref, w_ref, b_ref, o_ref):
    c = c_ref[...]
    act = (c / (1.0 + jnp.exp(-c))).astype(BF16)
    o_ref[0] = jnp.dot(act, w_ref[0].astype(BF16), preferred_element_type=F32) + b_ref[0]


def _modulation(c_all, w_mod, b_mod):
    tn = TN_MOD
    n = 6 * D_MODEL
    return pl.pallas_call(
        _mod_kernel,
        grid=(DEPTH, n // tn),
        in_specs=[
            pl.BlockSpec((MOD_ROWS, D_MODEL), lambda l, j: (0, 0)),
            pl.BlockSpec((1, D_MODEL, tn), lambda l, j: (l, 0, j)),
            pl.BlockSpec((1, 1, tn), lambda l, j: (l, 0, j)),
        ],
        out_specs=pl.BlockSpec((1, MOD_ROWS, tn), lambda l, j: (l, 0, j)),
        out_shape=jax.ShapeDtypeStruct((DEPTH, MOD_ROWS, n), F32),
        compiler_params=_cparams(("parallel", "parallel")),
        name="adaln_modulation",
    )(c_all, w_mod, b_mod.reshape(DEPTH, 1, n))


def _norm_mod(x, gain, scale, shift):
    ms = jnp.mean(x * x, axis=-1, keepdims=True)
    return (x * lax.rsqrt(ms + EPS) * gain) * (1.0 + scale) + shift


def _rope128(x, cos, sin_signed, low_half):
    up = pltpu.roll(x, LANES - ROPE_HALF, 1)
    down = pltpu.roll(x, ROPE_HALF, 1)
    return x * cos + jnp.where(low_half, up, down) * sin_signed


def _low_half_mask():
    lane = lax.broadcasted_iota(jnp.int32, (1, LANES), 1)
    return (lane % (2 * ROPE_HALF)) < ROPE_HALF


def _rope_tables(seq):
    inv = ROPE_THETA ** (-jnp.arange(ROPE_HALF, dtype=F32) / ROPE_HALF)
    rows = seq // GRID_W

    def trig(pos):
        ang = pos.astype(F32)[:, None] * inv[None, :]
        return jnp.cos(ang), jnp.sin(ang)

    per_row = lambda x: jnp.repeat(x, GRID_W, axis=0)
    per_col = lambda x: jnp.tile(x, (rows, 1))
    c_row, s_row = (per_row(x) for x in trig(jnp.arange(rows)))
    c_col, s_col = (per_col(x) for x in trig(jnp.arange(GRID_W)))
    c_blk, s_blk = (per_row(x) for x in trig(jnp.arange(rows) * GRID_W))
    c_tok = c_blk * c_col - s_blk * s_col
    s_tok = s_blk * c_col + c_blk * s_col

    def table(ca, sa, cb, sb):
        return jnp.concatenate([ca, ca, cb, cb], axis=1), jnp.concatenate([-sa, sa, -sb, sb], axis=1)

    return table(c_tok, s_tok, c_tok, s_tok), table(c_row, s_row, c_col, s_col)


def _qkv_kernel(*refs, mode, dil, halves, tn):
    if mode == "a":
        x_ref, mod_ref, g_ref, w_ref, o_ref = refs
    elif mode == "b":
        x_ref, mod_ref, g_ref, w_ref, cos_ref, sin_ref, o_ref = refs
    else:
        x_ref, mod_ref, g_ref, w_ref, cos_ref, sin_ref, qg_ref, kg_ref, o_ref, vt_ref = refs
    mod = mod_ref[0]
    hm = x_ref.shape[0] // halves
    n = w_ref.shape[1]
    low = _low_half_mask()

    def normed(hf):
        rows = slice(hf * hm, (hf + 1) * hm)
        return _norm_mod(x_ref[rows, :], g_ref[0], mod[:, D_MODEL:2 * D_MODEL], mod[:, 0:D_MODEL]).astype(BF16)

    def emit(hf, j, acc):
        rows = slice(hf * hm, (hf + 1) * hm)
        if mode == "a":
            o_ref[rows, j * tn:(j + 1) * tn] = acc.astype(BF16)
            return
        cos, sin = cos_ref[rows, :], sin_ref[rows, :]
        for c in range(tn // LANES):
            col = j * tn + c * LANES
            val = acc[:, c * LANES:(c + 1) * LANES]
            if mode == "c":
                head = col // C_HEAD_DIM
                if head < C_Q_HEADS + C_KV_HEADS:
                    gain = qg_ref[...] if head < C_Q_HEADS else kg_ref[...]
                    ms = jnp.mean(val * val, axis=-1, keepdims=True)
                    val = _rope128(val * lax.rsqrt(ms + EPS) * gain, cos, sin, low)
                    o_ref[rows, col:col + LANES] = val.astype(BF16)
                else:
                    kv = head - C_Q_HEADS - C_KV_HEADS
                    vt_ref[kv * C_VT_ROWS:kv * C_VT_ROWS + C_HEAD_DIM, rows] = val.T.astype(BF16)
                    vt_ref[kv * C_VT_ROWS + C_HEAD_DIM:(kv + 1) * C_VT_ROWS, rows] = jnp.ones(
                        (C_VT_ROWS - C_HEAD_DIM, hm), BF16)
                continue
            if col < 2 * D_MODEL:
                val = _rope128(val, cos, sin, low)
            sub = hm // dil
            dst = slice(hf * sub, (hf + 1) * sub)
            if dil == 1:
                o_ref[0, 0, dst, col:col + LANES] = val.astype(BF16)
                continue
            by_residue = jnp.swapaxes(val.reshape(sub, dil, LANES), 0, 1)
            o_ref[0, :, dst, col:col + LANES] = by_residue.astype(BF16)

    hs = [normed(hf) for hf in range(halves)]
    for hf in range(halves):
        for j in range(n // tn):
            emit(hf, j, jnp.dot(hs[hf], w_ref[:, j * tn:(j + 1) * tn], preferred_element_type=F32))


def _qkv_proj(x2d, mods3, norm_g3, w, *, layer, mod_row0, seq, mode, cos=None, sin=None, qg=None, kg=None,
              group=0, dil=1):
    t = x2d.shape[0]
    tm = TM_QKV
    if mode == "b":
        n, col0 = 3 * D_MODEL, group
    else:
        n, col0 = w.shape[1], 0
    tn = TN_QKV_C if mode == "c" else TN_QKV
    hm = tm // QKV_HALVES
    per_seq = seq // tm
    in_specs = [
        pl.BlockSpec((tm, D_MODEL), lambda i: (i, 0)),
        pl.BlockSpec((1, 1, 6 * D_MODEL), lambda i: (layer * MOD_ROWS + mod_row0 + i // per_seq, 0, 0)),
        pl.BlockSpec((1, 1, D_MODEL), lambda i: (2 * layer, 0, 0)),
        pl.BlockSpec((D_MODEL, n), lambda i: (0, col0), pipeline_mode=pl.Buffered(1)),
    ]
    args = [x2d, mods3, norm_g3, w]
    if mode in ("b", "c"):
        in_specs += [pl.BlockSpec((tm, LANES), lambda i: (i % per_seq, 0))] * 2
        args += [cos, sin]
    if mode == "c":
        in_specs += [pl.BlockSpec((1, C_HEAD_DIM), lambda i: (0, 0))] * 2
        args += [qg, kg]
    scratch = []
    if mode == "b":
        assert hm % (dil * BF16_SUBLANES) == 0
        batch = t // seq
        out_spec = pl.BlockSpec((1, dil, tm // dil, n), lambda i: (i // per_seq, 0, i % per_seq, 0))
        out_shape = jax.ShapeDtypeStruct((batch, dil, seq // dil, n), BF16)
    elif mode == "c":
        assert tm == C_TK
        n_qk = (C_Q_HEADS + C_KV_HEADS) * C_HEAD_DIM
        kv_w = C_KV_HEADS * C_VT_ROWS
        out_spec = [pl.BlockSpec((tm, n_qk), lambda i: (i, 0)),
                    pl.BlockSpec((None, None, kv_w, tm), lambda i: (i // per_seq, i % per_seq, 0, 0))]
        out_shape = [jax.ShapeDtypeStruct((t, n_qk), BF16),
                     jax.ShapeDtypeStruct((t // seq, per_seq, kv_w, tm), BF16)]
    else:
        out_spec = pl.BlockSpec((tm, n), lambda i: (i, 0))
        out_shape = jax.ShapeDtypeStruct((t, n), BF16)
    return pl.pallas_call(
        functools.partial(_qkv_kernel, mode=mode, dil=dil, halves=QKV_HALVES, tn=tn),
        grid=(t // tm,),
        in_specs=in_specs,
        out_specs=out_spec,
        out_shape=out_shape,
        scratch_shapes=scratch,
        compiler_params=_cparams(("parallel",)),
        name="norm_qkv_" + mode,
    )(*args)


def _pair_masks():
    lane = lax.broadcasted_iota(jnp.int32, (1, LANES), 1)
    first = lane < HEAD_DIM_AB
    return first, jnp.logical_not(first)


def _dot_nt(a, b):
    return lax.dot_general(a, b, (((1,), (1,)), ((), ())), preferred_element_type=F32)


def _pipelined(units, scores, finish, ahead=1):
    pending = [scores(u) for u in units[:ahead]]
    for i, unit in enumerate(units):
        if i + ahead < len(units):
            pending.append(scores(units[i + ahead]))
        finish(unit, pending.pop(0))


def _attn_a_kernel(q_ref, kbuf, vbuf, bias_ref, o_ref):
    tq = q_ref.shape[0]
    rows_per = tq // GRID_W
    half = A_WIN_ROWS // 2
    win_keys = A_WIN_ROWS * GRID_W
    first = pl.program_id(1) == 0
    last = pl.program_id(1) == pl.num_programs(1) - 1
    masks = _pair_masks()
    units = [(p, rl) for p in range(HEADS_AB // 2) for rl in range(rows_per)]

    def window_of(rl):
        at_start, at_end = max(rl - half, 0), min(rl + half, rows_per)
        start = jnp.where(first, at_start, jnp.where(last, at_end, rl))
        win = jnp.where(first, at_start - rl + A_WIN_ROWS - 1,
                        jnp.where(last, at_end - rl - 1, A_WIN_ROWS - 1 - half))
        return pl.multiple_of(start * GRID_W, GRID_W), win

    def scores(unit):
        p, rl = unit
        sl = slice(p * LANES, (p + 1) * LANES)
        start, win = window_of(rl)
        qp = q_ref[rl * GRID_W:(rl + 1) * GRID_W, sl]
        stacked = jnp.concatenate([jnp.where(mk, qp, jnp.zeros_like(qp)) for mk in masks], axis=0)
        s = _dot_nt(stacked, kbuf[pl.ds(start, win_keys), sl]) + bias_ref[p, win]
        return s, s.max(axis=-1, keepdims=True)

    def finish(unit, scored):
        s, m = scored
        p, rl = unit
        sl = slice(p * LANES, (p + 1) * LANES)
        start, _ = window_of(rl)
        pr = jnp.exp2(s - m)
        l = pr.sum(axis=-1, keepdims=True)
        o = jnp.dot(pr.astype(BF16), vbuf[pl.ds(start, win_keys), sl], preferred_element_type=F32) / l
        o_ref[rl * GRID_W:(rl + 1) * GRID_W, sl] = jnp.where(masks[0], o[:GRID_W], o[GRID_W:]).astype(BF16)

    _pipelined(units, scores, finish, ahead=A_AHEAD)


def _attn_a_bias(rpb):
    n_dc = 2 * A_WIN_COLS - 1
    c, kc = np.arange(GRID_W)[:, None], np.arange(GRID_W)[None, :]
    onehot_c = ((kc - c + A_WIN_COLS - 1)[..., None] == np.arange(n_dc)).astype(np.float32)
    toe = jnp.einsum("hde,cje->hdcj", rpb * LOG2E, onehot_c, precision=lax.Precision.HIGHEST)
    cs = np.clip(c - A_WIN_COLS // 2, 0, GRID_W - A_WIN_COLS)
    col_ok = (kc >= cs) & (kc < cs + A_WIN_COLS)
    toe = jnp.where(jnp.asarray(col_ok)[None, None], toe, MASK_NEG)
    wins = jnp.stack([jnp.concatenate([toe[:, w + m] for m in range(A_WIN_ROWS)], axis=-1)
                      for w in range(A_WIN_ROWS)], axis=1)
    wins = wins.reshape(HEADS_AB // 2, 2, A_WIN_ROWS, GRID_W, A_WIN_ROWS * GRID_W)
    return jnp.transpose(wins, (0, 2, 1, 3, 4)).reshape(HEADS_AB // 2, A_WIN_ROWS, 2 * GRID_W, A_WIN_ROWS * GRID_W)


def _attn_a(qkv, bias, *, batch, seq):
    tq = A_TQ
    nb = seq // tq
    rows_per = tq // GRID_W
    side = (A_WIN_ROWS // 2) * GRID_W
    buf = tq + 2 * side
    assert nb >= 2 and rows_per >= A_WIN_ROWS // 2 and tq % side == 0 and seq >= buf
    q_spec = pl.BlockSpec((tq, D_MODEL), lambda b, i: (b * nb + i, 0))
    def kv_spec(col):
        def index(b, i):
            row = b * seq + jnp.clip(i * tq - side, 0, seq - buf)
            return pl.multiple_of(row, side), col * D_MODEL
        return pl.BlockSpec((pl.Element(buf), pl.Element(D_MODEL)), index)
    bias_spec = pl.BlockSpec(bias.shape, lambda b, i: (0, 0, 0, 0), pipeline_mode=pl.Buffered(1))
    return pl.pallas_call(
        _attn_a_kernel,
        grid=(batch, nb),
        in_specs=[q_spec, kv_spec(1), kv_spec(2), bias_spec],
        out_specs=pl.BlockSpec((tq, D_MODEL), lambda b, i: (b * nb + i, 0)),
        out_shape=jax.ShapeDtypeStruct((batch * seq, D_MODEL), BF16),
        compiler_params=_cparams(("parallel", "arbitrary")),
        name="attn_neighbourhood",
    )(qkv, qkv, qkv, bias)


def _attn_b_kernel(q_ref, kp_ref, kc_ref, kn_ref, vp_ref, vc_ref, vn_ref, mask_ref, o_ref, lse_ref):
    tq = q_ref.shape[0]
    masks = _pair_masks()
    lane = lax.broadcasted_iota(jnp.int32, (1, LANES), 1)
    lse_tiles = [jnp.zeros((B_HALF, LANES), F32) for _ in range(tq // B_HALF)]
    units = [(p, hf) for p in range(HEADS_AB // 2) for hf in range(tq // B_HALF)]

    def window(prev_ref, cur_ref, next_ref, hf, sl):
        lo, hi = hf * B_HALF - B_SIDE, (hf + 1) * B_HALF + B_SIDE
        parts = []
        if lo < 0:
            parts.append(prev_ref[tq + lo:, sl])
        parts.append(cur_ref[max(lo, 0):min(hi, tq), sl])
        if hi > tq:
            parts.append(next_ref[:hi - tq, sl])
        return jnp.concatenate(parts, axis=0)

    def scores(unit):
        p, hf = unit
        sl = slice(p * LANES, (p + 1) * LANES)
        qp = q_ref[hf * B_HALF:(hf + 1) * B_HALF, sl]
        stacked = jnp.concatenate([jnp.where(mk, qp, jnp.zeros_like(qp)) for mk in masks], axis=0)
        s = _dot_nt(stacked, window(kp_ref, kc_ref, kn_ref, hf, sl)) + mask_ref[0, hf]
        return s, s.max(axis=-1, keepdims=True)

    def finish(unit, scored):
        s, m = scored
        p, hf = unit
        sl = slice(p * LANES, (p + 1) * LANES)
        pr = jnp.exp2(s - m)
        l = pr.sum(axis=-1, keepdims=True)
        o = jnp.dot(pr.astype(BF16), window(vp_ref, vc_ref, vn_ref, hf, sl), preferred_element_type=F32) / l
        o_ref[hf * B_HALF:(hf + 1) * B_HALF, sl] = jnp.where(masks[0], o[:B_HALF], o[B_HALF:]).astype(BF16)
        lse = m + jnp.log2(l)
        tile = jnp.where(lane == 2 * p, lse[:B_HALF], lse_tiles[hf])
        lse_tiles[hf] = jnp.where(lane == 2 * p + 1, lse[B_HALF:], tile)

    _pipelined(units, scores, finish, ahead=B_AHEAD)
    for hf, tile in enumerate(lse_tiles):
        lse_ref[hf * B_HALF:(hf + 1) * B_HALF, :] = tile


def _attn_b_masks():
    qq = np.arange(B_HALF)[:, None]
    jj = np.arange(B_HALF + 2 * B_SIDE)[None, :]
    band = (jj - qq >= 0) & (jj - qq <= 2 * B_SIDE)
    out = []
    for ty in range(4):
        per_half = []
        for hf in range(B_TQ // B_HALF):
            pos = hf * B_HALF - B_SIDE + jj
            ok = band
            if ty & 1:
                ok = ok & (pos >= 0)
            if ty & 2:
                ok = ok & (pos < B_TQ)
            half = np.where(ok, 0.0, MASK_NEG)
            per_half.append(np.concatenate([half, half], axis=0))
        out.append(np.stack(per_half))
    return np.stack(out).astype(np.float32)


def _attn_b_group(qkv, masks, *, group):
    batch, dil, sub, _ = qkv.shape
    tq = B_TQ
    nb = sub // tq
    assert sub % tq == 0
    q_spec = pl.BlockSpec((None, None, tq, D_MODEL), lambda b, r, i: (b, r, i, 0))
    def kv_spec(which, off):
        return pl.BlockSpec((None, None, tq, D_MODEL),
                            lambda b, r, i: (b, r, jnp.clip(i + off, 0, nb - 1), which))
    mask_spec = pl.BlockSpec((1,) + masks.shape[1:],
                             lambda b, r, i: ((i == 0).astype(jnp.int32) + 2 * (i == nb - 1).astype(jnp.int32),
                                              0, 0, 0))
    return pl.pallas_call(
        _attn_b_kernel,
        grid=(batch, dil, nb),
        in_specs=[q_spec, kv_spec(1, -1), kv_spec(1, 0), kv_spec(1, 1),
                  kv_spec(2, -1), kv_spec(2, 0), kv_spec(2, 1), mask_spec],
        out_specs=[pl.BlockSpec((None, None, tq, D_MODEL), lambda b, r, i: (b, r, i, 0)),
                   pl.BlockSpec((None, None, tq, LANES), lambda b, r, i: (b, r, i, 0))],
        out_shape=[jax.ShapeDtypeStruct((batch, dil, sub, D_MODEL), BF16),
                   jax.ShapeDtypeStruct((batch, dil, sub, LANES), F32)],
        compiler_params=_cparams(("parallel", "parallel", "arbitrary")),
        name="attn_dilated_g%d" % group,
    )(qkv, qkv, qkv, qkv, qkv, qkv, qkv, masks)


def _attn_c_kernel(q_ref, k_ref, vt_ref, o_ref, s_ref, smax_ref, m_ref, acc_ref, *, tk, unroll):
    nk = k_ref.shape[0] // tk
    m_ref[...] = jnp.full(m_ref.shape, MASK_NEG, F32)
    acc_ref[...] = jnp.zeros(acc_ref.shape, F32)

    def issue_scores(h, c, slot):
        start = pl.multiple_of(c * tk, tk)
        s = _dot_nt(k_ref[pl.ds(start, tk), :], q_ref[:, h * C_HEAD_DIM:(h + 1) * C_HEAD_DIM])
        s_ref[slot] = s
        smax_ref[slot] = s.max(axis=0, keepdims=True)

    issue_scores(0, 0, 0)

    def chunk(c, carry):
        vt = vt_ref[c]
        for h in range(C_GROUP):
            if h + 1 < C_GROUP:
                issue_scores(h + 1, c, (h + 1) % 2)
            else:
                issue_scores(0, jnp.minimum(c + 1, nk - 1), (h + 1) % 2)
            s = s_ref[h % 2]
            m_prev = m_ref[h]
            m_new = jnp.maximum(m_prev, smax_ref[h % 2])
            alpha = jnp.exp2(m_prev - m_new)
            pr = jnp.exp2(s - m_new)
            acc_ref[h] = alpha * acc_ref[h] + jnp.dot(vt, pr.astype(BF16), preferred_element_type=F32)
            m_ref[h] = m_new
        return carry

    def unrolled_chunks(cu, carry):
        for u in range(unroll):
            chunk(unroll * cu + u, carry)
        return carry

    lax.fori_loop(0, nk // unroll, unrolled_chunks, 0)
    for h in range(C_GROUP):
        acc = acc_ref[h]
        out = acc[:C_HEAD_DIM] / acc[C_HEAD_DIM:C_HEAD_DIM + 1]
        o_ref[:, h * C_HEAD_DIM:(h + 1) * C_HEAD_DIM] = out.T.astype(BF16)


def _attn_c(qk, vt, *, batch, seq):
    tq, tk = C_TQ, C_TK
    nq = seq // tq
    assert seq % tk == 0 and C_GROUP % 2 == 0
    nk = seq // tk
    unroll = C_UNROLL if nk % C_UNROLL == 0 and nk >= 2 * C_UNROLL else 1
    qw = C_GROUP * C_HEAD_DIM
    return pl.pallas_call(
        functools.partial(_attn_c_kernel, tk=tk, unroll=unroll),
        grid=(batch, C_KV_HEADS, nq),
        in_specs=[
            pl.BlockSpec((tq, qw), lambda b, g, i: (b * nq + i, g)),
            pl.BlockSpec((seq, C_HEAD_DIM), lambda b, g, i: (b, C_Q_HEADS + g)),
            pl.BlockSpec((None, seq // tk, C_VT_ROWS, tk), lambda b, g, i: (b, 0, g, 0)),
        ],
        out_specs=pl.BlockSpec((tq, qw), lambda b, g, i: (b * nq + i, g)),
        out_shape=jax.ShapeDtypeStruct((batch * seq, C_Q_HEADS * C_HEAD_DIM), BF16),
        scratch_shapes=[pltpu.VMEM((2, tk, tq), F32), pltpu.VMEM((2, 1, tq), F32),
                        pltpu.VMEM((C_GROUP, 1, tq), F32),
                        pltpu.VMEM((C_GROUP, C_VT_ROWS, tq), F32)],
        compiler_params=_cparams(("parallel", "parallel", "arbitrary")),
        name="attn_gqa_flash",
    )(qk, qk, vt)


def _post_kernel(*refs, merge, final, halves, tf, lead):
    refs = list(refs)
    x_ref = refs.pop(0)
    if merge:
        o_refs = [refs.pop(0) for _ in range(3)]
        lse_refs = [refs.pop(0) for _ in range(3)]
    else:
        o_ref_in = refs.pop(0)
    mod_ref, g_ref, wo_ref, w1_ref, w2_ref = [refs.pop(0) for _ in range(5)]
    fg_ref = refs.pop(0) if final else None
    out_ref = refs.pop(0)
    mod = mod_ref[0]
    hm = x_ref.shape[0] // halves

    def token_order(ref, hf, c):
        dil = ref.shape[0]
        rows = hm // dil
        part = ref[:, hf * rows:(hf + 1) * rows, c * LANES:(c + 1) * LANES].astype(F32)
        return part[0] if dil == 1 else jnp.swapaxes(part, 0, 1).reshape(hm, LANES)

    def merged_groups(hf):
        lse = [token_order(r, hf, 0) for r in lse_refs]
        top = jnp.maximum(jnp.maximum(lse[0], lse[1]), lse[2])
        ex = [jnp.exp2(v - top) for v in lse]
        den = ex[0] + ex[1] + ex[2]
        wgt = [e / den for e in ex]
        lane = lax.broadcasted_iota(jnp.int32, (hm, LANES), 1)
        cols = []
        for c in range(D_MODEL // LANES):
            head_of_lane = lane // HEAD_DIM_AB + c * (LANES // HEAD_DIM_AB)
            mixed = None
            for g in range(3):
                term = jnp.take_along_axis(wgt[g], head_of_lane, axis=1) * token_order(o_refs[g], hf, c)
                mixed = term if mixed is None else mixed + term
            cols.append(mixed.astype(BF16))
        return jnp.concatenate(cols, axis=1)

    def pre(hf):
        rows = slice(hf * hm, (hf + 1) * hm)
        o = merged_groups(hf) if merge else o_ref_in[rows, :]
        mix = jnp.dot(o, wo_ref[...], preferred_element_type=F32)
        x1 = x_ref[rows, :] + mod[:, 2 * D_MODEL:3 * D_MODEL] * mix
        h2 = _norm_mod(x1, g_ref[0], mod[:, 4 * D_MODEL:5 * D_MODEL], mod[:, 3 * D_MODEL:4 * D_MODEL])
        return x1, h2.astype(BF16)

    def mlp(h2, acc, chunks):
        for c in chunks:
            a = jnp.dot(h2, w1_ref[:, c * tf:(c + 1) * tf], preferred_element_type=F32)
            a = jnp.square(jnp.maximum(a, 0.0)).astype(BF16)
            d = jnp.dot(a, w2_ref[c * tf:(c + 1) * tf, :], preferred_element_type=F32)
            acc = d if acc is None else acc + d
        return acc

    def finish(hf, x1, acc):
        x2 = x1 + mod[:, 5 * D_MODEL:6 * D_MODEL] * acc
        if final:
            ms = jnp.mean(x2 * x2, axis=-1, keepdims=True)
            x2 = x2 * lax.rsqrt(ms + EPS) * fg_ref[...]
        out_ref[hf * hm:(hf + 1) * hm, :] = x2

    n_chunks = D_FF // tf
    cur = pre(0)
    for hf in range(halves):
        x1, h2 = cur
        acc = mlp(h2, None, range(0, lead))
        if hf + 1 < halves:
            cur = pre(hf + 1)
        finish(hf, x1, mlp(h2, acc, range(lead, n_chunks)))


def _post(x2d, attn, mods3, norm_g3, wo, w1, w2, final_g, *, layer, mod_row0, seq, merge, final):
    t = x2d.shape[0]
    tm = TM_POST_MERGE if merge else TM_POST
    halves = POST_HALVES_MERGE if merge else POST_HALVES
    hm = tm // halves
    per_seq = seq // tm
    row = lambda i: (i, 0)
    const = lambda i: (0, 0)
    resident = pl.Buffered(1)
    in_specs = [pl.BlockSpec((tm, D_MODEL), row)]
    args = [x2d]
    scratch = []
    if merge:
        outs, lses = attn
        def res_spec(arr):
            dil, width = arr.shape[1], arr.shape[3]
            assert hm % (dil * BF16_SUBLANES) == 0
            return pl.BlockSpec((None, dil, tm // dil, width), lambda i: (i // per_seq, 0, i % per_seq, 0))
        in_specs += [res_spec(a) for a in outs] + [res_spec(a) for a in lses]
        args += list(outs) + list(lses)
    else:
        in_specs += [pl.BlockSpec((tm, D_MODEL), row)]
        args += [attn]
    in_specs += [
        pl.BlockSpec((1, 1, 6 * D_MODEL), lambda i: (layer * MOD_ROWS + mod_row0 + i // per_seq, 0, 0)),
        pl.BlockSpec((1, 1, D_MODEL), lambda i: (2 * layer + 1, 0, 0)),
        pl.BlockSpec((D_MODEL, D_MODEL), const, pipeline_mode=resident),
        pl.BlockSpec((D_MODEL, D_FF), const, pipeline_mode=resident),
        pl.BlockSpec((D_FF, D_MODEL), const, pipeline_mode=resident),
    ]
    args += [mods3, norm_g3, wo, w1, w2]
    if final:
        in_specs += [pl.BlockSpec((1, D_MODEL), const)]
        args += [final_g.reshape(1, D_MODEL)]
    return pl.pallas_call(
        functools.partial(_post_kernel, merge=merge, final=final, halves=halves, tf=TF_POST,
                          lead=POST_LEAD_MERGE if merge else POST_LEAD),
        grid=(t // tm,),
        in_specs=in_specs,
        out_specs=pl.BlockSpec((tm, D_MODEL), row),
        out_shape=jax.ShapeDtypeStruct((t, D_MODEL), F32),
        scratch_shapes=scratch,
        compiler_params=_cparams(("parallel",)),
        name="wo_mlp_merge" if merge else "wo_mlp",
    )(*args)


def _trunk(x, mods3, mod_row0, norm_g3, final_g, wts, a_bias, b_masks, tabs_b, tabs_c):
    batch, seq, _ = x.shape
    assert seq <= tabs_b[0].shape[0]
    x2d = x.reshape(batch * seq, D_MODEL)
    for layer in range(DEPTH):
        kind, j = layer % N_MIXERS, layer // N_MIXERS
        common = dict(layer=layer, mod_row0=mod_row0, seq=seq)
        if kind == 0:
            qkv = _qkv_proj(x2d, mods3, norm_g3, wts["a_qkv"][j], mode="a", **common)
            attn = _attn_a(qkv, a_bias[j], batch=batch, seq=seq)
            wo = wts["a_o"][j]
        elif kind == 1:
            groups = []
            for g, (win, dil) in enumerate(B_PAIRS):
                assert win == 2 * B_SIDE * dil
                qkv = _qkv_proj(x2d, mods3, norm_g3, wts["b_qkv"][j], mode="b", cos=tabs_b[0], sin=tabs_b[1],
                                group=g, dil=dil, **common)
                groups.append(_attn_b_group(qkv, b_masks, group=g))
            attn = ([o for o, _ in groups], [l for _, l in groups])
            wo = wts["b_o"][j]
        else:
            qk, vt = _qkv_proj(x2d, mods3, norm_g3, wts["c_qkv"][j], mode="c", cos=tabs_c[0], sin=tabs_c[1],
                               qg=wts["c_qg"][j], kg=wts["c_kg"][j], **common)
            attn = _attn_c(qk, vt, batch=batch, seq=seq)
            wo = wts["c_o"][j]
        x2d = _post(x2d, attn, mods3, norm_g3, wo, wts["w1"][layer], wts["w2"][layer], final_g,
                    merge=(kind == 1), final=(layer == DEPTH - 1), **common)
    return x2d.reshape(batch, seq, D_MODEL)


def kernel(x_prompt, x_sample, c_prompt, c_sample, w_mod, b_mod, norm_g, final_g, a_w_qkv, a_rpb, a_w_o,
           b_w_qkv, b_w_o, c_w_qkv, c_q_g, c_k_g, c_w_o, mlp_w1, mlp_w2):
    nb_p, nb_s = c_prompt.shape[0], c_sample.shape[0]
    assert nb_p + nb_s <= MOD_ROWS
    c_all = jnp.concatenate([c_prompt, c_sample, jnp.zeros((MOD_ROWS - nb_p - nb_s, D_MODEL), F32)], axis=0)
    mods3 = _modulation(c_all, w_mod, b_mod).reshape(DEPTH * MOD_ROWS, 1, 6 * D_MODEL)
    norm_g3 = norm_g.reshape(DEPTH * 2, 1, D_MODEL)

    scale_ab = HEAD_DIM_AB ** -0.5 * LOG2E
    col_scale = np.ones((3 * D_MODEL,), np.float32)
    col_scale[:D_MODEL] = scale_ab
    a_qkv = (a_w_qkv * col_scale).astype(BF16)
    b_qkv = (b_w_qkv * np.tile(col_scale, len(B_PAIRS))).astype(BF16)
    wts = {
        "a_qkv": a_qkv, "a_o": a_w_o.astype(BF16),
        "b_qkv": b_qkv, "b_o": b_w_o.astype(BF16),
        "c_qkv": c_w_qkv.astype(BF16), "c_o": c_w_o.astype(BF16),
        "c_qg": (c_q_g * (C_HEAD_DIM ** -0.5 * LOG2E)).reshape(-1, 1, C_HEAD_DIM),
        "c_kg": c_k_g.reshape(-1, 1, C_HEAD_DIM),
        "w1": mlp_w1.astype(BF16), "w2": mlp_w2.astype(BF16),
    }
    a_bias = [_attn_a_bias(a_rpb[j]) for j in range(a_rpb.shape[0])]
    b_masks = jnp.asarray(_attn_b_masks())

    tabs_b, tabs_c = _rope_tables(max(x_prompt.shape[1], x_sample.shape[1]))
    shared = (norm_g3, final_g, wts, a_bias, b_masks, tabs_b, tabs_c)
    y_prompt = _trunk(x_prompt, mods3, 0, *shared)
    y_sample = _trunk(x_sample, mods3, nb_p, *shared)
    return (y_prompt, y_sample)
```
